```python
import jax, jax.numpy as jnp
from jax import lax
import numpy as np

D_MODEL = 4096
BATCH = 4
SEQ = 4096
DEPTH = 2

HEAD_DIM = 128
CHUNK = 128
WINDOW = 128
ROPE_THETA = 10000.0
NORM_EPS = 1e-5
NEG_INF = -1e30
D_FF = 4 * D_MODEL
MIX_WIDTH = D_MODEL
SGU_WIDTH = MIX_WIDTH // 4
RET_WIDTH = MIX_WIDTH // 4
ATT_WIDTH = MIX_WIDTH - SGU_WIDTH - RET_WIDTH
N_SGU_GROUPS = SGU_WIDTH // HEAD_DIM
N_RET_HEADS = RET_WIDTH // HEAD_DIM
N_Q_HEADS = ATT_WIDTH // HEAD_DIM
N_KV_HEADS = N_Q_HEADS // 4
KV_WIDTH = N_KV_HEADS * HEAD_DIM
IN_WIDTHS = (SGU_WIDTH, SGU_WIDTH, RET_WIDTH, RET_WIDTH, RET_WIDTH, RET_WIDTH,
             ATT_WIDTH, KV_WIDTH, KV_WIDTH)
IN_WIDTH = sum(IN_WIDTHS)

kernel_name = "hybrid_sgu_retention_swa_encoder"


def rms_norm(x, g):
    xf = x.astype(jnp.float32)
    y = xf * lax.rsqrt(jnp.mean(xf * xf, axis=-1, keepdims=True) + NORM_EPS)
    return (y * g.astype(jnp.float32)).astype(x.dtype)


def rope_tables(seq, dtype):
    pos = jnp.arange(seq, dtype=jnp.float32)
    inv = ROPE_THETA ** (-jnp.arange(0, HEAD_DIM, 2, dtype=jnp.float32) / HEAD_DIM)
    ang = pos[:, None] * inv[None, :]
    ang = jnp.concatenate([ang, ang], axis=-1)
    return jnp.cos(ang).astype(dtype), jnp.sin(ang).astype(dtype)


def apply_rope(t, cos, sin):
    t1, t2 = jnp.split(t, 2, axis=-1)
    rot = jnp.concatenate([-t2, t1], axis=-1)
    return t * cos[None, :, None, :] + rot * sin[None, :, None, :]


def spatial_gating(u, v, ln_g, ln_b, w_s, b_s):
    B, S, _ = u.shape
    u = jax.nn.gelu(u, approximate=False)
    v = jax.nn.gelu(v, approximate=False)
    vf = v.astype(jnp.float32)
    mu = jnp.mean(vf, axis=-1, keepdims=True)
    var = jnp.mean(jnp.square(vf - mu), axis=-1, keepdims=True)
    vn = ((vf - mu) * lax.rsqrt(var + NORM_EPS) * ln_g.astype(jnp.float32)
          + ln_b.astype(jnp.float32)).astype(v.dtype)
    vc = vn.reshape(B, S // CHUNK, CHUNK, N_SGU_GROUPS, HEAD_DIM)
    s = jnp.einsum('gij,bnjgd->bnigd', w_s, vc) + b_s.T[None, None, :, :, None]
    return u * s.reshape(B, S, SGU_WIDTH)


def retention_direction(q, k, v, log_gamma, include_diag):
    B, S, H, D = q.shape
    N = S // CHUNK
    dt = q.dtype
    qc = q.reshape(B, N, CHUNK, H, D)
    kc = k.reshape(B, N, CHUNK, H, D)
    vc = v.reshape(B, N, CHUNK, H, D)
    idx = jnp.arange(CHUNK, dtype=jnp.float32)
    delta = idx[:, None] - idx[None, :]
    mask = (delta >= 0) if include_diag else (delta > 0)
    decay_in = jnp.where(mask[None],
                         jnp.exp(log_gamma[:, None, None] * jnp.maximum(delta, 0.0)[None]),
                         0.0)
    scores = jnp.einsum('bnihd,bnjhd->bnhij', qc, kc) * decay_in.astype(dt)[None, None]
    inner = jnp.einsum('bnhij,bnjhd->bnihd', scores, vc)
    w_k = jnp.exp(log_gamma[None, :] * (CHUNK - 1 - idx)[:, None]).astype(dt)
    kv = jnp.einsum('bnjhd,bnjhe->nbhde', kc * w_k[None, None, :, :, None], vc)
    kv = kv.astype(jnp.float32)
    chunk_decay = jnp.exp(log_gamma * CHUNK)[None, :, None, None]

    def step(state, kv_n):
        return chunk_decay * state + kv_n, state

    _, prev = lax.scan(step, jnp.zeros((B, H, D, D), jnp.float32), kv)
    w_q = jnp.exp(log_gamma[None, :] * (idx + 1.0)[:, None]).astype(dt)
    cross = jnp.einsum('bnihd,nbhde->bnihe', qc * w_q[None, None, :, :, None], prev.astype(dt))
    return (inner + cross).reshape(B, S, H, D)


def retention_mixer(q, k, v, g, log_decay_raw, cos, sin):
    B, S, H, D = q.shape
    q = apply_rope(q, cos, sin)
    k = apply_rope(k, cos, sin) * (D ** -0.5)
    log_gamma = -jnp.exp(log_decay_raw.astype(jnp.float32))
    fwd = retention_direction(q, k, v, log_gamma[0], True)
    bwd = jnp.flip(retention_direction(jnp.flip(q, 1), jnp.flip(k, 1), jnp.flip(v, 1),
                                       log_gamma[1], False), 1)
    rf = (fwd + bwd).astype(jnp.float32)
    rn = rf * lax.rsqrt(jnp.mean(rf * rf, axis=-1, keepdims=True) + NORM_EPS)
    return jax.nn.silu(g) * rn.reshape(B, S, H * D).astype(g.dtype)


def window_attention(q, k, v, sink, cos, sin):
    B, S, _, D = q.shape
    N = S // CHUNK
    G = N_Q_HEADS // N_KV_HEADS
    q = apply_rope(q, cos, sin) * (D ** -0.5)
    k = apply_rope(k, cos, sin)
    qb = q.reshape(B, N, CHUNK, N_KV_HEADS, G, D)

    def neighbours(t):
        tp = jnp.pad(t, ((0, 0), (CHUNK, CHUNK), (0, 0), (0, 0)))
        tp = tp.reshape(B, N + 2, CHUNK, N_KV_HEADS, D)
        return jnp.concatenate([tp[:, :-2], tp[:, 1:-1], tp[:, 2:]], axis=2)

    kb, vb = neighbours(k), neighbours(v)
    s = jnp.einsum('bnikgd,bnjkd->bnkgij', qb, kb).astype(jnp.float32)
    qi = jnp.arange(CHUNK)
    kj = jnp.arange(3 * CHUNK)
    rel = kj[None, :] - CHUNK - qi[:, None]
    kpos = (jnp.arange(N)[:, None] - 1) * CHUNK + kj[None, :]
    valid = (jnp.abs(rel) <= WINDOW)[None] & ((kpos >= 0) & (kpos < S))[:, None, :]
    s = jnp.where(valid[None, :, None, None], s, NEG_INF)
    sink_f = sink.astype(jnp.float32).reshape(N_KV_HEADS, G)[None, None, :, :, None, None]
    m = jnp.maximum(jnp.max(s, axis=-1, keepdims=True), sink_f)
    p = jnp.exp(s - m)
    p = p / (jnp.sum(p, axis=-1, keepdims=True) + jnp.exp(sink_f - m))
    o = jnp.einsum('bnkgij,bnjkd->bnikgd', p.astype(v.dtype), vb)
    return o.reshape(B, S, ATT_WIDTH)


def hybrid_layer(x, ln_mix_g, w_in, sgu_ln_g, sgu_ln_b, sgu_w, sgu_b, ret_log_decay,
                 attn_sink, w_out, ln_mlp_g, w_up, w_down, cos, sin):
    B, S, _ = x.shape
    h = rms_norm(x, ln_mix_g)
    proj = h @ w_in
    splits = [int(o) for o in np.cumsum(IN_WIDTHS)[:-1]]
    u, v, rq, rk, rv, rg, aq, ak, av = jnp.split(proj, splits, axis=-1)
    a_out = spatial_gating(u, v, sgu_ln_g, sgu_ln_b, sgu_w, sgu_b)
    r_out = retention_mixer(rq.reshape(B, S, N_RET_HEADS, HEAD_DIM),
                            rk.reshape(B, S, N_RET_HEADS, HEAD_DIM),
                            rv.reshape(B, S, N_RET_HEADS, HEAD_DIM),
                            rg, ret_log_decay, cos, sin)
    c_out = window_attention(aq.reshape(B, S, N_Q_HEADS, HEAD_DIM),
                             ak.reshape(B, S, N_KV_HEADS, HEAD_DIM),
                             av.reshape(B, S, N_KV_HEADS, HEAD_DIM),
                             attn_sink, cos, sin)
    x = x + jnp.concatenate([a_out, r_out, c_out], axis=-1) @ w_out
    h = rms_norm(x, ln_mlp_g)
    x = x + jnp.square(jax.nn.relu(h @ w_up)) @ w_down
    return x


def setup_inputs(seed: int = 0) -> dict:
    key = jax.random.key(seed)
    ks = jax.random.split(key, 16)
    f32 = jnp.float32
    nrm = lambda k, shape: jax.random.normal(k, shape, dtype=f32)
    x = nrm(ks[0], (BATCH, SEQ, D_MODEL))
    ln_mix_g = 1.0 + 0.02 * nrm(ks[1], (DEPTH, D_MODEL))
    w_in = nrm(ks[2], (DEPTH, D_MODEL, IN_WIDTH)) * (D_MODEL ** -0.5)
    sgu_ln_g = 1.0 + 0.02 * nrm(ks[3], (DEPTH, SGU_WIDTH))
    sgu_ln_b = 0.02 * nrm(ks[4], (DEPTH, SGU_WIDTH))
    sgu_w = nrm(ks[5], (DEPTH, N_SGU_GROUPS, CHUNK, CHUNK)) * (CHUNK ** -0.5)
    sgu_b = 1.0 + 0.02 * nrm(ks[6], (DEPTH, N_SGU_GROUPS, CHUNK))
    p = 2.0 ** (-5.0 - jnp.arange(N_RET_HEADS, dtype=f32))
    base = jnp.log(-jnp.log1p(-p))
    ret_log_decay = base[None, None, :] + 0.1 * nrm(ks[7], (DEPTH, 2, N_RET_HEADS))
    attn_sink = nrm(ks[8], (DEPTH, N_Q_HEADS))
    w_out = nrm(ks[9], (DEPTH, MIX_WIDTH, D_MODEL)) * (MIX_WIDTH ** -0.5)
    ln_mlp_g = 1.0 + 0.02 * nrm(ks[10], (DEPTH, D_MODEL))
    w_up = nrm(ks[11], (DEPTH, D_MODEL, D_FF)) * (D_MODEL ** -0.5)
    w_down = nrm(ks[12], (DEPTH, D_FF, D_MODEL)) * (D_FF ** -0.5)
    final_norm_g = 1.0 + 0.02 * nrm(ks[13], (D_MODEL,))
    return {"x": x, "ln_mix_g": ln_mix_g, "w_in": w_in, "sgu_ln_g": sgu_ln_g,
            "sgu_ln_b": sgu_ln_b, "sgu_w": sgu_w, "sgu_b": sgu_b,
            "ret_log_decay": ret_log_decay, "attn_sink": attn_sink, "w_out": w_out,
            "ln_mlp_g": ln_mlp_g, "w_up": w_up, "w_down": w_down,
            "final_norm_g": final_norm_g}


def reference(x, ln_mix_g, w_in, sgu_ln_g, sgu_ln_b, sgu_w, sgu_b, ret_log_decay,
              attn_sink, w_out, ln_mlp_g, w_up, w_down, final_norm_g):
    cos, sin = rope_tables(x.shape[1], x.dtype)
    for l in range(DEPTH):
        x = hybrid_layer(x, ln_mix_g[l], w_in[l], sgu_ln_g[l], sgu_ln_b[l], sgu_w[l],
                         sgu_b[l], ret_log_decay[l], attn_sink[l], w_out[l],
                         ln_mlp_g[l], w_up[l], w_down[l], cos, sin)
    return rms_norm(x, final_norm_g)
```

```python
import functools

import numpy as np
import jax
import jax.numpy as jnp
from jax import lax
from jax.experimental import pallas as pl
from jax.experimental.pallas import tpu as pltpu

HEAD_DIM = 128
CHUNK = 128
WINDOW = 128
ROPE_THETA = 10000.0
NORM_EPS = 1e-5
NEG_INF = -1e30
N_SGU_GROUPS = 8
N_RET_HEADS = 8
N_Q_HEADS = 16
N_KV_HEADS = 4
Q_PER_KV = N_Q_HEADS // N_KV_HEADS

SGU_WIDTH = N_SGU_GROUPS * HEAD_DIM
RET_WIDTH = N_RET_HEADS * HEAD_DIM
ATT_WIDTH = N_Q_HEADS * HEAD_DIM
KV_WIDTH = N_KV_HEADS * HEAD_DIM
OFF_U = 0
OFF_V = OFF_U + SGU_WIDTH
OFF_RQ = OFF_V + SGU_WIDTH
OFF_RK = OFF_RQ + RET_WIDTH
OFF_RV = OFF_RK + RET_WIDTH
OFF_RG = OFF_RV + RET_WIDTH
OFF_AQ = OFF_RG + RET_WIDTH
OFF_AK = OFF_AQ + ATT_WIDTH
OFF_AV = OFF_AK + KV_WIDTH
IN_WIDTH = OFF_AV + KV_WIDTH
MIX_WIDTH = SGU_WIDTH + RET_WIDTH + ATT_WIDTH

V7X_VMEM_BYTES = 64 * 1024 * 1024
V7X_VMEM_BUDGET = 56 * 1024 * 1024

F32 = jnp.float32
BF16 = jnp.bfloat16


def _vmem_limit(est_bytes):
    assert est_bytes <= V7X_VMEM_BUDGET, est_bytes
    return min(max(int(est_bytes * 1.25), 16 << 20), V7X_VMEM_BUDGET)


def _dot(a, b):
    return jnp.dot(a, b, preferred_element_type=F32)


def _dot_nt(a, b):
    return lax.dot_general(a, b, (((1,), (1,)), ((), ())), preferred_element_type=F32)


def _dot_tn(a, b):
    return lax.dot_general(a, b, (((0,), (0,)), ((), ())), preferred_element_type=F32)


def _norm_matmul_kernel(x_ref, g_ref, w_ref, o_ref, h_ref, *, relu2):
    @pl.when(pl.program_id(1) == 0)
    def _():
        x = x_ref[...]
        ms = jnp.mean(x * x, axis=-1, keepdims=True)
        h_ref[...] = (x * lax.rsqrt(ms + NORM_EPS) * g_ref[...]).astype(BF16)

    acc = _dot(h_ref[...], w_ref[...])
    if relu2:
        acc = jnp.square(jnp.maximum(acc, 0.0))
    o_ref[...] = acc.astype(o_ref.dtype)


def _norm_matmul(x, g, w, *, relu2, out_dtype, tm=512, tn=1024):
    m, k = x.shape
    n = w.shape[1]
    assert m % tm == 0 and n % tn == 0
    est = 2 * tm * k * 4 + tm * k * 2 + 2 * k * tn * 2 + 2 * tm * tn * 4 + tm * k * 4 + tm * tn * 4
    return pl.pallas_call(
        functools.partial(_norm_matmul_kernel, relu2=relu2),
        grid=(m // tm, n // tn),
        in_specs=[
            pl.BlockSpec((tm, k), lambda i, j: (i, 0)),
            pl.BlockSpec((1, k), lambda i, j: (0, 0)),
            pl.BlockSpec((k, tn), lambda i, j: (0, j)),
        ],
        out_specs=pl.BlockSpec((tm, tn), lambda i, j: (i, j)),
        out_shape=jax.ShapeDtypeStruct((m, n), out_dtype),
        scratch_shapes=[pltpu.VMEM((tm, k), BF16)],
        compiler_params=pltpu.CompilerParams(
            dimension_semantics=("parallel", "arbitrary"),
            vmem_limit_bytes=_vmem_limit(est)),
        name="norm_matmul_relu2" if relu2 else "norm_matmul",
    )(x, g.reshape(1, k), w)


def _matmul_res_kernel(a_ref, w_ref, r_ref, o_ref, acc_ref, *, nk):
    if nk == 1:
        o_ref[...] = r_ref[...] + _dot(a_ref[...], w_ref[...])
        return
    kk = pl.program_id(2)

    @pl.when(kk == 0)
    def _():
        acc_ref[...] = r_ref[...] + _dot(a_ref[...], w_ref[...])

    @pl.when(jnp.logical_and(kk > 0, kk < nk - 1))
    def _():
        acc_ref[...] += _dot(a_ref[...], w_ref[...])

    @pl.when(kk == nk - 1)
    def _():
        o_ref[...] = acc_ref[...] + _dot(a_ref[...], w_ref[...])


def _matmul_res(a, w, res, *, tm=1024, tn=1024, tk=2048):
    m, k = a.shape
    n = w.shape[1]
    tk = min(tk, k)
    assert m % tm == 0 and n % tn == 0 and k % tk == 0
    nk = k // tk
    est = 2 * tm * tk * 2 + 2 * tk * tn * 2 + 4 * tm * tn * 4 + 2 * tm * tn * 4
    return pl.pallas_call(
        functools.partial(_matmul_res_kernel, nk=nk),
        grid=(m // tm, n // tn, nk),
        in_specs=[
            pl.BlockSpec((tm, tk), lambda i, j, kk: (i, kk)),
            pl.BlockSpec((tk, tn), lambda i, j, kk: (kk, j)),
            pl.BlockSpec((tm, tn), lambda i, j, kk: (i, j)),
        ],
        out_specs=pl.BlockSpec((tm, tn), lambda i, j, kk: (i, j)),
        out_shape=jax.ShapeDtypeStruct((m, n), F32),
        scratch_shapes=[pltpu.VMEM((tm, tn), F32)],
        compiler_params=pltpu.CompilerParams(
            dimension_semantics=("parallel", "parallel", "arbitrary"),
            vmem_limit_bytes=_vmem_limit(est)),
        name="matmul_res_k%d" % nk,
    )(a, w, res)


def _rmsnorm_kernel(x_ref, g_ref, o_ref):
    x = x_ref[...]
    ms = jnp.mean(x * x, axis=-1, keepdims=True)
    o_ref[...] = x * lax.rsqrt(ms + NORM_EPS) * g_ref[...]


def _rmsnorm(x, g, *, tm=256):
    m, k = x.shape
    return pl.pallas_call(
        _rmsnorm_kernel,
        grid=(m // tm,),
        in_specs=[pl.BlockSpec((tm, k), lambda i: (i, 0)),
                  pl.BlockSpec((1, k), lambda i: (0, 0))],
        out_specs=pl.BlockSpec((tm, k), lambda i: (i, 0)),
        out_shape=jax.ShapeDtypeStruct((m, k), F32),
        compiler_params=pltpu.CompilerParams(
            dimension_semantics=("parallel",),
            vmem_limit_bytes=_vmem_limit(6 * tm * k * 4)),
        name="final_rmsnorm",
    )(x, g.reshape(1, k))


def _rope(t, cos, sgn_sin):
    return t * cos + pltpu.roll(t, HEAD_DIM // 2, 1) * sgn_sin


def _row_index_f32():
    return lax.broadcasted_iota(jnp.int32, (CHUNK, HEAD_DIM), 0).astype(F32)


def _head(ref, off, h):
    return ref[:, off + h * HEAD_DIM: off + (h + 1) * HEAD_DIM]


def _ret_bwd_state_kernel(k_ref, v_ref, cos_ref, sin_ref, raw_ref, sb_ref, st_ref):
    @pl.when(pl.program_id(1) == 0)
    def _():
        st_ref[...] = jnp.zeros_like(st_ref)

    cos = cos_ref[...]
    sin = sin_ref[...]
    row = _row_index_f32()
    scale = HEAD_DIM ** -0.5
    for h in range(N_RET_HEADS):
        rows = slice(h * HEAD_DIM, (h + 1) * HEAD_DIM)
        st = st_ref[rows, :]
        sb_ref[0, 0, rows, :] = st.astype(BF16)
        lgb = -jnp.exp(raw_ref[N_RET_HEADS + h: N_RET_HEADS + h + 1, :])
        kh = _rope(_head(k_ref, 0, h), cos, sin) * scale
        kw = (kh * jnp.exp(lgb * row)).astype(BF16)
        kv = _dot_tn(kw, _head(v_ref, 0, h).astype(BF16))
        st_ref[rows, :] = jnp.exp(lgb * float(CHUNK)) * st + kv


def _ret_bwd_states(proj, cos, sgn_sin, raw_rows, *, batch, n_chunks):
    assert OFF_RK % RET_WIDTH == 0 and OFF_RV % RET_WIDTH == 0
    kblk, vblk = OFF_RK // RET_WIDTH, OFF_RV // RET_WIDTH
    last = n_chunks - 1
    return pl.pallas_call(
        _ret_bwd_state_kernel,
        grid=(batch, n_chunks),
        in_specs=[
            pl.BlockSpec((CHUNK, RET_WIDTH), lambda b, t: (b * n_chunks + last - t, kblk)),
            pl.BlockSpec((CHUNK, RET_WIDTH), lambda b, t: (b * n_chunks + last - t, vblk)),
            pl.BlockSpec((CHUNK, HEAD_DIM), lambda b, t: (last - t, 0)),
            pl.BlockSpec((CHUNK, HEAD_DIM), lambda b, t: (last - t, 0)),
            pl.BlockSpec((2 * N_RET_HEADS, HEAD_DIM), lambda b, t: (0, 0)),
        ],
        out_specs=pl.BlockSpec((1, 1, RET_WIDTH, HEAD_DIM), lambda b, t: (b, last - t, 0, 0)),
        out_shape=jax.ShapeDtypeStruct((batch, n_chunks, RET_WIDTH, HEAD_DIM), BF16),
        scratch_shapes=[pltpu.VMEM((RET_WIDTH, HEAD_DIM), F32)],
        compiler_params=pltpu.CompilerParams(
            dimension_semantics=("parallel", "arbitrary"),
            vmem_limit_bytes=_vmem_limit(8 << 20)),
        name="ret_bwd_states",
    )(proj, proj, cos, sgn_sin, raw_rows)


def _sgu(p_ref, lng_ref, lnb_ref, ws_ref, bs_ref, o_ref):
    sqrt_half = np.sqrt(0.5).astype(np.float32)

    def gelu(t):
        return 0.5 * t * (1.0 + lax.erf(t * sqrt_half))

    u = gelu(p_ref[:, OFF_U:OFF_U + SGU_WIDTH])
    v = gelu(p_ref[:, OFF_V:OFF_V + SGU_WIDTH])
    mu = jnp.mean(v, axis=-1, keepdims=True)
    var = jnp.mean(jnp.square(v - mu), axis=-1, keepdims=True)
    vn = (v - mu) * lax.rsqrt(var + NORM_EPS) * lng_ref[...] + lnb_ref[...]
    for g in range(N_SGU_GROUPS):
        cols = slice(g * HEAD_DIM, (g + 1) * HEAD_DIM)
        s = _dot(ws_ref[g], vn[:, cols].astype(BF16)) + bs_ref[:, g:g + 1]
        o_ref[:, cols] = (u[:, cols] * s).astype(o_ref.dtype)


def _retention(p_ref, sb_ref, raw_ref, cos, sin, o_ref, sf_ref, first_chunk):
    @pl.when(first_chunk)
    def _():
        sf_ref[...] = jnp.zeros_like(sf_ref)

    row = _row_index_f32()
    col = lax.broadcasted_iota(jnp.int32, (CHUNK, CHUNK), 1).astype(F32)
    delta = row - col
    scale = HEAD_DIM ** -0.5
    for h in range(N_RET_HEADS):
        rows = slice(h * HEAD_DIM, (h + 1) * HEAD_DIM)
        lgf = -jnp.exp(raw_ref[h:h + 1, :])
        lgb = -jnp.exp(raw_ref[N_RET_HEADS + h: N_RET_HEADS + h + 1, :])
        decay = jnp.where(delta >= 0.0,
                          jnp.exp(lgf * jnp.maximum(delta, 0.0)),
                          jnp.exp(lgb * jnp.maximum(-delta, 0.0)))
        q = _rope(_head(p_ref, OFF_RQ, h), cos, sin)
        k = _rope(_head(p_ref, OFF_RK, h), cos, sin) * scale
        v = _head(p_ref, OFF_RV, h).astype(BF16)
        scores = _dot_nt(q.astype(BF16), k.astype(BF16)) * decay
        inner = _dot(scores.astype(BF16), v)
        sf_prev = sf_ref[rows, :]
        kv_f = _dot_tn((k * jnp.exp(lgf * (float(CHUNK - 1) - row))).astype(BF16), v)
        cross_f = _dot((q * jnp.exp(lgf * (row + 1.0))).astype(BF16), sf_prev.astype(BF16))
        sf_ref[rows, :] = jnp.exp(lgf * float(CHUNK)) * sf_prev + kv_f
        cross_b = _dot((q * jnp.exp(lgb * (float(CHUNK) - row))).astype(BF16),
                       sb_ref[0, 0, rows, :])
        rf = (inner + cross_f) + cross_b
        rn = rf * lax.rsqrt(jnp.mean(rf * rf, axis=-1, keepdims=True) + NORM_EPS)
        gate = _head(p_ref, OFF_RG, h)
        o_ref[:, SGU_WIDTH + h * HEAD_DIM: SGU_WIDTH + (h + 1) * HEAD_DIM] = (
            gate * jax.nn.sigmoid(gate) * rn).astype(o_ref.dtype)


def _attention(p_ref, kvp_ref, kvn_ref, cos3_ref, sin3_ref, sink_ref, o_ref, chunk, seq):
    cos3 = cos3_ref[...]
    sin3 = sin3_ref[...]
    cos = cos3[CHUNK:2 * CHUNK]
    sin = sin3[CHUNK:2 * CHUNK]
    scale = HEAD_DIM ** -0.5
    rows = Q_PER_KV * CHUNK
    qi = lax.broadcasted_iota(jnp.int32, (rows, 3 * CHUNK), 0) & (CHUNK - 1)
    kj = lax.broadcasted_iota(jnp.int32, (rows, 3 * CHUNK), 1)
    rel = kj - CHUNK - qi
    kpos = (chunk - 1) * CHUNK + kj
    valid = (jnp.abs(rel) <= WINDOW) & (kpos >= 0) & (kpos < seq)
    out_off = SGU_WIDTH + RET_WIDTH
    for kh in range(N_KV_HEADS):
        kcols = slice(kh * HEAD_DIM, (kh + 1) * HEAD_DIM)
        vcols = slice(KV_WIDTH + kh * HEAD_DIM, KV_WIDTH + (kh + 1) * HEAD_DIM)
        k3 = jnp.concatenate([kvp_ref[:, kcols], _head(p_ref, OFF_AK, kh), kvn_ref[:, kcols]], axis=0)
        v3 = jnp.concatenate([kvp_ref[:, vcols], _head(p_ref, OFF_AV, kh), kvn_ref[:, vcols]], axis=0)
        k3 = _rope(k3, cos3, sin3).astype(BF16)
        qs = jnp.concatenate(
            [_rope(_head(p_ref, OFF_AQ, kh * Q_PER_KV + g), cos, sin) * scale for g in range(Q_PER_KV)],
            axis=0)
        sink = jnp.concatenate(
            [jnp.broadcast_to(sink_ref[kh * Q_PER_KV + g: kh * Q_PER_KV + g + 1, 0:1], (CHUNK, 1))
             for g in range(Q_PER_KV)], axis=0)
        s = jnp.where(valid, _dot_nt(qs.astype(BF16), k3), NEG_INF)
        m = jnp.maximum(jnp.max(s, axis=-1, keepdims=True), sink)
        p = jnp.exp(s - m)
        inv = 1.0 / (jnp.sum(p, axis=-1, keepdims=True) + jnp.exp(sink - m))
        o = _dot((p * inv).astype(BF16), v3.astype(BF16))
        for g in range(Q_PER_KV):
            hq = kh * Q_PER_KV + g
            o_ref[:, out_off + hq * HEAD_DIM: out_off + (hq + 1) * HEAD_DIM] = (
                o[g * CHUNK:(g + 1) * CHUNK]).astype(o_ref.dtype)


def _mixer_kernel(p_ref, kvp_ref, kvn_ref, cos3_ref, sin3_ref, sb_ref, lng_ref, lnb_ref,
                  ws_ref, bs_ref, raw_ref, sink_ref, o_ref, sf_ref, *, seq):
    chunk = pl.program_id(1)
    _sgu(p_ref, lng_ref, lnb_ref, ws_ref, bs_ref, o_ref)
    cos = cos3_ref[CHUNK:2 * CHUNK, :]
    sin = sin3_ref[CHUNK:2 * CHUNK, :]
    _retention(p_ref, sb_ref, raw_ref, cos, sin, o_ref, sf_ref, chunk == 0)
    _attention(p_ref, kvp_ref, kvn_ref, cos3_ref, sin3_ref, sink_ref, o_ref, chunk, seq)


def _mixers(proj, sb, cos3, sin3, lng, lnb, ws, bs_t, raw_rows, sink_rows, *, batch, n_chunks):
    m = proj.shape[0]
    kv_blk = OFF_AK // (2 * KV_WIDTH)
    assert kv_blk * 2 * KV_WIDTH == OFF_AK
    last = n_chunks - 1
    const2 = lambda b, n: (0, 0)
    est = 2 * CHUNK * IN_WIDTH * 4 + 4 * CHUNK * 2 * KV_WIDTH * 4 + 2 * CHUNK * MIX_WIDTH * 2 + (12 << 20)
    return pl.pallas_call(
        functools.partial(_mixer_kernel, seq=n_chunks * CHUNK),
        grid=(batch, n_chunks),
        in_specs=[
            pl.BlockSpec((CHUNK, IN_WIDTH), lambda b, n: (b * n_chunks + n, 0)),
            pl.BlockSpec((CHUNK, 2 * KV_WIDTH),
                         lambda b, n: (b * n_chunks + jnp.maximum(n - 1, 0), kv_blk)),
            pl.BlockSpec((CHUNK, 2 * KV_WIDTH),
                         lambda b, n: (b * n_chunks + jnp.minimum(n + 1, last), kv_blk)),
            pl.BlockSpec((3 * CHUNK, HEAD_DIM), lambda b, n: (n, 0)),
            pl.BlockSpec((3 * CHUNK, HEAD_DIM), lambda b, n: (n, 0)),
            pl.BlockSpec((1, 1, RET_WIDTH, HEAD_DIM), lambda b, n: (b, n, 0, 0)),
            pl.BlockSpec((1, SGU_WIDTH), const2),
            pl.BlockSpec((1, SGU_WIDTH), const2),
            pl.BlockSpec((N_SGU_GROUPS, CHUNK, CHUNK), lambda b, n: (0, 0, 0)),
            pl.BlockSpec((CHUNK, N_SGU_GROUPS), const2),
            pl.BlockSpec((2 * N_RET_HEADS, HEAD_DIM), const2),
            pl.BlockSpec((N_Q_HEADS, HEAD_DIM), const2),
        ],
        out_specs=pl.BlockSpec((CHUNK, MIX_WIDTH), lambda b, n: (b * n_chunks + n, 0)),
        out_shape=jax.ShapeDtypeStruct((m, MIX_WIDTH), BF16),
        scratch_shapes=[pltpu.VMEM((RET_WIDTH, HEAD_DIM), F32)],
        compiler_params=pltpu.CompilerParams(
            dimension_semantics=("parallel", "arbitrary"),
            vmem_limit_bytes=_vmem_limit(est)),
        name="mixers",
    )(proj, proj, proj, cos3, sin3, sb, lng, lnb, ws, bs_t, raw_rows, sink_rows)


def _rope_tables(seq):
    pos = jnp.arange(seq, dtype=F32)
    inv = ROPE_THETA ** (-jnp.arange(0, HEAD_DIM, 2, dtype=F32) / HEAD_DIM)
    ang = pos[:, None] * inv[None, :]
    ang = jnp.concatenate([ang, ang], axis=-1)
    sgn = jnp.where(jnp.arange(HEAD_DIM) < HEAD_DIM // 2, -1.0, 1.0).astype(F32)
    return jnp.cos(ang), jnp.sin(ang) * sgn[None, :]


def _three_chunk_view(t, n_chunks):
    seq = t.shape[0]
    pad = jnp.zeros((CHUNK, HEAD_DIM), t.dtype)
    tp = jnp.concatenate([pad, t, pad], axis=0)
    parts = [tp[s * CHUNK: s * CHUNK + seq].reshape(n_chunks, CHUNK, HEAD_DIM) for s in range(3)]
    return jnp.concatenate(parts, axis=1).reshape(n_chunks * 3 * CHUNK, HEAD_DIM)


def kernel(x, ln_mix_g, w_in, sgu_ln_g, sgu_ln_b, sgu_w, sgu_b, ret_log_decay, attn_sink,
           w_out, ln_mlp_g, w_up, w_down, final_norm_g):
    batch, seq, d_model = x.shape
    depth = w_in.shape[0]
    assert w_in.shape[2] == IN_WIDTH and w_out.shape[1] == MIX_WIDTH and seq % CHUNK == 0
    n_chunks = seq // CHUNK
    m = batch * seq

    cos, sgn_sin = _rope_tables(seq)
    cos3 = _three_chunk_view(cos, n_chunks)
    sin3 = _three_chunk_view(sgn_sin, n_chunks)

    x2 = x.reshape(m, d_model)
    for l in range(depth):
        raw_rows = jnp.broadcast_to(
            ret_log_decay[l].astype(F32).reshape(2 * N_RET_HEADS, 1), (2 * N_RET_HEADS, HEAD_DIM))
        sink_rows = jnp.broadcast_to(
            attn_sink[l].astype(F32).reshape(N_Q_HEADS, 1), (N_Q_HEADS, HEAD_DIM))
        proj = _norm_matmul(x2, ln_mix_g[l], w_in[l].astype(BF16), relu2=False, out_dtype=F32)
        sb = _ret_bwd_states(proj, cos, sgn_sin, raw_rows, batch=batch, n_chunks=n_chunks)
        mix = _mixers(proj, sb, cos3, sin3,
                      sgu_ln_g[l].reshape(1, SGU_WIDTH), sgu_ln_b[l].reshape(1, SGU_WIDTH),
                      sgu_w[l].astype(BF16), sgu_b[l].T, raw_rows, sink_rows,
                      batch=batch, n_chunks=n_chunks)
        x2 = _matmul_res(mix, w_out[l].astype(BF16), x2)
        hid = _norm_matmul(x2, ln_mlp_g[l], w_up[l].astype(BF16), relu2=True, out_dtype=BF16)
        x2 = _matmul_res(hid, w_down[l].astype(BF16), x2)
    return _rmsnorm(x2, final_norm_g).reshape(batch, seq, d_model)
```

```python
import functools

import numpy as np
import jax
import jax.numpy as jnp
from jax import lax
from jax.experimental import pallas as pl
from jax.experimental.pallas import tpu as pltpu

HEAD_DIM = 128
CHUNK = 128
WINDOW = 128
ROPE_THETA = 10000.0
NORM_EPS = 1e-5
NEG_INF = -1e30
N_SGU_GROUPS = 8
N_RET_HEADS = 8
N_Q_HEADS = 16
N_KV_HEADS = 4
Q_PER_KV = N_Q_HEADS // N_KV_HEADS

SGU_WIDTH = N_SGU_GROUPS * HEAD_DIM
RET_WIDTH = N_RET_HEADS * HEAD_DIM
ATT_WIDTH = N_Q_HEADS * HEAD_DIM
KV_WIDTH = N_KV_HEADS * HEAD_DIM
OFF_U = 0
OFF_V = OFF_U + SGU_WIDTH
OFF_RQ = OFF_V + SGU_WIDTH
OFF_RK = OFF_RQ + RET_WIDTH
OFF_RV = OFF_RK + RET_WIDTH
OFF_RG = OFF_RV + RET_WIDTH
OFF_AQ = OFF_RG + RET_WIDTH
OFF_AK = OFF_AQ + ATT_WIDTH
OFF_AV = OFF_AK + KV_WIDTH
IN_WIDTH = OFF_AV + KV_WIDTH
MIX_WIDTH = SGU_WIDTH + RET_WIDTH + ATT_WIDTH

V7X_VMEM_BYTES = 64 * 1024 * 1024
V7X_VMEM_BUDGET = 56 * 1024 * 1024
CAST_SLAB_BYTES = 4 * 1024 * 1024

F32 = jnp.float32
BF16 = jnp.bfloat16


def _vmem_limit(est_bytes):
    assert est_bytes <= V7X_VMEM_BUDGET, est_bytes
    return min(max(int(est_bytes * 1.25), 16 << 20), V7X_VMEM_BUDGET)


def _dot(a, b):
    return jnp.dot(a, b, preferred_element_type=F32)


def _dot_nt(a, b):
    return lax.dot_general(a, b, (((1,), (1,)), ((), ())), preferred_element_type=F32)


def _dot_tn(a, b):
    return lax.dot_general(a, b, (((0,), (0,)), ((), ())), preferred_element_type=F32)


def _cast_kernel(w_ref, o_ref):
    o_ref[...] = w_ref[...].astype(o_ref.dtype)


def _cast_bf16(w):
    depth, k, n = w.shape
    rows = depth * k
    tr = 1
    while tr * 2 * n * 4 <= CAST_SLAB_BYTES and rows % (tr * 2) == 0:
        tr *= 2
    out = pl.pallas_call(
        _cast_kernel,
        grid=(rows // tr,),
        in_specs=[pl.BlockSpec((tr, n), lambda i: (i, 0))],
        out_specs=pl.BlockSpec((tr, n), lambda i: (i, 0)),
        out_shape=jax.ShapeDtypeStruct((rows, n), BF16),
        compiler_params=pltpu.CompilerParams(
            dimension_semantics=("parallel",),
            vmem_limit_bytes=_vmem_limit(2 * tr * n * 6 + tr * n * 4)),
        name="cast_bf16",
    )(w.reshape(rows, n))
    return out.reshape(depth, k, n)


def _prenorm_kernel(x_ref, g_ref, xg_ref, r_ref):
    x = x_ref[...]
    xg_ref[...] = (x * g_ref[...]).astype(BF16)
    r_ref[...] = lax.rsqrt(jnp.mean(x * x, axis=-1, keepdims=True) + NORM_EPS)


def _prenorm(x, g, *, tm=256):
    m, k = x.shape
    return pl.pallas_call(
        _prenorm_kernel,
        grid=(m // tm,),
        in_specs=[pl.BlockSpec((tm, k), lambda i: (i, 0)),
                  pl.BlockSpec((1, k), lambda i: (0, 0))],
        out_specs=[pl.BlockSpec((tm, k), lambda i: (i, 0)),
                   pl.BlockSpec((tm, 1), lambda i: (i, 0))],
        out_shape=[jax.ShapeDtypeStruct((m, k), BF16), jax.ShapeDtypeStruct((m, 1), F32)],
        compiler_params=pltpu.CompilerParams(
            dimension_semantics=("parallel",),
            vmem_limit_bytes=_vmem_limit(2 * tm * k * 6 + 2 * tm * k * 4)),
        name="prenorm",
    )(x, g.reshape(1, k))


def _scaled_matmul_kernel(a_ref, r_ref, w_ref, o_ref, *, relu2):
    acc = _dot(a_ref[...], w_ref[...]) * r_ref[...]
    if relu2:
        acc = jnp.square(jnp.maximum(acc, 0.0))
    o_ref[...] = acc.astype(o_ref.dtype)


def _scaled_matmul(a, r, w, *, relu2, out_dtype, tm=1024, tn=1024):
    m, k = a.shape
    n = w.shape[1]
    assert m % tm == 0 and n % tn == 0
    out_bytes = jnp.dtype(out_dtype).itemsize
    est = 2 * tm * k * 2 + 2 * k * tn * 2 + 2 * tm * tn * out_bytes + 2 * tm * tn * 4
    return pl.pallas_call(
        functools.partial(_scaled_matmul_kernel, relu2=relu2),
        grid=(m // tm, n // tn),
        in_specs=[
            pl.BlockSpec((tm, k), lambda i, j: (i, 0)),
            pl.BlockSpec((tm, 1), lambda i, j: (i, 0)),
            pl.BlockSpec((k, tn), lambda i, j: (0, j)),
        ],
        out_specs=pl.BlockSpec((tm, tn), lambda i, j: (i, j)),
        out_shape=jax.ShapeDtypeStruct((m, n), out_dtype),
        compiler_params=pltpu.CompilerParams(
            dimension_semantics=("parallel", "parallel"),
            vmem_limit_bytes=_vmem_limit(est)),
        name="scaled_matmul_relu2" if relu2 else "scaled_matmul",
    )(a, r, w)


def _matmul_res_kernel(a_ref, w_ref, res_ref, g_ref, o_ref, r_ref, *rest, nk, nj, width):
    maybe_xg_ref, acc_ref = (rest[0], rest[1]) if len(rest) == 2 else (None, rest[0])
    j = pl.program_id(1)
    kk = pl.program_id(2)

    def finish(x_new):
        o_ref[...] = x_new
        if maybe_xg_ref is not None:
            maybe_xg_ref[...] = (x_new * g_ref[...]).astype(BF16)
        ss = jnp.sum(x_new * x_new, axis=-1, keepdims=True)

        @pl.when(j == 0)
        def _():
            r_ref[...] = ss

        @pl.when(j > 0)
        def _():
            r_ref[...] += ss

        @pl.when(j == nj - 1)
        def _():
            r_ref[...] = lax.rsqrt(r_ref[...] / float(width) + NORM_EPS)

    if nk == 1:
        finish(res_ref[...] + _dot(a_ref[...], w_ref[...]))
        return

    @pl.when(kk == 0)
    def _():
        acc_ref[...] = res_ref[...] + _dot(a_ref[...], w_ref[...])

    @pl.when(jnp.logical_and(kk > 0, kk < nk - 1))
    def _():
        acc_ref[...] += _dot(a_ref[...], w_ref[...])

    @pl.when(kk == nk - 1)
    def _():
        finish(acc_ref[...] + _dot(a_ref[...], w_ref[...]))


def _matmul_res(a, w, res, g_next, *, tm, tn, tk, emit_xg=True):
    m, k = a.shape
    n = w.shape[1]
    assert m % tm == 0 and n % tn == 0 and k % tk == 0
    nk, nj = k // tk, n // tn
    acc_shape = (tm, tn) if nk > 1 else (8, 128)
    est = (2 * tm * tk * 2 + 2 * tk * tn * 2 + 2 * tm * tn * (4 + 4 + 2 * emit_xg)
           + acc_shape[0] * acc_shape[1] * 4 + 2 * tm * tn * 4)
    tile = pl.BlockSpec((tm, tn), lambda i, j, kk: (i, j))
    out_specs = [tile, pl.BlockSpec((tm, 1), lambda i, j, kk: (i, 0))]
    out_shape = [jax.ShapeDtypeStruct((m, n), F32), jax.ShapeDtypeStruct((m, 1), F32)]
    if emit_xg:
        out_specs.append(tile)
        out_shape.append(jax.ShapeDtypeStruct((m, n), BF16))
    return pl.pallas_call(
        functools.partial(_matmul_res_kernel, nk=nk, nj=nj, width=n),
        grid=(m // tm, nj, nk),
        in_specs=[
            pl.BlockSpec((tm, tk), lambda i, j, kk: (i, kk)),
            pl.BlockSpec((tk, tn), lambda i, j, kk: (kk, j)),
            tile,
            pl.BlockSpec((1, tn), lambda i, j, kk: (0, j)),
        ],
        out_specs=out_specs,
        out_shape=out_shape,
        scratch_shapes=[pltpu.VMEM(acc_shape, F32)],
        compiler_params=pltpu.CompilerParams(
            dimension_semantics=("parallel", "arbitrary", "arbitrary"),
            vmem_limit_bytes=_vmem_limit(est)),
        name="matmul_res_k%d" % nk,
    )(a, w, res, g_next.reshape(1, n))


def _final_scale_kernel(x_ref, r_ref, g_ref, o_ref):
    o_ref[...] = x_ref[...] * r_ref[...] * g_ref[...]


def _final_scale(x, r, g, *, tm=512):
    m, k = x.shape
    return pl.pallas_call(
        _final_scale_kernel,
        grid=(m // tm,),
        in_specs=[pl.BlockSpec((tm, k), lambda i: (i, 0)),
                  pl.BlockSpec((tm, 1), lambda i: (i, 0)),
                  pl.BlockSpec((1, k), lambda i: (0, 0))],
        out_specs=pl.BlockSpec((tm, k), lambda i: (i, 0)),
        out_shape=jax.ShapeDtypeStruct((m, k), F32),
        compiler_params=pltpu.CompilerParams(
            dimension_semantics=("parallel",),
            vmem_limit_bytes=_vmem_limit(5 * tm * k * 4)),
        name="final_scale",
    )(x, r, g.reshape(1, k))


def _rope(t, cos, sgn_sin):
    return t * cos + pltpu.roll(t, HEAD_DIM // 2, 1) * sgn_sin


def _row_index_f32():
    return lax.broadcasted_iota(jnp.int32, (CHUNK, HEAD_DIM), 0).astype(F32)


def _head(ref, off, h):
    return ref[:, off + h * HEAD_DIM: off + (h + 1) * HEAD_DIM]


def _ret_bwd_state_kernel(k_ref, v_ref, cos_ref, sin_ref, raw_ref, sb_ref, st_ref):
    @pl.when(pl.program_id(1) == 0)
    def _():
        st_ref[...] = jnp.zeros_like(st_ref)

    cos = cos_ref[...]
    sin = sin_ref[...]
    row = _row_index_f32()
    scale = HEAD_DIM ** -0.5
    for h in range(N_RET_HEADS):
        rows = slice(h * HEAD_DIM, (h + 1) * HEAD_DIM)
        st = st_ref[rows, :]
        sb_ref[0, 0, rows, :] = st.astype(BF16)
        lgb = -jnp.exp(raw_ref[N_RET_HEADS + h: N_RET_HEADS + h + 1, :])
        kh = _rope(_head(k_ref, 0, h), cos, sin) * scale
        kw = (kh * jnp.exp(lgb * row)).astype(BF16)
        kv = _dot_tn(kw, _head(v_ref, 0, h).astype(BF16))
        st_ref[rows, :] = jnp.exp(lgb * float(CHUNK)) * st + kv


def _ret_bwd_states(proj, cos, sgn_sin, raw_rows, *, batch, n_chunks):
    assert OFF_RK % RET_WIDTH == 0 and OFF_RV % RET_WIDTH == 0
    kblk, vblk = OFF_RK // RET_WIDTH, OFF_RV // RET_WIDTH
    last = n_chunks - 1
    return pl.pallas_call(
        _ret_bwd_state_kernel,
        grid=(batch, n_chunks),
        in_specs=[
            pl.BlockSpec((CHUNK, RET_WIDTH), lambda b, t: (b * n_chunks + last - t, kblk)),
            pl.BlockSpec((CHUNK, RET_WIDTH), lambda b, t: (b * n_chunks + last - t, vblk)),
            pl.BlockSpec((CHUNK, HEAD_DIM), lambda b, t: (last - t, 0)),
            pl.BlockSpec((CHUNK, HEAD_DIM), lambda b, t: (last - t, 0)),
            pl.BlockSpec((2 * N_RET_HEADS, HEAD_DIM), lambda b, t: (0, 0)),
        ],
        out_specs=pl.BlockSpec((1, 1, RET_WIDTH, HEAD_DIM), lambda b, t: (b, last - t, 0, 0)),
        out_shape=jax.ShapeDtypeStruct((batch, n_chunks, RET_WIDTH, HEAD_DIM), BF16),
        scratch_shapes=[pltpu.VMEM((RET_WIDTH, HEAD_DIM), F32)],
        compiler_params=pltpu.CompilerParams(
            dimension_semantics=("parallel", "arbitrary"),
            vmem_limit_bytes=_vmem_limit(8 << 20)),
        name="ret_bwd_states",
    )(proj, proj, cos, sgn_sin, raw_rows)


def _sgu(p_ref, lng_ref, lnb_ref, ws_ref, bs_ref, o_ref):
    sqrt_half = np.sqrt(0.5).astype(np.float32)

    def gelu(t):
        return 0.5 * t * (1.0 + lax.erf(t * sqrt_half))

    u = gelu(p_ref[:, OFF_U:OFF_U + SGU_WIDTH])
    v = gelu(p_ref[:, OFF_V:OFF_V + SGU_WIDTH])
    mu = jnp.mean(v, axis=-1, keepdims=True)
    var = jnp.mean(jnp.square(v - mu), axis=-1, keepdims=True)
    vn = (v - mu) * lax.rsqrt(var + NORM_EPS) * lng_ref[...] + lnb_ref[...]
    for g in range(N_SGU_GROUPS):
        cols = slice(g * HEAD_DIM, (g + 1) * HEAD_DIM)
        s = _dot(ws_ref[g], vn[:, cols].astype(BF16)) + bs_ref[:, g:g + 1]
        o_ref[:, cols] = (u[:, cols] * s).astype(o_ref.dtype)


N_RET_TABLES = 4


def _ret_table(tab_ref, h, t):
    base = (h * N_RET_TABLES + t) * CHUNK
    return tab_ref.at[base:base + CHUNK, :]


def _fill_ret_tables(raw_ref, tab_ref):
    row = _row_index_f32()
    col = lax.broadcasted_iota(jnp.int32, (CHUNK, CHUNK), 1).astype(F32)
    delta = row - col
    for h in range(N_RET_HEADS):
        lgf = -jnp.exp(raw_ref[h:h + 1, :])
        lgb = -jnp.exp(raw_ref[N_RET_HEADS + h: N_RET_HEADS + h + 1, :])
        _ret_table(tab_ref, h, 0)[...] = jnp.where(
            delta >= 0.0,
            jnp.exp(lgf * jnp.maximum(delta, 0.0)),
            jnp.exp(lgb * jnp.maximum(-delta, 0.0)))
        _ret_table(tab_ref, h, 1)[...] = jnp.exp(lgf * (float(CHUNK - 1) - row))
        _ret_table(tab_ref, h, 2)[...] = jnp.exp(lgf * (row + 1.0))
        _ret_table(tab_ref, h, 3)[...] = jnp.exp(lgb * (float(CHUNK) - row))


def _retention(p_ref, sb_ref, raw_ref, cos, sin, o_ref, sf_ref, tab_ref, first_chunk):
    @pl.when(first_chunk)
    def _():
        sf_ref[...] = jnp.zeros_like(sf_ref)
        _fill_ret_tables(raw_ref, tab_ref)

    scale = HEAD_DIM ** -0.5
    for h in range(N_RET_HEADS):
        rows = slice(h * HEAD_DIM, (h + 1) * HEAD_DIM)
        lgf = -jnp.exp(raw_ref[h:h + 1, :])
        q = _rope(_head(p_ref, OFF_RQ, h), cos, sin)
        k = _rope(_head(p_ref, OFF_RK, h), cos, sin) * scale
        v = _head(p_ref, OFF_RV, h).astype(BF16)
        scores = _dot_nt(q.astype(BF16), k.astype(BF16)) * _ret_table(tab_ref, h, 0)[...]
        inner = _dot(scores.astype(BF16), v)
        sf_prev = sf_ref[rows, :]
        kv_f = _dot_tn((k * _ret_table(tab_ref, h, 1)[...]).astype(BF16), v)
        cross_f = _dot((q * _ret_table(tab_ref, h, 2)[...]).astype(BF16), sf_prev.astype(BF16))
        sf_ref[rows, :] = jnp.exp(lgf * float(CHUNK)) * sf_prev + kv_f
        cross_b = _dot((q * _ret_table(tab_ref, h, 3)[...]).astype(BF16), sb_ref[0, 0, rows, :])
        rf = (inner + cross_f) + cross_b
        rn = rf * lax.rsqrt(jnp.mean(rf * rf, axis=-1, keepdims=True) + NORM_EPS)
        gate = _head(p_ref, OFF_RG, h)
        o_ref[:, SGU_WIDTH + h * HEAD_DIM: SGU_WIDTH + (h + 1) * HEAD_DIM] = (
            gate * jax.nn.sigmoid(gate) * rn).astype(o_ref.dtype)


def _attention(p_ref, kvp_ref, kvn_ref, cos3_ref, sin3_ref, sink_ref, o_ref, chunk, seq):
    cos3 = cos3_ref[...]
    sin3 = sin3_ref[...]
    cos = cos3[CHUNK:2 * CHUNK]
    sin = sin3[CHUNK:2 * CHUNK]
    scale = HEAD_DIM ** -0.5
    qi = lax.broadcasted_iota(jnp.int32, (CHUNK, 3 * CHUNK), 0)
    kj = lax.broadcasted_iota(jnp.int32, (CHUNK, 3 * CHUNK), 1)
    rel = kj - CHUNK - qi
    kpos = (chunk - 1) * CHUNK + kj
    valid = (jnp.abs(rel) <= WINDOW) & (kpos >= 0) & (kpos < seq)
    valid = jnp.concatenate([valid] * Q_PER_KV, axis=0)
    out_off = SGU_WIDTH + RET_WIDTH
    for kh in range(N_KV_HEADS):
        kcols = slice(kh * HEAD_DIM, (kh + 1) * HEAD_DIM)
        vcols = slice(KV_WIDTH + kh * HEAD_DIM, KV_WIDTH + (kh + 1) * HEAD_DIM)
        k3 = jnp.concatenate([kvp_ref[:, kcols], _head(p_ref, OFF_AK, kh), kvn_ref[:, kcols]], axis=0)
        v3 = jnp.concatenate([kvp_ref[:, vcols], _head(p_ref, OFF_AV, kh), kvn_ref[:, vcols]], axis=0)
        k3 = _rope(k3, cos3, sin3).astype(BF16)
        qs = jnp.concatenate(
            [_rope(_head(p_ref, OFF_AQ, kh * Q_PER_KV + g), cos, sin) * scale for g in range(Q_PER_KV)],
            axis=0)
        sink = jnp.concatenate(
            [jnp.broadcast_to(sink_ref[kh * Q_PER_KV + g: kh * Q_PER_KV + g + 1, 0:1], (CHUNK, 1))
             for g in range(Q_PER_KV)], axis=0)
        s = jnp.where(valid, _dot_nt(qs.astype(BF16), k3), NEG_INF)
        m = jnp.maximum(jnp.max(s, axis=-1, keepdims=True), sink)
        p = jnp.exp(s - m)
        inv = 1.0 / (jnp.sum(p, axis=-1, keepdims=True) + jnp.exp(sink - m))
        o = _dot((p * inv).astype(BF16), v3.astype(BF16))
        for g in range(Q_PER_KV):
            hq = kh * Q_PER_KV + g
            o_ref[:, out_off + hq * HEAD_DIM: out_off + (hq + 1) * HEAD_DIM] = (
                o[g * CHUNK:(g + 1) * CHUNK]).astype(o_ref.dtype)


def _mixer_kernel(p_ref, kvp_ref, kvn_ref, cos3_ref, sin3_ref, sb_ref, lng_ref, lnb_ref,
                  ws_ref, bs_ref, raw_ref, sink_ref, o_ref, sf_ref, tab_ref, *, seq):
    chunk = pl.program_id(1)
    _sgu(p_ref, lng_ref, lnb_ref, ws_ref, bs_ref, o_ref)
    cos = cos3_ref[CHUNK:2 * CHUNK, :]
    sin = sin3_ref[CHUNK:2 * CHUNK, :]
    _retention(p_ref, sb_ref, raw_ref, cos, sin, o_ref, sf_ref, tab_ref, chunk == 0)
    _attention(p_ref, kvp_ref, kvn_ref, cos3_ref, sin3_ref, sink_ref, o_ref, chunk, seq)


def _mixers(proj, sb, cos3, sin3, lng, lnb, ws, bs_t, raw_rows, sink_rows, *, batch, n_chunks):
    m = proj.shape[0]
    kv_blk = OFF_AK // (2 * KV_WIDTH)
    assert kv_blk * 2 * KV_WIDTH == OFF_AK
    last = n_chunks - 1
    const2 = lambda b, n: (0, 0)
    est = 2 * CHUNK * IN_WIDTH * 4 + 4 * CHUNK * 2 * KV_WIDTH * 4 + 2 * CHUNK * MIX_WIDTH * 2 + (12 << 20)
    return pl.pallas_call(
        functools.partial(_mixer_kernel, seq=n_chunks * CHUNK),
        grid=(batch, n_chunks),
        in_specs=[
            pl.BlockSpec((CHUNK, IN_WIDTH), lambda b, n: (b * n_chunks + n, 0)),
            pl.BlockSpec((CHUNK, 2 * KV_WIDTH),
                         lambda b, n: (b * n_chunks + jnp.maximum(n - 1, 0), kv_blk)),
            pl.BlockSpec((CHUNK, 2 * KV_WIDTH),
                         lambda b, n: (b * n_chunks + jnp.minimum(n + 1, last), kv_blk)),
            pl.BlockSpec((3 * CHUNK, HEAD_DIM), lambda b, n: (n, 0)),
            pl.BlockSpec((3 * CHUNK, HEAD_DIM), lambda b, n: (n, 0)),
            pl.BlockSpec((1, 1, RET_WIDTH, HEAD_DIM), lambda b, n: (b, n, 0, 0)),
            pl.BlockSpec((1, SGU_WIDTH), const2),
            pl.BlockSpec((1, SGU_WIDTH), const2),
            pl.BlockSpec((N_SGU_GROUPS, CHUNK, CHUNK), lambda b, n: (0, 0, 0)),
            pl.BlockSpec((CHUNK, N_SGU_GROUPS), const2),
            pl.BlockSpec((2 * N_RET_HEADS, HEAD_DIM), const2),
            pl.BlockSpec((N_Q_HEADS, HEAD_DIM), const2),
        ],
        out_specs=pl.BlockSpec((CHUNK, MIX_WIDTH), lambda b, n: (b * n_chunks + n, 0)),
        out_shape=jax.ShapeDtypeStruct((m, MIX_WIDTH), BF16),
        scratch_shapes=[pltpu.VMEM((RET_WIDTH, HEAD_DIM), F32),
                        pltpu.VMEM((N_RET_HEADS * N_RET_TABLES * CHUNK, HEAD_DIM), F32)],
        compiler_params=pltpu.CompilerParams(
            dimension_semantics=("parallel", "arbitrary"),
            vmem_limit_bytes=_vmem_limit(est)),
        name="mixers",
    )(proj, proj, proj, cos3, sin3, sb, lng, lnb, ws, bs_t, raw_rows, sink_rows)


def _rope_tables(seq):
    pos = jnp.arange(seq, dtype=F32)
    inv = ROPE_THETA ** (-jnp.arange(0, HEAD_DIM, 2, dtype=F32) / HEAD_DIM)
    ang = pos[:, None] * inv[None, :]
    ang = jnp.concatenate([ang, ang], axis=-1)
    sgn = jnp.where(jnp.arange(HEAD_DIM) < HEAD_DIM // 2, -1.0, 1.0).astype(F32)
    return jnp.cos(ang), jnp.sin(ang) * sgn[None, :]


def _three_chunk_view(t, n_chunks):
    seq = t.shape[0]
    pad = jnp.zeros((CHUNK, HEAD_DIM), t.dtype)
    tp = jnp.concatenate([pad, t, pad], axis=0)
    parts = [tp[s * CHUNK: s * CHUNK + seq].reshape(n_chunks, CHUNK, HEAD_DIM) for s in range(3)]
    return jnp.concatenate(parts, axis=1).reshape(n_chunks * 3 * CHUNK, HEAD_DIM)


def kernel(x, ln_mix_g, w_in, sgu_ln_g, sgu_ln_b, sgu_w, sgu_b, ret_log_decay, attn_sink,
           w_out, ln_mlp_g, w_up, w_down, final_norm_g):
    batch, seq, d_model = x.shape
    depth = w_in.shape[0]
    assert w_in.shape[2] == IN_WIDTH and w_out.shape[1] == MIX_WIDTH and seq % CHUNK == 0
    n_chunks = seq // CHUNK
    m = batch * seq

    cos, sgn_sin = _rope_tables(seq)
    cos3 = _three_chunk_view(cos, n_chunks)
    sin3 = _three_chunk_view(sgn_sin, n_chunks)

    w_in_b, w_out_b, w_up_b, w_down_b = (_cast_bf16(w) for w in (w_in, w_out, w_up, w_down))
    next_gains = [ln_mix_g[l] for l in range(1, depth)] + [final_norm_g]

    x2 = x.reshape(m, d_model)
    xg, r = _prenorm(x2, ln_mix_g[0])
    for l in range(depth):
        raw_rows = jnp.broadcast_to(
            ret_log_decay[l].astype(F32).reshape(2 * N_RET_HEADS, 1), (2 * N_RET_HEADS, HEAD_DIM))
        sink_rows = jnp.broadcast_to(
            attn_sink[l].astype(F32).reshape(N_Q_HEADS, 1), (N_Q_HEADS, HEAD_DIM))
        proj = _scaled_matmul(xg, r, w_in_b[l], relu2=False, out_dtype=F32)
        sb = _ret_bwd_states(proj, cos, sgn_sin, raw_rows, batch=batch, n_chunks=n_chunks)
        mix = _mixers(proj, sb, cos3, sin3,
                      sgu_ln_g[l].reshape(1, SGU_WIDTH), sgu_ln_b[l].reshape(1, SGU_WIDTH),
                      sgu_w[l].astype(BF16), sgu_b[l].T, raw_rows, sink_rows,
                      batch=batch, n_chunks=n_chunks)
        x2, r, xg = _matmul_res(mix, w_out_b[l], x2, ln_mlp_g[l], tm=1024, tn=512, tk=MIX_WIDTH)
        hid = _scaled_matmul(xg, r, w_up_b[l], relu2=True, out_dtype=BF16)
        x2, r, *maybe_xg = _matmul_res(hid, w_down_b[l], x2, next_gains[l], tm=1024, tn=1024, tk=2048,
                                       emit_xg=l + 1 < depth)
        xg = maybe_xg[0] if maybe_xg else None
    return _final_scale(x2, r, final_norm_g).reshape(batch, seq, d_model)
```

```python
import functools

import numpy as np
import jax
import jax.numpy as jnp
from jax import lax
from jax.experimental import pallas as pl
from jax.experimental.pallas import tpu as pltpu

HEAD_DIM = 128
CHUNK = 128
WINDOW = 128
ROPE_THETA = 10000.0
NORM_EPS = 1e-5
NEG_INF = -1e30
N_SGU_GROUPS = 8
N_RET_HEADS = 8
N_Q_HEADS = 16
N_KV_HEADS = 4
Q_PER_KV = N_Q_HEADS // N_KV_HEADS

SGU_WIDTH = N_SGU_GROUPS * HEAD_DIM
RET_WIDTH = N_RET_HEADS * HEAD_DIM
ATT_WIDTH = N_Q_HEADS * HEAD_DIM
KV_WIDTH = N_KV_HEADS * HEAD_DIM
OFF_U = 0
OFF_V = OFF_U + SGU_WIDTH
OFF_RQ = OFF_V + SGU_WIDTH
OFF_RK = OFF_RQ + RET_WIDTH
OFF_RV = OFF_RK + RET_WIDTH
OFF_RG = OFF_RV + RET_WIDTH
OFF_AQ = OFF_RG + RET_WIDTH
OFF_AK = OFF_AQ + ATT_WIDTH
OFF_AV = OFF_AK + KV_WIDTH
IN_WIDTH = OFF_AV + KV_WIDTH
MIX_WIDTH = SGU_WIDTH + RET_WIDTH + ATT_WIDTH

V7X_VMEM_BYTES = 64 * 1024 * 1024
V7X_VMEM_BUDGET = 56 * 1024 * 1024
CAST_SLAB_BYTES = 4 * 1024 * 1024
BF16_SUBLANES = 16

F32 = jnp.float32
BF16 = jnp.bfloat16


def _vmem_limit(est_bytes):
    assert est_bytes <= V7X_VMEM_BUDGET, est_bytes
    return min(max(int(est_bytes * 1.25), 16 << 20), V7X_VMEM_BUDGET)


def _dot(a, b):
    return jnp.dot(a, b, preferred_element_type=F32)


def _dot_nt(a, b):
    return lax.dot_general(a, b, (((1,), (1,)), ((), ())), preferred_element_type=F32)


def _dot_tn(a, b):
    return lax.dot_general(a, b, (((0,), (0,)), ((), ())), preferred_element_type=F32)


def _cast_kernel(w_ref, o_ref):
    o_ref[...] = w_ref[...].astype(o_ref.dtype)


def _cast_layer(w, layer):
    _, k, n = w.shape
    tr = 1
    while tr * 2 * n * 4 <= CAST_SLAB_BYTES and k % (tr * 2) == 0:
        tr *= 2
    return pl.pallas_call(
        _cast_kernel,
        grid=(k // tr,),
        in_specs=[pl.BlockSpec((None, tr, n), lambda i: (layer, i, 0))],
        out_specs=pl.BlockSpec((tr, n), lambda i: (i, 0)),
        out_shape=jax.ShapeDtypeStruct((k, n), BF16),
        compiler_params=pltpu.CompilerParams(
            dimension_semantics=("parallel",),
            vmem_limit_bytes=_vmem_limit(2 * tr * n * 6 + tr * n * 4)),
        name="cast_bf16",
    )(w)


def _side_cast_specs(side_casts, steps, step_index):
    in_specs, out_specs, out_shape, vmem = [], [], [], 0
    for w, layer in side_casts:
        _, rows, cols = w.shape
        n_slabs = 1
        while n_slabs * 2 <= steps and rows % (n_slabs * 2) == 0 and (rows // (n_slabs * 2)) % BF16_SUBLANES == 0:
            n_slabs *= 2
        slab = rows // n_slabs

        def slab_index(*ids, _last=n_slabs - 1):
            return jnp.minimum(step_index(*ids), _last)

        in_specs.append(pl.BlockSpec(
            (None, slab, cols), lambda *ids, _l=layer, _s=slab_index: (_l, _s(*ids), 0)))
        out_specs.append(pl.BlockSpec((slab, cols), lambda *ids, _s=slab_index: (_s(*ids), 0)))
        out_shape.append(jax.ShapeDtypeStruct((rows, cols), BF16))
        vmem += 2 * slab * cols * 6 + slab * cols * 4
    return in_specs, out_specs, out_shape, vmem


def _prenorm_kernel(x_ref, g_ref, xg_ref, r_ref):
    x = x_ref[...]
    xg_ref[...] = (x * g_ref[...]).astype(BF16)
    r_ref[...] = lax.rsqrt(jnp.mean(x * x, axis=-1, keepdims=True) + NORM_EPS)


def _prenorm(x, g, *, tm=256):
    m, k = x.shape
    return pl.pallas_call(
        _prenorm_kernel,
        grid=(m // tm,),
        in_specs=[pl.BlockSpec((tm, k), lambda i: (i, 0)),
                  pl.BlockSpec((1, k), lambda i: (0, 0))],
        out_specs=[pl.BlockSpec((tm, k), lambda i: (i, 0)),
                   pl.BlockSpec((tm, 1), lambda i: (i, 0))],
        out_shape=[jax.ShapeDtypeStruct((m, k), BF16), jax.ShapeDtypeStruct((m, 1), F32)],
        compiler_params=pltpu.CompilerParams(
            dimension_semantics=("parallel",),
            vmem_limit_bytes=_vmem_limit(2 * tm * k * 6 + 2 * tm * k * 4)),
        name="prenorm",
    )(x, g.reshape(1, k))


def _scaled_matmul_kernel(a_ref, r_ref, w_ref, *rest, relu2, n_side):
    side_in, o_ref, side_out = rest[:n_side], rest[n_side], rest[n_side + 1:]
    acc = _dot(a_ref[...], w_ref[...]) * r_ref[...]
    if relu2:
        acc = jnp.square(jnp.maximum(acc, 0.0))
    o_ref[...] = acc.astype(o_ref.dtype)
    for src_ref, dst_ref in zip(side_in, side_out):
        dst_ref[...] = src_ref[...].astype(dst_ref.dtype)


def _scaled_matmul(a, r, w, *, relu2, out_dtype, side_casts=(), tm=1024, tn=1024):
    m, k = a.shape
    n = w.shape[1]
    assert m % tm == 0 and n % tn == 0
    nj = n // tn
    out_bytes = jnp.dtype(out_dtype).itemsize
    side_in, side_out, side_shape, side_vmem = _side_cast_specs(
        side_casts, (m // tm) * nj, lambda i, j: i * nj + j)
    est = 2 * tm * k * 2 + 2 * k * tn * 2 + 2 * tm * tn * out_bytes + 2 * tm * tn * 4 + side_vmem
    return pl.pallas_call(
        functools.partial(_scaled_matmul_kernel, relu2=relu2, n_side=len(side_casts)),
        grid=(m // tm, nj),
        in_specs=[
            pl.BlockSpec((tm, k), lambda i, j: (i, 0)),
            pl.BlockSpec((tm, 1), lambda i, j: (i, 0)),
            pl.BlockSpec((k, tn), lambda i, j: (0, j)),
            *side_in,
        ],
        out_specs=[pl.BlockSpec((tm, tn), lambda i, j: (i, j)), *side_out],
        out_shape=[jax.ShapeDtypeStruct((m, n), out_dtype), *side_shape],
        compiler_params=pltpu.CompilerParams(
            dimension_semantics=("arbitrary", "arbitrary"),
            vmem_limit_bytes=_vmem_limit(est)),
        name="scaled_matmul_relu2" if relu2 else "scaled_matmul",
    )(a, r, w, *[src for src, _ in side_casts])


def _matmul_res_kernel(a_ref, w_ref, res_ref, g_ref, o_ref, r_ref, *rest, nk, nj, width):
    maybe_xg_ref, acc_ref = (rest[0], rest[1]) if len(rest) == 2 else (None, rest[0])
    j = pl.program_id(1)
    kk = pl.program_id(2)

    def finish(x_new):
        o_ref[...] = x_new
        if maybe_xg_ref is not None:
            maybe_xg_ref[...] = (x_new * g_ref[...]).astype(BF16)
        ss = jnp.sum(x_new * x_new, axis=-1, keepdims=True)

        @pl.when(j == 0)
        def _():
            r_ref[...] = ss

        @pl.when(j > 0)
        def _():
            r_ref[...] += ss

        @pl.when(j == nj - 1)
        def _():
            r_ref[...] = lax.rsqrt(r_ref[...] / float(width) + NORM_EPS)

    if nk == 1:
        finish(res_ref[...] + _dot(a_ref[...], w_ref[...]))
        return

    @pl.when(kk == 0)
    def _():
        acc_ref[...] = res_ref[...] + _dot(a_ref[...], w_ref[...])

    @pl.when(jnp.logical_and(kk > 0, kk < nk - 1))
    def _():
        acc_ref[...] += _dot(a_ref[...], w_ref[...])

    @pl.when(kk == nk - 1)
    def _():
        finish(acc_ref[...] + _dot(a_ref[...], w_ref[...]))


def _matmul_res(a, w, res, g_next, *, tm, tn, tk, emit_xg=True):
    m, k = a.shape
    n = w.shape[1]
    assert m % tm == 0 and n % tn == 0 and k % tk == 0
    nk, nj = k // tk, n // tn
    acc_shape = (tm, tn) if nk > 1 else (8, 128)
    est = (2 * tm * tk * 2 + 2 * tk * tn * 2 + 2 * tm * tn * (4 + 4 + 2 * emit_xg)
           + acc_shape[0] * acc_shape[1] * 4 + 2 * tm * tn * 4)
    tile = pl.BlockSpec((tm, tn), lambda i, j, kk: (i, j))
    out_specs = [tile, pl.BlockSpec((tm, 1), lambda i, j, kk: (i, 0))]
    out_shape = [jax.ShapeDtypeStruct((m, n), F32), jax.ShapeDtypeStruct((m, 1), F32)]
    if emit_xg:
        out_specs.append(tile)
        out_shape.append(jax.ShapeDtypeStruct((m, n), BF16))
    return pl.pallas_call(
        functools.partial(_matmul_res_kernel, nk=nk, nj=nj, width=n),
        grid=(m // tm, nj, nk),
        in_specs=[
            pl.BlockSpec((tm, tk), lambda i, j, kk: (i, kk)),
            pl.BlockSpec((tk, tn), lambda i, j, kk: (kk, j)),
            tile,
            pl.BlockSpec((1, tn), lambda i, j, kk: (0, j)),
        ],
        out_specs=out_specs,
        out_shape=out_shape,
        scratch_shapes=[pltpu.VMEM(acc_shape, F32)],
        compiler_params=pltpu.CompilerParams(
            dimension_semantics=("parallel", "arbitrary", "arbitrary"),
            vmem_limit_bytes=_vmem_limit(est)),
        name="matmul_res_k%d" % nk,
    )(a, w, res, g_next.reshape(1, n))


def _final_scale_kernel(x_ref, r_ref, g_ref, o_ref):
    o_ref[...] = x_ref[...] * r_ref[...] * g_ref[...]


def _final_scale(x, r, g, *, tm=512):
    m, k = x.shape
    return pl.pallas_call(
        _final_scale_kernel,
        grid=(m // tm,),
        in_specs=[pl.BlockSpec((tm, k), lambda i: (i, 0)),
                  pl.BlockSpec((tm, 1), lambda i: (i, 0)),
                  pl.BlockSpec((1, k), lambda i: (0, 0))],
        out_specs=pl.BlockSpec((tm, k), lambda i: (i, 0)),
        out_shape=jax.ShapeDtypeStruct((m, k), F32),
        compiler_params=pltpu.CompilerParams(
            dimension_semantics=("parallel",),
            vmem_limit_bytes=_vmem_limit(5 * tm * k * 4)),
        name="final_scale",
    )(x, r, g.reshape(1, k))


def _rope(t, cos, sgn_sin):
    return t * cos + pltpu.roll(t, HEAD_DIM // 2, 1) * sgn_sin


def _row_index_f32():
    return lax.broadcasted_iota(jnp.int32, (CHUNK, HEAD_DIM), 0).astype(F32)


def _head(ref, off, h):
    return ref[:, off + h * HEAD_DIM: off + (h + 1) * HEAD_DIM]


def _ret_bwd_state_kernel(k_ref, v_ref, cos_ref, sin_ref, raw_ref, sb_ref, st_ref):
    @pl.when(pl.program_id(1) == 0)
    def _():
        st_ref[...] = jnp.zeros_like(st_ref)

    cos = cos_ref[...]
    sin = sin_ref[...]
    row = _row_index_f32()
    scale = HEAD_DIM ** -0.5
    for h in range(N_RET_HEADS):
        rows = slice(h * HEAD_DIM, (h + 1) * HEAD_DIM)
        st = st_ref[rows, :]
        sb_ref[0, 0, rows, :] = st.astype(BF16)
        lgb = -jnp.exp(raw_ref[N_RET_HEADS + h: N_RET_HEADS + h + 1, :])
        kh = _rope(_head(k_ref, 0, h), cos, sin) * scale
        kw = (kh * jnp.exp(lgb * row)).astype(BF16)
        kv = _dot_tn(kw, _head(v_ref, 0, h).astype(BF16))
        st_ref[rows, :] = jnp.exp(lgb * float(CHUNK)) * st + kv


def _ret_bwd_states(proj, cos, sgn_sin, raw_rows, *, batch, n_chunks):
    assert OFF_RK % RET_WIDTH == 0 and OFF_RV % RET_WIDTH == 0
    kblk, vblk = OFF_RK // RET_WIDTH, OFF_RV // RET_WIDTH
    last = n_chunks - 1
    return pl.pallas_call(
        _ret_bwd_state_kernel,
        grid=(batch, n_chunks),
        in_specs=[
            pl.BlockSpec((CHUNK, RET_WIDTH), lambda b, t: (b * n_chunks + last - t, kblk)),
            pl.BlockSpec((CHUNK, RET_WIDTH), lambda b, t: (b * n_chunks + last - t, vblk)),
            pl.BlockSpec((CHUNK, HEAD_DIM), lambda b, t: (last - t, 0)),
            pl.BlockSpec((CHUNK, HEAD_DIM), lambda b, t: (last - t, 0)),
            pl.BlockSpec((2 * N_RET_HEADS, HEAD_DIM), lambda b, t: (0, 0)),
        ],
        out_specs=pl.BlockSpec((1, 1, RET_WIDTH, HEAD_DIM), lambda b, t: (b, last - t, 0, 0)),
        out_shape=jax.ShapeDtypeStruct((batch, n_chunks, RET_WIDTH, HEAD_DIM), BF16),
        scratch_shapes=[pltpu.VMEM((RET_WIDTH, HEAD_DIM), F32)],
        compiler_params=pltpu.CompilerParams(
            dimension_semantics=("parallel", "arbitrary"),
            vmem_limit_bytes=_vmem_limit(8 << 20)),
        name="ret_bwd_states",
    )(proj, proj, cos, sgn_sin, raw_rows)


def _sgu(p_ref, lng_ref, lnb_ref, ws_ref, bs_ref, o_ref):
    sqrt_half = np.sqrt(0.5).astype(np.float32)

    def gelu(t):
        return 0.5 * t * (1.0 + lax.erf(t * sqrt_half))

    u = gelu(p_ref[:, OFF_U:OFF_U + SGU_WIDTH])
    v = gelu(p_ref[:, OFF_V:OFF_V + SGU_WIDTH])
    mu = jnp.mean(v, axis=-1, keepdims=True)
    var = jnp.mean(jnp.square(v - mu), axis=-1, keepdims=True)
    vn = (v - mu) * lax.rsqrt(var + NORM_EPS) * lng_ref[...] + lnb_ref[...]
    for g in range(N_SGU_GROUPS):
        cols = slice(g * HEAD_DIM, (g + 1) * HEAD_DIM)
        s = _dot(ws_ref[g], vn[:, cols].astype(BF16)) + bs_ref[:, g:g + 1]
        o_ref[:, cols] = (u[:, cols] * s).astype(o_ref.dtype)


N_RET_TABLES = 4


def _ret_table(tab_ref, h, t):
    base = (h * N_RET_TABLES + t) * CHUNK
    return tab_ref.at[base:base + CHUNK, :]


def _fill_ret_tables(raw_ref, tab_ref):
    row = _row_index_f32()
    col = lax.broadcasted_iota(jnp.int32, (CHUNK, CHUNK), 1).astype(F32)
    delta = row - col
    for h in range(N_RET_HEADS):
        lgf = -jnp.exp(raw_ref[h:h + 1, :])
        lgb = -jnp.exp(raw_ref[N_RET_HEADS + h: N_RET_HEADS + h + 1, :])
        _ret_table(tab_ref, h, 0)[...] = jnp.where(
            delta >= 0.0,
            jnp.exp(lgf * jnp.maximum(delta, 0.0)),
            jnp.exp(lgb * jnp.maximum(-delta, 0.0)))
        _ret_table(tab_ref, h, 1)[...] = jnp.exp(lgf * (float(CHUNK - 1) - row))
        _ret_table(tab_ref, h, 2)[...] = jnp.exp(lgf * (row + 1.0))
        _ret_table(tab_ref, h, 3)[...] = jnp.exp(lgb * (float(CHUNK) - row))


def _retention(p_ref, sb_ref, raw_ref, cos, sin, o_ref, sf_ref, tab_ref, first_chunk):
    @pl.when(first_chunk)
    def _():
        sf_ref[...] = jnp.zeros_like(sf_ref)
        _fill_ret_tables(raw_ref, tab_ref)

    scale = HEAD_DIM ** -0.5
    for h in range(N_RET_HEADS):
        rows = slice(h * HEAD_DIM, (h + 1) * HEAD_DIM)
        lgf = -jnp.exp(raw_ref[h:h + 1, :])
        q = _rope(_head(p_ref, OFF_RQ, h), cos, sin)
        k = _rope(_head(p_ref, OFF_RK, h), cos, sin) * scale
        v = _head(p_ref, OFF_RV, h).astype(BF16)
        scores = _dot_nt(q.astype(BF16), k.astype(BF16)) * _ret_table(tab_ref, h, 0)[...]
        inner = _dot(scores.astype(BF16), v)
        sf_prev = sf_ref[rows, :]
        kv_f = _dot_tn((k * _ret_table(tab_ref, h, 1)[...]).astype(BF16), v)
        cross_f = _dot((q * _ret_table(tab_ref, h, 2)[...]).astype(BF16), sf_prev.astype(BF16))
        sf_ref[rows, :] = jnp.exp(lgf * float(CHUNK)) * sf_prev + kv_f
        cross_b = _dot((q * _ret_table(tab_ref, h, 3)[...]).astype(BF16), sb_ref[0, 0, rows, :])
        rf = (inner + cross_f) + cross_b
        rn = rf * lax.rsqrt(jnp.mean(rf * rf, axis=-1, keepdims=True) + NORM_EPS)
        gate = _head(p_ref, OFF_RG, h)
        o_ref[:, SGU_WIDTH + h * HEAD_DIM: SGU_WIDTH + (h + 1) * HEAD_DIM] = (
            gate * jax.nn.sigmoid(gate) * rn).astype(o_ref.dtype)


def _attention(p_ref, kvp_ref, kvn_ref, cos3_ref, sin3_ref, sink_ref, o_ref, chunk, seq):
    cos3 = cos3_ref[...]
    sin3 = sin3_ref[...]
    cos = cos3[CHUNK:2 * CHUNK]
    sin = sin3[CHUNK:2 * CHUNK]
    scale = HEAD_DIM ** -0.5
    qi = lax.broadcasted_iota(jnp.int32, (CHUNK, 3 * CHUNK), 0)
    kj = lax.broadcasted_iota(jnp.int32, (CHUNK, 3 * CHUNK), 1)
    rel = kj - CHUNK - qi
    kpos = (chunk - 1) * CHUNK + kj
    valid = (jnp.abs(rel) <= WINDOW) & (kpos >= 0) & (kpos < seq)
    valid = jnp.concatenate([valid] * Q_PER_KV, axis=0)
    out_off = SGU_WIDTH + RET_WIDTH
    for kh in range(N_KV_HEADS):
        kcols = slice(kh * HEAD_DIM, (kh + 1) * HEAD_DIM)
        vcols = slice(KV_WIDTH + kh * HEAD_DIM, KV_WIDTH + (kh + 1) * HEAD_DIM)
        k3 = jnp.concatenate([kvp_ref[:, kcols], _head(p_ref, OFF_AK, kh), kvn_ref[:, kcols]], axis=0)
        v3 = jnp.concatenate([kvp_ref[:, vcols], _head(p_ref, OFF_AV, kh), kvn_ref[:, vcols]], axis=0)
        k3 = _rope(k3, cos3, sin3).astype(BF16)
        qs = jnp.concatenate(
            [_rope(_head(p_ref, OFF_AQ, kh * Q_PER_KV + g), cos, sin) * scale for g in range(Q_PER_KV)],
            axis=0)
        sink = jnp.concatenate(
            [jnp.broadcast_to(sink_ref[kh * Q_PER_KV + g: kh * Q_PER_KV + g + 1, 0:1], (CHUNK, 1))
             for g in range(Q_PER_KV)], axis=0)
        s = jnp.where(valid, _dot_nt(qs.astype(BF16), k3), NEG_INF)
        m = jnp.maximum(jnp.max(s, axis=-1, keepdims=True), sink)
        p = jnp.exp(s - m)
        inv = 1.0 / (jnp.sum(p, axis=-1, keepdims=True) + jnp.exp(sink - m))
        o = _dot((p * inv).astype(BF16), v3.astype(BF16))
        for g in range(Q_PER_KV):
            hq = kh * Q_PER_KV + g
            o_ref[:, out_off + hq * HEAD_DIM: out_off + (hq + 1) * HEAD_DIM] = (
                o[g * CHUNK:(g + 1) * CHUNK]).astype(o_ref.dtype)


def _mixer_kernel(p_ref, kvp_ref, kvn_ref, cos3_ref, sin3_ref, sb_ref, lng_ref, lnb_ref,
                  ws_ref, bs_ref, raw_ref, sink_ref, cast_src_ref, o_ref, cast_dst_ref,
                  sf_ref, tab_ref, *, seq):
    chunk = pl.program_id(1)
    cast_dst_ref[...] = cast_src_ref[...].astype(cast_dst_ref.dtype)
    _sgu(p_ref, lng_ref, lnb_ref, ws_ref, bs_ref, o_ref)
    cos = cos3_ref[CHUNK:2 * CHUNK, :]
    sin = sin3_ref[CHUNK:2 * CHUNK, :]
    _retention(p_ref, sb_ref, raw_ref, cos, sin, o_ref, sf_ref, tab_ref, chunk == 0)
    _attention(p_ref, kvp_ref, kvn_ref, cos3_ref, sin3_ref, sink_ref, o_ref, chunk, seq)


def _mixers(proj, sb, cos3, sin3, lng, lnb, ws, bs_t, raw_rows, sink_rows, side_cast,
            *, batch, n_chunks):
    m = proj.shape[0]
    kv_blk = OFF_AK // (2 * KV_WIDTH)
    assert kv_blk * 2 * KV_WIDTH == OFF_AK
    last = n_chunks - 1
    const2 = lambda b, n: (0, 0)
    side_in, side_out, side_shape, side_vmem = _side_cast_specs(
        (side_cast,), batch * n_chunks, lambda b, n: b * n_chunks + n)
    est = (2 * CHUNK * IN_WIDTH * 4 + 4 * CHUNK * 2 * KV_WIDTH * 4 + 2 * CHUNK * MIX_WIDTH * 2
           + (12 << 20) + side_vmem)
    return pl.pallas_call(
        functools.partial(_mixer_kernel, seq=n_chunks * CHUNK),
        grid=(batch, n_chunks),
        in_specs=[
            pl.BlockSpec((CHUNK, IN_WIDTH), lambda b, n: (b * n_chunks + n, 0)),
            pl.BlockSpec((CHUNK, 2 * KV_WIDTH),
                         lambda b, n: (b * n_chunks + jnp.maximum(n - 1, 0), kv_blk)),
            pl.BlockSpec((CHUNK, 2 * KV_WIDTH),
                         lambda b, n: (b * n_chunks + jnp.minimum(n + 1, last), kv_blk)),
            pl.BlockSpec((3 * CHUNK, HEAD_DIM), lambda b, n: (n, 0)),
            pl.BlockSpec((3 * CHUNK, HEAD_DIM), lambda b, n: (n, 0)),
            pl.BlockSpec((1, 1, RET_WIDTH, HEAD_DIM), lambda b, n: (b, n, 0, 0)),
            pl.BlockSpec((1, SGU_WIDTH), const2),
            pl.BlockSpec((1, SGU_WIDTH), const2),
            pl.BlockSpec((N_SGU_GROUPS, CHUNK, CHUNK), lambda b, n: (0, 0, 0)),
            pl.BlockSpec((CHUNK, N_SGU_GROUPS), const2),
            pl.BlockSpec((2 * N_RET_HEADS, HEAD_DIM), const2),
            pl.BlockSpec((N_Q_HEADS, HEAD_DIM), const2),
            *side_in,
        ],
        out_specs=[pl.BlockSpec((CHUNK, MIX_WIDTH), lambda b, n: (b * n_chunks + n, 0)), *side_out],
        out_shape=[jax.ShapeDtypeStruct((m, MIX_WIDTH), BF16), *side_shape],
        scratch_shapes=[pltpu.VMEM((RET_WIDTH, HEAD_DIM), F32),
                        pltpu.VMEM((N_RET_HEADS * N_RET_TABLES * CHUNK, HEAD_DIM), F32)],
        compiler_params=pltpu.CompilerParams(
            dimension_semantics=("arbitrary", "arbitrary"),
            vmem_limit_bytes=_vmem_limit(est)),
        name="mixers",
    )(proj, proj, proj, cos3, sin3, sb, lng, lnb, ws, bs_t, raw_rows, sink_rows, side_cast[0])


def _rope_tables(seq):
    pos = jnp.arange(seq, dtype=F32)
    inv = ROPE_THETA ** (-jnp.arange(0, HEAD_DIM, 2, dtype=F32) / HEAD_DIM)
    ang = pos[:, None] * inv[None, :]
    ang = jnp.concatenate([ang, ang], axis=-1)
    sgn = jnp.where(jnp.arange(HEAD_DIM) < HEAD_DIM // 2, -1.0, 1.0).astype(F32)
    return jnp.cos(ang), jnp.sin(ang) * sgn[None, :]


def _three_chunk_view(t, n_chunks):
    seq = t.shape[0]
    pad = jnp.zeros((CHUNK, HEAD_DIM), t.dtype)
    tp = jnp.concatenate([pad, t, pad], axis=0)
    parts = [tp[s * CHUNK: s * CHUNK + seq].reshape(n_chunks, CHUNK, HEAD_DIM) for s in range(3)]
    return jnp.concatenate(parts, axis=1).reshape(n_chunks * 3 * CHUNK, HEAD_DIM)


def kernel(x, ln_mix_g, w_in, sgu_ln_g, sgu_ln_b, sgu_w, sgu_b, ret_log_decay, attn_sink,
           w_out, ln_mlp_g, w_up, w_down, final_norm_g):
    batch, seq, d_model = x.shape
    depth = w_in.shape[0]
    assert w_in.shape[2] == IN_WIDTH and w_out.shape[1] == MIX_WIDTH and seq % CHUNK == 0
    n_chunks = seq // CHUNK
    m = batch * seq

    cos, sgn_sin = _rope_tables(seq)
    cos3 = _three_chunk_view(cos, n_chunks)
    sin3 = _three_chunk_view(sgn_sin, n_chunks)

    next_gains = [ln_mix_g[l] for l in range(1, depth)] + [final_norm_g]

    w_in_b = _cast_layer(w_in, 0)
    x2 = x.reshape(m, d_model)
    xg, r = _prenorm(x2, ln_mix_g[0])
    for l in range(depth):
        raw_rows = jnp.broadcast_to(
            ret_log_decay[l].astype(F32).reshape(2 * N_RET_HEADS, 1), (2 * N_RET_HEADS, HEAD_DIM))
        sink_rows = jnp.broadcast_to(
            attn_sink[l].astype(F32).reshape(N_Q_HEADS, 1), (N_Q_HEADS, HEAD_DIM))
        proj, w_out_b = _scaled_matmul(
            xg, r, w_in_b, relu2=False, out_dtype=F32, side_casts=((w_out, l),))
        sb = _ret_bwd_states(proj, cos, sgn_sin, raw_rows, batch=batch, n_chunks=n_chunks)
        mix, w_up_b = _mixers(proj, sb, cos3, sin3,
                              sgu_ln_g[l].reshape(1, SGU_WIDTH), sgu_ln_b[l].reshape(1, SGU_WIDTH),
                              sgu_w[l].astype(BF16), sgu_b[l].T, raw_rows, sink_rows, (w_up, l),
                              batch=batch, n_chunks=n_chunks)
        x2, r, xg = _matmul_res(mix, w_out_b, x2, ln_mlp_g[l], tm=1024, tn=512, tk=MIX_WIDTH)
        last = l + 1 == depth
        hid, w_down_b, *maybe_w_in = _scaled_matmul(
            xg, r, w_up_b, relu2=True, out_dtype=BF16,
            side_casts=((w_down, l),) if last else ((w_down, l), (w_in, l + 1)))
        w_in_b = None if last else maybe_w_in[0]
        x2, r, *maybe_xg = _matmul_res(hid, w_down_b, x2, next_gains[l], tm=1024, tn=1024, tk=2048,
                                       emit_xg=not last)
        xg = None if last else maybe_xg[0]
    return _final_scale(x2, r, final_norm_g).reshape(batch, seq, d_model)
```

```python
import functools

import numpy as np
import jax
import jax.numpy as jnp
from jax import lax
from jax.experimental import pallas as pl
from jax.experimental.pallas import tpu as pltpu

HEAD_DIM = 128
CHUNK = 128
WINDOW = 128
ROPE_THETA = 10000.0
NORM_EPS = 1e-5
NEG_INF = -1e30
N_SGU_GROUPS = 8
N_RET_HEADS = 8
N_Q_HEADS = 16
N_KV_HEADS = 4
Q_PER_KV = N_Q_HEADS // N_KV_HEADS

SGU_WIDTH = N_SGU_GROUPS * HEAD_DIM
RET_WIDTH = N_RET_HEADS * HEAD_DIM
ATT_WIDTH = N_Q_HEADS * HEAD_DIM
KV_WIDTH = N_KV_HEADS * HEAD_DIM
OFF_U = 0
OFF_V = OFF_U + SGU_WIDTH
OFF_RQ = OFF_V + SGU_WIDTH
OFF_RK = OFF_RQ + RET_WIDTH
OFF_RV = OFF_RK + RET_WIDTH
OFF_RG = OFF_RV + RET_WIDTH
OFF_AQ = OFF_RG + RET_WIDTH
OFF_AK = OFF_AQ + ATT_WIDTH
OFF_AV = OFF_AK + KV_WIDTH
IN_WIDTH = OFF_AV + KV_WIDTH
MIX_WIDTH = SGU_WIDTH + RET_WIDTH + ATT_WIDTH

V7X_VMEM_BYTES = 64 * 1024 * 1024
V7X_VMEM_BUDGET = 56 * 1024 * 1024
CAST_SLAB_BYTES = 4 * 1024 * 1024
BF16_SUBLANES = 16

F32 = jnp.float32
BF16 = jnp.bfloat16


def _vmem_limit(est_bytes):
    assert est_bytes <= V7X_VMEM_BUDGET, est_bytes
    return min(max(int(est_bytes * 1.25), 16 << 20), V7X_VMEM_BUDGET)


def _dot(a, b):
    return jnp.dot(a, b, preferred_element_type=F32)


def _dot_nt(a, b):
    return lax.dot_general(a, b, (((1,), (1,)), ((), ())), preferred_element_type=F32)


def _dot_tn(a, b):
    return lax.dot_general(a, b, (((0,), (0,)), ((), ())), preferred_element_type=F32)


def _cast_kernel(w_ref, o_ref):
    o_ref[...] = w_ref[...].astype(o_ref.dtype)


def _cast_layer(w, layer):
    _, k, n = w.shape
    tr = 1
    while tr * 2 * n * 4 <= CAST_SLAB_BYTES and k % (tr * 2) == 0:
        tr *= 2
    return pl.pallas_call(
        _cast_kernel,
        grid=(k // tr,),
        in_specs=[pl.BlockSpec((None, tr, n), lambda i: (layer, i, 0))],
        out_specs=pl.BlockSpec((tr, n), lambda i: (i, 0)),
        out_shape=jax.ShapeDtypeStruct((k, n), BF16),
        compiler_params=pltpu.CompilerParams(
            dimension_semantics=("parallel",),
            vmem_limit_bytes=_vmem_limit(2 * tr * n * 6 + tr * n * 4)),
        name="cast_bf16",
    )(w)


def _side_cast_specs(side_casts, steps, step_index):
    in_specs, out_specs, out_shape, vmem = [], [], [], 0
    for w, layer in side_casts:
        _, rows, cols = w.shape
        n_slabs = 1
        while n_slabs * 2 <= steps and rows % (n_slabs * 2) == 0 and (rows // (n_slabs * 2)) % BF16_SUBLANES == 0:
            n_slabs *= 2
        slab = rows // n_slabs

        def slab_index(*ids, _last=n_slabs - 1):
            return jnp.minimum(step_index(*ids), _last)

        in_specs.append(pl.BlockSpec(
            (None, slab, cols), lambda *ids, _l=layer, _s=slab_index: (_l, _s(*ids), 0)))
        out_specs.append(pl.BlockSpec((slab, cols), lambda *ids, _s=slab_index: (_s(*ids), 0)))
        out_shape.append(jax.ShapeDtypeStruct((rows, cols), BF16))
        vmem += 2 * slab * cols * 6 + slab * cols * 4
    return in_specs, out_specs, out_shape, vmem


def _prenorm_kernel(x_ref, g_ref, xg_ref, r_ref):
    x = x_ref[...]
    xg_ref[...] = (x * g_ref[...]).astype(BF16)
    r_ref[...] = lax.rsqrt(jnp.mean(x * x, axis=-1, keepdims=True) + NORM_EPS)


def _prenorm(x, g, *, tm=256):
    m, k = x.shape
    return pl.pallas_call(
        _prenorm_kernel,
        grid=(m // tm,),
        in_specs=[pl.BlockSpec((tm, k), lambda i: (i, 0)),
                  pl.BlockSpec((1, k), lambda i: (0, 0))],
        out_specs=[pl.BlockSpec((tm, k), lambda i: (i, 0)),
                   pl.BlockSpec((tm, 1), lambda i: (i, 0))],
        out_shape=[jax.ShapeDtypeStruct((m, k), BF16), jax.ShapeDtypeStruct((m, 1), F32)],
        compiler_params=pltpu.CompilerParams(
            dimension_semantics=("parallel",),
            vmem_limit_bytes=_vmem_limit(2 * tm * k * 6 + 2 * tm * k * 4)),
        name="prenorm",
    )(x, g.reshape(1, k))


def _scaled_matmul_kernel(a_ref, r_ref, w_ref, *rest, relu2, n_side):
    side_in, o_ref, side_out = rest[:n_side], rest[n_side], rest[n_side + 1:]
    acc = _dot(a_ref[...], w_ref[...]) * r_ref[...]
    if relu2:
        acc = jnp.square(jnp.maximum(acc, 0.0))
    o_ref[...] = acc.astype(o_ref.dtype)
    for src_ref, dst_ref in zip(side_in, side_out):
        dst_ref[...] = src_ref[...].astype(dst_ref.dtype)


def _scaled_matmul(a, r, w, *, relu2, out_dtype, side_casts=(), tm=1024, tn=1024):
    m, k = a.shape
    n = w.shape[1]
    assert m % tm == 0 and n % tn == 0
    nj = n // tn
    out_bytes = jnp.dtype(out_dtype).itemsize
    side_in, side_out, side_shape, side_vmem = _side_cast_specs(
        side_casts, (m // tm) * nj, lambda i, j: i * nj + j)
    est = 2 * tm * k * 2 + 2 * k * tn * 2 + 2 * tm * tn * out_bytes + 2 * tm * tn * 4 + side_vmem
    return pl.pallas_call(
        functools.partial(_scaled_matmul_kernel, relu2=relu2, n_side=len(side_casts)),
        grid=(m // tm, nj),
        in_specs=[
            pl.BlockSpec((tm, k), lambda i, j: (i, 0)),
            pl.BlockSpec((tm, 1), lambda i, j: (i, 0)),
            pl.BlockSpec((k, tn), lambda i, j: (0, j)),
            *side_in,
        ],
        out_specs=[pl.BlockSpec((tm, tn), lambda i, j: (i, j)), *side_out],
        out_shape=[jax.ShapeDtypeStruct((m, n), out_dtype), *side_shape],
        compiler_params=pltpu.CompilerParams(
            dimension_semantics=("arbitrary", "arbitrary"),
            vmem_limit_bytes=_vmem_limit(est)),
        name="scaled_matmul_relu2" if relu2 else "scaled_matmul",
    )(a, r, w, *[src for src, _ in side_casts])


def _matmul_res_kernel(a_ref, w_ref, res_ref, g_ref, o_ref, r_ref, *rest, nk, nj, width):
    maybe_xg_ref, acc_ref = (rest[0], rest[1]) if len(rest) == 2 else (None, rest[0])
    j = pl.program_id(1)
    kk = pl.program_id(2)

    def finish(x_new):
        o_ref[...] = x_new
        if maybe_xg_ref is not None:
            maybe_xg_ref[...] = (x_new * g_ref[...]).astype(BF16)
        ss = jnp.sum(x_new * x_new, axis=-1, keepdims=True)

        @pl.when(j == 0)
        def _():
            r_ref[...] = ss

        @pl.when(j > 0)
        def _():
            r_ref[...] += ss

        @pl.when(j == nj - 1)
        def _():
            r_ref[...] = lax.rsqrt(r_ref[...] / float(width) + NORM_EPS)

    if nk == 1:
        finish(res_ref[...] + _dot(a_ref[...], w_ref[...]))
        return

    @pl.when(kk == 0)
    def _():
        acc_ref[...] = res_ref[...] + _dot(a_ref[...], w_ref[...])

    @pl.when(jnp.logical_and(kk > 0, kk < nk - 1))
    def _():
        acc_ref[...] += _dot(a_ref[...], w_ref[...])

    @pl.when(kk == nk - 1)
    def _():
        finish(acc_ref[...] + _dot(a_ref[...], w_ref[...]))


def _matmul_res(a, w, res, g_next, *, tm, tn, tk, emit_xg=True):
    m, k = a.shape
    n = w.shape[1]
    assert m % tm == 0 and n % tn == 0 and k % tk == 0
    nk, nj = k // tk, n // tn
    acc_shape = (tm, tn) if nk > 1 else (8, 128)
    est = (2 * tm * tk * 2 + 2 * tk * tn * 2 + 2 * tm * tn * (4 + 4 + 2 * emit_xg)
           + acc_shape[0] * acc_shape[1] * 4 + 2 * tm * tn * 4)
    tile = pl.BlockSpec((tm, tn), lambda i, j, kk: (i, j))
    out_specs = [tile, pl.BlockSpec((tm, 1), lambda i, j, kk: (i, 0))]
    out_shape = [jax.ShapeDtypeStruct((m, n), F32), jax.ShapeDtypeStruct((m, 1), F32)]
    if emit_xg:
        out_specs.append(tile)
        out_shape.append(jax.ShapeDtypeStruct((m, n), BF16))
    return pl.pallas_call(
        functools.partial(_matmul_res_kernel, nk=nk, nj=nj, width=n),
        grid=(m // tm, nj, nk),
        in_specs=[
            pl.BlockSpec((tm, tk), lambda i, j, kk: (i, kk)),
            pl.BlockSpec((tk, tn), lambda i, j, kk: (kk, j)),
            tile,
            pl.BlockSpec((1, tn), lambda i, j, kk: (0, j)),
        ],
        out_specs=out_specs,
        out_shape=out_shape,
        scratch_shapes=[pltpu.VMEM(acc_shape, F32)],
        compiler_params=pltpu.CompilerParams(
            dimension_semantics=("parallel", "arbitrary", "arbitrary"),
            vmem_limit_bytes=_vmem_limit(est)),
        name="matmul_res_k%d" % nk,
    )(a, w, res, g_next.reshape(1, n))


def _final_scale_kernel(x_ref, r_ref, g_ref, o_ref):
    o_ref[...] = x_ref[...] * r_ref[...] * g_ref[...]


def _final_scale(x, r, g, *, tm=512):
    m, k = x.shape
    return pl.pallas_call(
        _final_scale_kernel,
        grid=(m // tm,),
        in_specs=[pl.BlockSpec((tm, k), lambda i: (i, 0)),
                  pl.BlockSpec((tm, 1), lambda i: (i, 0)),
                  pl.BlockSpec((1, k), lambda i: (0, 0))],
        out_specs=pl.BlockSpec((tm, k), lambda i: (i, 0)),
        out_shape=jax.ShapeDtypeStruct((m, k), F32),
        compiler_params=pltpu.CompilerParams(
            dimension_semantics=("parallel",),
            vmem_limit_bytes=_vmem_limit(5 * tm * k * 4)),
        name="final_scale",
    )(x, r, g.reshape(1, k))


def _rope(t, cos, sgn_sin):
    return t * cos + pltpu.roll(t, HEAD_DIM // 2, 1) * sgn_sin


def _row_index_f32():
    return lax.broadcasted_iota(jnp.int32, (CHUNK, HEAD_DIM), 0).astype(F32)


def _head(ref, off, h):
    return ref[:, off + h * HEAD_DIM: off + (h + 1) * HEAD_DIM]


BWD_CHUNKS_PER_STEP = 4


def _ret_bwd_state_kernel(k_ref, v_ref, cos_ref, sin_ref, raw_ref, sb_ref, st_ref):
    @pl.when(pl.program_id(1) == 0)
    def _():
        st_ref[...] = jnp.zeros_like(st_ref)

    row = _row_index_f32()
    scale = HEAD_DIM ** -0.5
    for c in reversed(range(BWD_CHUNKS_PER_STEP)):
        pos = slice(c * CHUNK, (c + 1) * CHUNK)
        cos = cos_ref[pos, :]
        sin = sin_ref[pos, :]
        for h in range(N_RET_HEADS):
            rows = slice(h * HEAD_DIM, (h + 1) * HEAD_DIM)
            cols = slice(h * HEAD_DIM, (h + 1) * HEAD_DIM)
            st = st_ref[rows, :]
            sb_ref[0, c, rows, :] = st.astype(BF16)
            lgb = -jnp.exp(raw_ref[N_RET_HEADS + h: N_RET_HEADS + h + 1, :])
            kh = _rope(k_ref[pos, cols], cos, sin) * scale
            kw = (kh * jnp.exp(lgb * row)).astype(BF16)
            kv = _dot_tn(kw, v_ref[pos, cols].astype(BF16))
            st_ref[rows, :] = jnp.exp(lgb * float(CHUNK)) * st + kv


def _ret_bwd_states(proj, cos, sgn_sin, raw_rows, *, batch, n_chunks):
    assert OFF_RK % RET_WIDTH == 0 and OFF_RV % RET_WIDTH == 0
    kblk, vblk = OFF_RK // RET_WIDTH, OFF_RV // RET_WIDTH
    per = BWD_CHUNKS_PER_STEP
    assert n_chunks % per == 0
    n_steps = n_chunks // per
    last = n_steps - 1
    return pl.pallas_call(
        _ret_bwd_state_kernel,
        grid=(batch, n_steps),
        in_specs=[
            pl.BlockSpec((per * CHUNK, RET_WIDTH), lambda b, t: (b * n_steps + last - t, kblk)),
            pl.BlockSpec((per * CHUNK, RET_WIDTH), lambda b, t: (b * n_steps + last - t, vblk)),
            pl.BlockSpec((per * CHUNK, HEAD_DIM), lambda b, t: (last - t, 0)),
            pl.BlockSpec((per * CHUNK, HEAD_DIM), lambda b, t: (last - t, 0)),
            pl.BlockSpec((2 * N_RET_HEADS, HEAD_DIM), lambda b, t: (0, 0)),
        ],
        out_specs=pl.BlockSpec((1, per, RET_WIDTH, HEAD_DIM), lambda b, t: (b, last - t, 0, 0)),
        out_shape=jax.ShapeDtypeStruct((batch, n_chunks, RET_WIDTH, HEAD_DIM), BF16),
        scratch_shapes=[pltpu.VMEM((RET_WIDTH, HEAD_DIM), F32)],
        compiler_params=pltpu.CompilerParams(
            dimension_semantics=("parallel", "arbitrary"),
            vmem_limit_bytes=_vmem_limit(4 * per * CHUNK * RET_WIDTH * 4 + (6 << 20))),
        name="ret_bwd_states",
    )(proj, proj, cos, sgn_sin, raw_rows)


def _sgu(p_ref, lng_ref, lnb_ref, ws_ref, bs_ref, o_ref):
    sqrt_half = np.sqrt(0.5).astype(np.float32)

    def gelu(t):
        return 0.5 * t * (1.0 + lax.erf(t * sqrt_half))

    u = gelu(p_ref[:, OFF_U:OFF_U + SGU_WIDTH])
    v = gelu(p_ref[:, OFF_V:OFF_V + SGU_WIDTH])
    mu = jnp.mean(v, axis=-1, keepdims=True)
    var = jnp.mean(jnp.square(v - mu), axis=-1, keepdims=True)
    vn = (v - mu) * lax.rsqrt(var + NORM_EPS) * lng_ref[...] + lnb_ref[...]
    for g in range(N_SGU_GROUPS):
        cols = slice(g * HEAD_DIM, (g + 1) * HEAD_DIM)
        s = _dot(ws_ref[g], vn[:, cols].astype(BF16)) + bs_ref[g]
        o_ref[:, cols] = (u[:, cols] * s).astype(o_ref.dtype)


N_RET_TABLES = 4


def _ret_table(tab_ref, h, t):
    base = (h * N_RET_TABLES + t) * CHUNK
    return tab_ref.at[base:base + CHUNK, :]


def _fill_ret_tables(raw_ref, tab_ref):
    row = _row_index_f32()
    col = lax.broadcasted_iota(jnp.int32, (CHUNK, CHUNK), 1).astype(F32)
    delta = row - col
    for h in range(N_RET_HEADS):
        lgf = -jnp.exp(raw_ref[h:h + 1, :])
        lgb = -jnp.exp(raw_ref[N_RET_HEADS + h: N_RET_HEADS + h + 1, :])
        _ret_table(tab_ref, h, 0)[...] = jnp.where(
            delta >= 0.0,
            jnp.exp(lgf * jnp.maximum(delta, 0.0)),
            jnp.exp(lgb * jnp.maximum(-delta, 0.0)))
        _ret_table(tab_ref, h, 1)[...] = jnp.exp(lgf * (float(CHUNK - 1) - row))
        _ret_table(tab_ref, h, 2)[...] = jnp.exp(lgf * (row + 1.0))
        _ret_table(tab_ref, h, 3)[...] = jnp.exp(lgb * (float(CHUNK) - row))


def _retention(p_ref, sb_ref, raw_ref, cos, sin, o_ref, sf_ref, tab_ref, first_chunk):
    @pl.when(first_chunk)
    def _():
        sf_ref[...] = jnp.zeros_like(sf_ref)
        _fill_ret_tables(raw_ref, tab_ref)

    scale = HEAD_DIM ** -0.5
    for h in range(N_RET_HEADS):
        rows = slice(h * HEAD_DIM, (h + 1) * HEAD_DIM)
        lgf = -jnp.exp(raw_ref[h:h + 1, :])
        q = _rope(_head(p_ref, OFF_RQ, h), cos, sin)
        k = _rope(_head(p_ref, OFF_RK, h), cos, sin) * scale
        v = _head(p_ref, OFF_RV, h).astype(BF16)
        scores = _dot_nt(q.astype(BF16), k.astype(BF16)) * _ret_table(tab_ref, h, 0)[...]
        inner = _dot(scores.astype(BF16), v)
        sf_prev = sf_ref[rows, :]
        kv_f = _dot_tn((k * _ret_table(tab_ref, h, 1)[...]).astype(BF16), v)
        cross_f = _dot((q * _ret_table(tab_ref, h, 2)[...]).astype(BF16), sf_prev.astype(BF16))
        sf_ref[rows, :] = jnp.exp(lgf * float(CHUNK)) * sf_prev + kv_f
        cross_b = _dot((q * _ret_table(tab_ref, h, 3)[...]).astype(BF16), sb_ref[0, 0, rows, :])
        rf = (inner + cross_f) + cross_b
        rn = rf * lax.rsqrt(jnp.mean(rf * rf, axis=-1, keepdims=True) + NORM_EPS)
        gate = _head(p_ref, OFF_RG, h)
        o_ref[:, SGU_WIDTH + h * HEAD_DIM: SGU_WIDTH + (h + 1) * HEAD_DIM] = (
            gate * jax.nn.sigmoid(gate) * rn).astype(o_ref.dtype)


def _attention(p_ref, kvp_ref, kvn_ref, cos3_ref, sin3_ref, sink_ref, o_ref, chunk, seq):
    cos3 = cos3_ref[...]
    sin3 = sin3_ref[...]
    cos = cos3[CHUNK:2 * CHUNK]
    sin = sin3[CHUNK:2 * CHUNK]
    scale = HEAD_DIM ** -0.5
    kj = lax.broadcasted_iota(jnp.int32, (3 * CHUNK, CHUNK), 0)
    qi = lax.broadcasted_iota(jnp.int32, (3 * CHUNK, CHUNK), 1)
    rel = kj - CHUNK - qi
    kpos = (chunk - 1) * CHUNK + kj
    valid = (jnp.abs(rel) <= WINDOW) & (kpos >= 0) & (kpos < seq)
    valid = jnp.concatenate([valid] * Q_PER_KV, axis=1)
    out_off = SGU_WIDTH + RET_WIDTH
    for kh in range(N_KV_HEADS):
        kcols = slice(kh * HEAD_DIM, (kh + 1) * HEAD_DIM)
        vcols = slice(KV_WIDTH + kh * HEAD_DIM, KV_WIDTH + (kh + 1) * HEAD_DIM)
        k3 = jnp.concatenate([kvp_ref[:, kcols], _head(p_ref, OFF_AK, kh), kvn_ref[:, kcols]], axis=0)
        v3 = jnp.concatenate([kvp_ref[:, vcols], _head(p_ref, OFF_AV, kh), kvn_ref[:, vcols]], axis=0)
        k3 = _rope(k3, cos3, sin3).astype(BF16)
        qs = jnp.concatenate(
            [_rope(_head(p_ref, OFF_AQ, kh * Q_PER_KV + g), cos, sin) * scale for g in range(Q_PER_KV)],
            axis=0)
        sink = jnp.concatenate(
            [sink_ref[kh * Q_PER_KV + g: kh * Q_PER_KV + g + 1, :] for g in range(Q_PER_KV)], axis=1)
        s = _dot_nt(k3, qs.astype(BF16))
        s = jnp.concatenate([
            jnp.where(valid[:CHUNK], s[:CHUNK], NEG_INF),
            s[CHUNK:2 * CHUNK],
            jnp.where(valid[2 * CHUNK:], s[2 * CHUNK:], NEG_INF)], axis=0)
        m = jnp.maximum(jnp.max(s, axis=0, keepdims=True), sink)
        p = jnp.exp(s - m)
        inv = 1.0 / (jnp.sum(p, axis=0, keepdims=True) + jnp.exp(sink - m))
        o_t = _dot_tn(v3.astype(BF16), (p * inv).astype(BF16)).astype(o_ref.dtype)
        for g in range(Q_PER_KV):
            hq = kh * Q_PER_KV + g
            o_ref[:, out_off + hq * HEAD_DIM: out_off + (hq + 1) * HEAD_DIM] = (
                o_t[:, g * CHUNK:(g + 1) * CHUNK].T)


def _mixer_kernel(p_ref, kvp_ref, kvn_ref, cos3_ref, sin3_ref, sb_ref, lng_ref, lnb_ref,
                  ws_ref, bs_ref, raw_ref, sink_ref, cast_src_ref, o_ref, cast_dst_ref,
                  sf_ref, tab_ref, *, seq):
    chunk = pl.program_id(1)
    cast_dst_ref[...] = cast_src_ref[...].astype(cast_dst_ref.dtype)
    _sgu(p_ref, lng_ref, lnb_ref, ws_ref, bs_ref, o_ref)
    cos = cos3_ref[CHUNK:2 * CHUNK, :]
    sin = sin3_ref[CHUNK:2 * CHUNK, :]
    _retention(p_ref, sb_ref, raw_ref, cos, sin, o_ref, sf_ref, tab_ref, chunk == 0)
    _attention(p_ref, kvp_ref, kvn_ref, cos3_ref, sin3_ref, sink_ref, o_ref, chunk, seq)


def _mixers(proj, sb, cos3, sin3, lng, lnb, ws, bias, raw_rows, sink_rows, side_cast,
            *, batch, n_chunks):
    m = proj.shape[0]
    kv_blk = OFF_AK // (2 * KV_WIDTH)
    assert kv_blk * 2 * KV_WIDTH == OFF_AK
    last = n_chunks - 1
    const2 = lambda b, n: (0, 0)
    side_in, side_out, side_shape, side_vmem = _side_cast_specs(
        (side_cast,), batch * n_chunks, lambda b, n: b * n_chunks + n)
    est = (2 * CHUNK * IN_WIDTH * 4 + 4 * CHUNK * 2 * KV_WIDTH * 4 + 2 * CHUNK * MIX_WIDTH * 2
           + (12 << 20) + side_vmem)
    return pl.pallas_call(
        functools.partial(_mixer_kernel, seq=n_chunks * CHUNK),
        grid=(batch, n_chunks),
        in_specs=[
            pl.BlockSpec((CHUNK, IN_WIDTH), lambda b, n: (b * n_chunks + n, 0)),
            pl.BlockSpec((CHUNK, 2 * KV_WIDTH),
                         lambda b, n: (b * n_chunks + jnp.maximum(n - 1, 0), kv_blk)),
            pl.BlockSpec((CHUNK, 2 * KV_WIDTH),
                         lambda b, n: (b * n_chunks + jnp.minimum(n + 1, last), kv_blk)),
            pl.BlockSpec((3 * CHUNK, HEAD_DIM), lambda b, n: (n, 0)),
            pl.BlockSpec((3 * CHUNK, HEAD_DIM), lambda b, n: (n, 0)),
            pl.BlockSpec((1, 1, RET_WIDTH, HEAD_DIM), lambda b, n: (b, n, 0, 0)),
            pl.BlockSpec((1, SGU_WIDTH), const2),
            pl.BlockSpec((1, SGU_WIDTH), const2),
            pl.BlockSpec((N_SGU_GROUPS, CHUNK, CHUNK), lambda b, n: (0, 0, 0)),
            pl.BlockSpec((N_SGU_GROUPS, CHUNK, HEAD_DIM), lambda b, n: (0, 0, 0)),
            pl.BlockSpec((2 * N_RET_HEADS, HEAD_DIM), const2),
            pl.BlockSpec((N_Q_HEADS, HEAD_DIM), const2),
            *side_in,
        ],
        out_specs=[pl.BlockSpec((CHUNK, MIX_WIDTH), lambda b, n: (b * n_chunks + n, 0)), *side_out],
        out_shape=[jax.ShapeDtypeStruct((m, MIX_WIDTH), BF16), *side_shape],
        scratch_shapes=[pltpu.VMEM((RET_WIDTH, HEAD_DIM), F32),
                        pltpu.VMEM((N_RET_HEADS * N_RET_TABLES * CHUNK, HEAD_DIM), F32)],
        compiler_params=pltpu.CompilerParams(
            dimension_semantics=("arbitrary", "arbitrary"),
            vmem_limit_bytes=_vmem_limit(est)),
        name="mixers",
    )(proj, proj, proj, cos3, sin3, sb, lng, lnb, ws, bias, raw_rows, sink_rows, side_cast[0])


def _rope_tables(seq):
    pos = jnp.arange(seq, dtype=F32)
    inv = ROPE_THETA ** (-jnp.arange(0, HEAD_DIM, 2, dtype=F32) / HEAD_DIM)
    ang = pos[:, None] * inv[None, :]
    ang = jnp.concatenate([ang, ang], axis=-1)
    sgn = jnp.where(jnp.arange(HEAD_DIM) < HEAD_DIM // 2, -1.0, 1.0).astype(F32)
    return jnp.cos(ang), jnp.sin(ang) * sgn[None, :]


def _three_chunk_view(t, n_chunks):
    seq = t.shape[0]
    pad = jnp.zeros((CHUNK, HEAD_DIM), t.dtype)
    tp = jnp.concatenate([pad, t, pad], axis=0)
    parts = [tp[s * CHUNK: s * CHUNK + seq].reshape(n_chunks, CHUNK, HEAD_DIM) for s in range(3)]
    return jnp.concatenate(parts, axis=1).reshape(n_chunks * 3 * CHUNK, HEAD_DIM)


def kernel(x, ln_mix_g, w_in, sgu_ln_g, sgu_ln_b, sgu_w, sgu_b, ret_log_decay, attn_sink,
           w_out, ln_mlp_g, w_up, w_down, final_norm_g):
    batch, seq, d_model = x.shape
    depth = w_in.shape[0]
    assert w_in.shape[2] == IN_WIDTH and w_out.shape[1] == MIX_WIDTH and seq % CHUNK == 0
    n_chunks = seq // CHUNK
    m = batch * seq

    cos, sgn_sin = _rope_tables(seq)
    cos3 = _three_chunk_view(cos, n_chunks)
    sin3 = _three_chunk_view(sgn_sin, n_chunks)

    next_gains = [ln_mix_g[l] for l in range(1, depth)] + [final_norm_g]

    w_in_b = _cast_layer(w_in, 0)
    x2 = x.reshape(m, d_model)
    xg, r = _prenorm(x2, ln_mix_g[0])
    for l in range(depth):
        raw_rows = jnp.broadcast_to(
            ret_log_decay[l].astype(F32).reshape(2 * N_RET_HEADS, 1), (2 * N_RET_HEADS, HEAD_DIM))
        sink_rows = jnp.broadcast_to(
            attn_sink[l].astype(F32).reshape(N_Q_HEADS, 1), (N_Q_HEADS, HEAD_DIM))
        sgu_bias = jnp.broadcast_to(
            sgu_b[l].astype(F32)[:, :, None], (N_SGU_GROUPS, CHUNK, HEAD_DIM))
        proj, w_out_b = _scaled_matmul(
            xg, r, w_in_b, relu2=False, out_dtype=F32, side_casts=((w_out, l),))
        sb = _ret_bwd_states(proj, cos, sgn_sin, raw_rows, batch=batch, n_chunks=n_chunks)
        mix, w_up_b = _mixers(proj, sb, cos3, sin3,
                              sgu_ln_g[l].reshape(1, SGU_WIDTH), sgu_ln_b[l].reshape(1, SGU_WIDTH),
                              sgu_w[l].astype(BF16), sgu_bias, raw_rows, sink_rows, (w_up, l),
                              batch=batch, n_chunks=n_chunks)
        x2, r, xg = _matmul_res(mix, w_out_b, x2, ln_mlp_g[l], tm=1024, tn=512, tk=MIX_WIDTH)
        last = l + 1 == depth
        hid, w_down_b, *maybe_w_in = _scaled_matmul(
            xg, r, w_up_b, relu2=True, out_dtype=BF16,
            side_casts=((w_down, l),) if last else ((w_down, l), (w_in, l + 1)))
        w_in_b = None if last else maybe_w_in[0]
        x2, r, *maybe_xg = _matmul_res(hid, w_down_b, x2, next_gains[l], tm=1024, tn=1024, tk=2048,
                                       emit_xg=not last)
        xg = None if last else maybe_xg[0]
    return _final_scale(x2, r, final_norm_g).reshape(batch, seq, d_model)
```

```python
import functools

import numpy as np
import jax
import jax.numpy as jnp
from jax import lax
from jax.experimental import pallas as pl
from jax.experimental.pallas import tpu as pltpu

HEAD_DIM = 128
CHUNK = 128
WINDOW = 128
ROPE_THETA = 10000.0
NORM_EPS = 1e-5
NEG_INF = -1e30
LOG2_E = float(np.log2(np.e))
N_SGU_GROUPS = 8
N_RET_HEADS = 8
N_Q_HEADS = 16
N_KV_HEADS = 4
Q_PER_KV = N_Q_HEADS // N_KV_HEADS

SGU_WIDTH = N_SGU_GROUPS * HEAD_DIM
RET_WIDTH = N_RET_HEADS * HEAD_DIM
ATT_WIDTH = N_Q_HEADS * HEAD_DIM
KV_WIDTH = N_KV_HEADS * HEAD_DIM
OFF_U = 0
OFF_V = OFF_U + SGU_WIDTH
OFF_RQ = OFF_V + SGU_WIDTH
OFF_RK = OFF_RQ + RET_WIDTH
OFF_RV = OFF_RK + RET_WIDTH
OFF_RG = OFF_RV + RET_WIDTH
OFF_AQ = OFF_RG + RET_WIDTH
OFF_AK = OFF_AQ + ATT_WIDTH
OFF_AV = OFF_AK + KV_WIDTH
IN_WIDTH = OFF_AV + KV_WIDTH
MIX_WIDTH = SGU_WIDTH + RET_WIDTH + ATT_WIDTH

V7X_VMEM_BYTES = 64 * 1024 * 1024
V7X_VMEM_BUDGET = 56 * 1024 * 1024
CAST_SLAB_BYTES = 4 * 1024 * 1024
BF16_SUBLANES = 16

F32 = jnp.float32
BF16 = jnp.bfloat16


def _vmem_limit(est_bytes):
    assert est_bytes <= V7X_VMEM_BUDGET, est_bytes
    return min(max(int(est_bytes * 1.25), 16 << 20), V7X_VMEM_BUDGET)


def _dot(a, b):
    return jnp.dot(a, b, preferred_element_type=F32)


def _dot_nt(a, b):
    return lax.dot_general(a, b, (((1,), (1,)), ((), ())), preferred_element_type=F32)


def _dot_tn(a, b):
    return lax.dot_general(a, b, (((0,), (0,)), ((), ())), preferred_element_type=F32)


def _cast_kernel(w_ref, o_ref):
    o_ref[...] = w_ref[...].astype(o_ref.dtype)


def _cast_layer(w, layer):
    _, k, n = w.shape
    tr = 1
    while tr * 2 * n * 4 <= CAST_SLAB_BYTES and k % (tr * 2) == 0:
        tr *= 2
    return pl.pallas_call(
        _cast_kernel,
        grid=(k // tr,),
        in_specs=[pl.BlockSpec((None, tr, n), lambda i: (layer, i, 0))],
        out_specs=pl.BlockSpec((tr, n), lambda i: (i, 0)),
        out_shape=jax.ShapeDtypeStruct((k, n), BF16),
        compiler_params=pltpu.CompilerParams(
            dimension_semantics=("parallel",),
            vmem_limit_bytes=_vmem_limit(2 * tr * n * 6 + tr * n * 4)),
        name="cast_bf16",
    )(w)


def _side_cast_specs(side_casts, steps, step_index):
    in_specs, out_specs, out_shape, vmem = [], [], [], 0
    for w, layer in side_casts:
        _, rows, cols = w.shape
        n_slabs = 1
        while n_slabs * 2 <= steps and rows % (n_slabs * 2) == 0 and (rows // (n_slabs * 2)) % BF16_SUBLANES == 0:
            n_slabs *= 2
        slab = rows // n_slabs

        def slab_index(*ids, _last=n_slabs - 1):
            return jnp.minimum(step_index(*ids), _last)

        in_specs.append(pl.BlockSpec(
            (None, slab, cols), lambda *ids, _l=layer, _s=slab_index: (_l, _s(*ids), 0)))
        out_specs.append(pl.BlockSpec((slab, cols), lambda *ids, _s=slab_index: (_s(*ids), 0)))
        out_shape.append(jax.ShapeDtypeStruct((rows, cols), BF16))
        vmem += 2 * slab * cols * 6 + slab * cols * 4
    return in_specs, out_specs, out_shape, vmem


def _row_rsqrt(ss, width):
    return lax.rsqrt(ss / float(width) + NORM_EPS)


def _prenorm_kernel(x_ref, g_ref, xg_ref, ss_ref):
    x = x_ref[...]
    xg_ref[...] = (x * g_ref[...]).astype(BF16)
    ss_ref[...] = jnp.sum(x * x, axis=-1, keepdims=True)


def _prenorm(x, g, *, tm=256):
    m, k = x.shape
    return pl.pallas_call(
        _prenorm_kernel,
        grid=(m // tm,),
        in_specs=[pl.BlockSpec((tm, k), lambda i: (i, 0)),
                  pl.BlockSpec((1, k), lambda i: (0, 0))],
        out_specs=[pl.BlockSpec((tm, k), lambda i: (i, 0)),
                   pl.BlockSpec((tm, 1), lambda i: (i, 0))],
        out_shape=[jax.ShapeDtypeStruct((m, k), BF16), jax.ShapeDtypeStruct((m, 1), F32)],
        compiler_params=pltpu.CompilerParams(
            dimension_semantics=("parallel",),
            vmem_limit_bytes=_vmem_limit(2 * tm * k * 6 + 2 * tm * k * 4)),
        name="prenorm",
    )(x, g.reshape(1, k))


def _scaled_matmul_kernel(a_ref, ss_ref, w_ref, *rest, relu2, n_side):
    side_in, o_ref, side_out = rest[:n_side], rest[n_side], rest[n_side + 1:]
    acc = _dot(a_ref[...], w_ref[...]) * _row_rsqrt(ss_ref[...], a_ref.shape[1])
    if relu2:
        acc = jnp.square(jnp.maximum(acc, 0.0))
    o_ref[...] = acc.astype(o_ref.dtype)
    for src_ref, dst_ref in zip(side_in, side_out):
        dst_ref[...] = src_ref[...].astype(dst_ref.dtype)


def _scaled_matmul(a, ss, w, *, relu2, out_dtype, side_casts=(), tm=1024, tn=1024):
    m, k = a.shape
    n = w.shape[1]
    assert m % tm == 0 and n % tn == 0
    nj = n // tn
    out_bytes = jnp.dtype(out_dtype).itemsize
    side_in, side_out, side_shape, side_vmem = _side_cast_specs(
        side_casts, (m // tm) * nj, lambda i, j: i * nj + j)
    est = 2 * tm * k * 2 + 2 * k * tn * 2 + 2 * tm * tn * out_bytes + 2 * tm * tn * 4 + side_vmem
    return pl.pallas_call(
        functools.partial(_scaled_matmul_kernel, relu2=relu2, n_side=len(side_casts)),
        grid=(m // tm, nj),
        in_specs=[
            pl.BlockSpec((tm, k), lambda i, j: (i, 0)),
            pl.BlockSpec((tm, 1), lambda i, j: (i, 0)),
            pl.BlockSpec((k, tn), lambda i, j: (0, j)),
            *side_in,
        ],
        out_specs=[pl.BlockSpec((tm, tn), lambda i, j: (i, j)), *side_out],
        out_shape=[jax.ShapeDtypeStruct((m, n), out_dtype), *side_shape],
        compiler_params=pltpu.CompilerParams(
            dimension_semantics=("arbitrary", "arbitrary"),
            vmem_limit_bytes=_vmem_limit(est)),
        name="scaled_matmul_relu2" if relu2 else "scaled_matmul",
    )(a, ss, w, *[src for src, _ in side_casts])


def _matmul_res_kernel(a_ref, w_ref, res_ref, g_ref, o_ref, ss_ref, *rest, nk):
    maybe_xg_ref, acc_ref = (rest[0], rest[1]) if len(rest) == 2 else (None, rest[0])
    j = pl.program_id(1)
    kk = pl.program_id(2)

    @pl.when(jnp.logical_and(j == 0, kk == 0))
    def _():
        ss_ref[...] = jnp.zeros_like(ss_ref)

    def finish(x_new):
        o_ref[...] = x_new
        if maybe_xg_ref is not None:
            maybe_xg_ref[...] = (x_new * g_ref[...]).astype(BF16)
        ss_ref[...] += jnp.sum(x_new * x_new, axis=-1, keepdims=True)

    if nk == 1:
        finish(res_ref[...] + _dot(a_ref[...], w_ref[...]))
        return

    @pl.when(kk == 0)
    def _():
        acc_ref[...] = res_ref[...] + _dot(a_ref[...], w_ref[...])

    @pl.when(jnp.logical_and(kk > 0, kk < nk - 1))
    def _():
        acc_ref[...] += _dot(a_ref[...], w_ref[...])

    @pl.when(kk == nk - 1)
    def _():
        finish(acc_ref[...] + _dot(a_ref[...], w_ref[...]))


def _matmul_res(a, w, res, g_next, *, tm, tn, tk, emit_xg=True):
    m, k = a.shape
    n = w.shape[1]
    assert m % tm == 0 and n % tn == 0 and k % tk == 0
    nk, nj = k // tk, n // tn
    acc_shape = (tm, tn) if nk > 1 else (8, 128)
    est = (2 * tm * tk * 2 + 2 * tk * tn * 2 + 2 * tm * tn * (4 + 4 + 2 * emit_xg)
           + acc_shape[0] * acc_shape[1] * 4 + 2 * tm * tn * 4)
    tile = pl.BlockSpec((tm, tn), lambda i, j, kk: (i, j))
    out_specs = [tile, pl.BlockSpec((tm, 1), lambda i, j, kk: (i, 0))]
    out_shape = [jax.ShapeDtypeStruct((m, n), F32), jax.ShapeDtypeStruct((m, 1), F32)]
    if emit_xg:
        out_specs.append(tile)
        out_shape.append(jax.ShapeDtypeStruct((m, n), BF16))
    return pl.pallas_call(
        functools.partial(_matmul_res_kernel, nk=nk),
        grid=(m // tm, nj, nk),
        in_specs=[
            pl.BlockSpec((tm, tk), lambda i, j, kk: (i, kk)),
            pl.BlockSpec((tk, tn), lambda i, j, kk: (kk, j)),
            tile,
            pl.BlockSpec((1, tn), lambda i, j, kk: (0, j)),
        ],
        out_specs=out_specs,
        out_shape=out_shape,
        scratch_shapes=[pltpu.VMEM(acc_shape, F32)],
        compiler_params=pltpu.CompilerParams(
            dimension_semantics=("parallel", "arbitrary", "arbitrary"),
            vmem_limit_bytes=_vmem_limit(est)),
        name="matmul_res_k%d" % nk,
    )(a, w, res, g_next.reshape(1, n))


def _final_scale_kernel(x_ref, ss_ref, g_ref, o_ref):
    o_ref[...] = x_ref[...] * _row_rsqrt(ss_ref[...], x_ref.shape[1]) * g_ref[...]


def _final_scale(x, ss, g, *, tm=512):
    m, k = x.shape
    return pl.pallas_call(
        _final_scale_kernel,
        grid=(m // tm,),
        in_specs=[pl.BlockSpec((tm, k), lambda i: (i, 0)),
                  pl.BlockSpec((tm, 1), lambda i: (i, 0)),
                  pl.BlockSpec((1, k), lambda i: (0, 0))],
        out_specs=pl.BlockSpec((tm, k), lambda i: (i, 0)),
        out_shape=jax.ShapeDtypeStruct((m, k), F32),
        compiler_params=pltpu.CompilerParams(
            dimension_semantics=("parallel",),
            vmem_limit_bytes=_vmem_limit(5 * tm * k * 4)),
        name="final_scale",
    )(x, ss, g.reshape(1, k))


def _rope(t, cos, sgn_sin):
    return t * cos + pltpu.roll(t, HEAD_DIM // 2, 1) * sgn_sin


def _row_index_f32():
    return lax.broadcasted_iota(jnp.int32, (CHUNK, HEAD_DIM), 0).astype(F32)


def _head(ref, off, h):
    return ref[:, off + h * HEAD_DIM: off + (h + 1) * HEAD_DIM]


BWD_CHUNKS_PER_STEP = 4


def _ret_bwd_state_kernel(k_ref, v_ref, cos_ref, sin_ref, raw_ref, sb_ref, st_ref):
    @pl.when(pl.program_id(1) == 0)
    def _():
        st_ref[...] = jnp.zeros_like(st_ref)

    row = _row_index_f32()
    scale = HEAD_DIM ** -0.5
    for c in reversed(range(BWD_CHUNKS_PER_STEP)):
        pos = slice(c * CHUNK, (c + 1) * CHUNK)
        cos = cos_ref[pos, :]
        sin = sin_ref[pos, :]
        for h in range(N_RET_HEADS):
            rows = slice(h * HEAD_DIM, (h + 1) * HEAD_DIM)
            cols = slice(h * HEAD_DIM, (h + 1) * HEAD_DIM)
            st = st_ref[rows, :]
            sb_ref[0, c, rows, :] = st.astype(BF16)
            lgb = -jnp.exp(raw_ref[N_RET_HEADS + h: N_RET_HEADS + h + 1, :])
            kh = _rope(k_ref[pos, cols], cos, sin) * scale
            kw = (kh * jnp.exp(lgb * row)).astype(BF16)
            kv = _dot_tn(kw, v_ref[pos, cols].astype(BF16))
            st_ref[rows, :] = jnp.exp(lgb * float(CHUNK)) * st + kv


def _ret_bwd_states(proj, cos, sgn_sin, raw_rows, *, batch, n_chunks):
    assert OFF_RK % RET_WIDTH == 0 and OFF_RV % RET_WIDTH == 0
    kblk, vblk = OFF_RK // RET_WIDTH, OFF_RV // RET_WIDTH
    per = BWD_CHUNKS_PER_STEP
    assert n_chunks % per == 0
    n_steps = n_chunks // per
    last = n_steps - 1
    return pl.pallas_call(
        _ret_bwd_state_kernel,
        grid=(batch, n_steps),
        in_specs=[
            pl.BlockSpec((per * CHUNK, RET_WIDTH), lambda b, t: (b * n_steps + last - t, kblk)),
            pl.BlockSpec((per * CHUNK, RET_WIDTH), lambda b, t: (b * n_steps + last - t, vblk)),
            pl.BlockSpec((per * CHUNK, HEAD_DIM), lambda b, t: (last - t, 0)),
            pl.BlockSpec((per * CHUNK, HEAD_DIM), lambda b, t: (last - t, 0)),
            pl.BlockSpec((2 * N_RET_HEADS, HEAD_DIM), lambda b, t: (0, 0)),
        ],
        out_specs=pl.BlockSpec((1, per, RET_WIDTH, HEAD_DIM), lambda b, t: (b, last - t, 0, 0)),
        out_shape=jax.ShapeDtypeStruct((batch, n_chunks, RET_WIDTH, HEAD_DIM), BF16),
        scratch_shapes=[pltpu.VMEM((RET_WIDTH, HEAD_DIM), F32)],
        compiler_params=pltpu.CompilerParams(
            dimension_semantics=("parallel", "arbitrary"),
            vmem_limit_bytes=_vmem_limit(4 * per * CHUNK * RET_WIDTH * 4 + (6 << 20))),
        name="ret_bwd_states",
    )(proj, proj, cos, sgn_sin, raw_rows)


def _sgu(p_ref, lng_ref, lnb_ref, ws_ref, bs_ref, o_ref):
    sqrt_half = np.sqrt(0.5).astype(np.float32)

    def gelu(t):
        return 0.5 * t * (1.0 + lax.erf(t * sqrt_half))

    u = gelu(p_ref[:, OFF_U:OFF_U + SGU_WIDTH])
    v = gelu(p_ref[:, OFF_V:OFF_V + SGU_WIDTH])
    mu = jnp.mean(v, axis=-1, keepdims=True)
    var = jnp.mean(jnp.square(v - mu), axis=-1, keepdims=True)
    vn = (v - mu) * lax.rsqrt(var + NORM_EPS) * lng_ref[...] + lnb_ref[...]
    for g in range(N_SGU_GROUPS):
        cols = slice(g * HEAD_DIM, (g + 1) * HEAD_DIM)
        s = _dot(ws_ref[g], vn[:, cols].astype(BF16)) + bs_ref[g]
        o_ref[:, cols] = (u[:, cols] * s).astype(o_ref.dtype)


N_RET_TABLES = 4


def _ret_table(tab_ref, h, t):
    base = (h * N_RET_TABLES + t) * CHUNK
    return tab_ref.at[base:base + CHUNK, :]


def _fill_ret_tables(raw_ref, tab_ref):
    row = _row_index_f32()
    col = lax.broadcasted_iota(jnp.int32, (CHUNK, CHUNK), 1).astype(F32)
    delta = row - col
    for h in range(N_RET_HEADS):
        lgf = -jnp.exp(raw_ref[h:h + 1, :])
        lgb = -jnp.exp(raw_ref[N_RET_HEADS + h: N_RET_HEADS + h + 1, :])
        _ret_table(tab_ref, h, 0)[...] = jnp.where(
            delta >= 0.0,
            jnp.exp(lgf * jnp.maximum(delta, 0.0)),
            jnp.exp(lgb * jnp.maximum(-delta, 0.0)))
        _ret_table(tab_ref, h, 1)[...] = jnp.exp(lgf * (float(CHUNK - 1) - row))
        _ret_table(tab_ref, h, 2)[...] = jnp.exp(lgf * (row + 1.0))
        _ret_table(tab_ref, h, 3)[...] = jnp.exp(lgb * (float(CHUNK) - row))


def _retention(p_ref, sb_ref, raw_ref, cos, sin, o_ref, sf_ref, tab_ref, first_chunk):
    @pl.when(first_chunk)
    def _():
        sf_ref[...] = jnp.zeros_like(sf_ref)
        _fill_ret_tables(raw_ref, tab_ref)

    scale = HEAD_DIM ** -0.5
    for h in range(N_RET_HEADS):
        rows = slice(h * HEAD_DIM, (h + 1) * HEAD_DIM)
        lgf = -jnp.exp(raw_ref[h:h + 1, :])
        q = _rope(_head(p_ref, OFF_RQ, h), cos, sin)
        k = _rope(_head(p_ref, OFF_RK, h), cos, sin) * scale
        v = _head(p_ref, OFF_RV, h).astype(BF16)
        scores = _dot_nt(q.astype(BF16), k.astype(BF16)) * _ret_table(tab_ref, h, 0)[...]
        inner = _dot(scores.astype(BF16), v)
        sf_prev = sf_ref[rows, :]
        kv_f = _dot_tn((k * _ret_table(tab_ref, h, 1)[...]).astype(BF16), v)
        cross_f = _dot((q * _ret_table(tab_ref, h, 2)[...]).astype(BF16), sf_prev.astype(BF16))
        sf_ref[rows, :] = jnp.exp(lgf * float(CHUNK)) * sf_prev + kv_f
        cross_b = _dot((q * _ret_table(tab_ref, h, 3)[...]).astype(BF16), sb_ref[0, 0, rows, :])
        rf = (inner + cross_f) + cross_b
        rn = rf * lax.rsqrt(jnp.mean(rf * rf, axis=-1, keepdims=True) + NORM_EPS)
        gate = _head(p_ref, OFF_RG, h)
        o_ref[:, SGU_WIDTH + h * HEAD_DIM: SGU_WIDTH + (h + 1) * HEAD_DIM] = (
            gate * jax.nn.sigmoid(gate) * rn).astype(o_ref.dtype)


def _attention(p_ref, kvp_ref, kvn_ref, cos3_ref, sin3_ref, sink_ref, o_ref, chunk, seq):
    cos3 = cos3_ref[...]
    sin3 = sin3_ref[...]
    q_scale = HEAD_DIM ** -0.5 * LOG2_E
    cos_q = cos3[CHUNK:2 * CHUNK] * q_scale
    sin_q = sin3[CHUNK:2 * CHUNK] * q_scale
    kj = lax.broadcasted_iota(jnp.int32, (3 * CHUNK, CHUNK), 0)
    qi = lax.broadcasted_iota(jnp.int32, (3 * CHUNK, CHUNK), 1)
    rel = kj - CHUNK - qi
    kpos = (chunk - 1) * CHUNK + kj
    valid = (jnp.abs(rel) <= WINDOW) & (kpos >= 0) & (kpos < seq)
    valid = jnp.concatenate([valid] * Q_PER_KV, axis=1)
    out_off = SGU_WIDTH + RET_WIDTH
    for kh in range(N_KV_HEADS):
        kcols = slice(kh * HEAD_DIM, (kh + 1) * HEAD_DIM)
        vcols = slice(KV_WIDTH + kh * HEAD_DIM, KV_WIDTH + (kh + 1) * HEAD_DIM)
        k3 = jnp.concatenate([kvp_ref[:, kcols], _head(p_ref, OFF_AK, kh), kvn_ref[:, kcols]], axis=0)
        v3 = jnp.concatenate([kvp_ref[:, vcols], _head(p_ref, OFF_AV, kh), kvn_ref[:, vcols]], axis=0)
        k3 = _rope(k3, cos3, sin3).astype(BF16)
        qs = jnp.concatenate(
            [_rope(_head(p_ref, OFF_AQ, kh * Q_PER_KV + g), cos_q, sin_q) for g in range(Q_PER_KV)],
            axis=0)
        sink = LOG2_E * jnp.concatenate(
            [sink_ref[kh * Q_PER_KV + g: kh * Q_PER_KV + g + 1, :] for g in range(Q_PER_KV)], axis=1)
        s = _dot_nt(k3, qs.astype(BF16))
        s = jnp.concatenate([
            jnp.where(valid[:CHUNK], s[:CHUNK], NEG_INF),
            s[CHUNK:2 * CHUNK],
            jnp.where(valid[2 * CHUNK:], s[2 * CHUNK:], NEG_INF)], axis=0)
        m = jnp.maximum(jnp.max(s, axis=0, keepdims=True), sink)
        p = jnp.exp2(s - m)
        inv = 1.0 / (jnp.sum(p, axis=0, keepdims=True) + jnp.exp2(sink - m))
        o_t = _dot_tn(v3.astype(BF16), (p * inv).astype(BF16)).astype(o_ref.dtype)
        for g in range(Q_PER_KV):
            hq = kh * Q_PER_KV + g
            o_ref[:, out_off + hq * HEAD_DIM: out_off + (hq + 1) * HEAD_DIM] = (
                o_t[:, g * CHUNK:(g + 1) * CHUNK].T)


def _mixer_kernel(p_ref, kvp_ref, kvn_ref, cos3_ref, sin3_ref, sb_ref, lng_ref, lnb_ref,
                  ws_ref, bs_ref, raw_ref, sink_ref, cast_src_ref, o_ref, cast_dst_ref,
                  sf_ref, tab_ref, *, seq):
    chunk = pl.program_id(1)
    cast_dst_ref[...] = cast_src_ref[...].astype(cast_dst_ref.dtype)
    _sgu(p_ref, lng_ref, lnb_ref, ws_ref, bs_ref, o_ref)
    cos = cos3_ref[CHUNK:2 * CHUNK, :]
    sin = sin3_ref[CHUNK:2 * CHUNK, :]
    _retention(p_ref, sb_ref, raw_ref, cos, sin, o_ref, sf_ref, tab_ref, chunk == 0)
    _attention(p_ref, kvp_ref, kvn_ref, cos3_ref, sin3_ref, sink_ref, o_ref, chunk, seq)


def _mixers(proj, sb, cos3, sin3, lng, lnb, ws, bias, raw_rows, sink_rows, side_cast,
            *, batch, n_chunks):
    m = proj.shape[0]
    kv_blk = OFF_AK // (2 * KV_WIDTH)
    assert kv_blk * 2 * KV_WIDTH == OFF_AK
    last = n_chunks - 1
    const2 = lambda b, n: (0, 0)
    side_in, side_out, side_shape, side_vmem = _side_cast_specs(
        (side_cast,), batch * n_chunks, lambda b, n: b * n_chunks + n)
    est = (2 * CHUNK * IN_WIDTH * 4 + 4 * CHUNK * 2 * KV_WIDTH * 4 + 2 * CHUNK * MIX_WIDTH * 2
           + (12 << 20) + side_vmem)
    return pl.pallas_call(
        functools.partial(_mixer_kernel, seq=n_chunks * CHUNK),
        grid=(batch, n_chunks),
        in_specs=[
            pl.BlockSpec((CHUNK, IN_WIDTH), lambda b, n: (b * n_chunks + n, 0)),
            pl.BlockSpec((CHUNK, 2 * KV_WIDTH),
                         lambda b, n: (b * n_chunks + jnp.maximum(n - 1, 0), kv_blk)),
            pl.BlockSpec((CHUNK, 2 * KV_WIDTH),
                         lambda b, n: (b * n_chunks + jnp.minimum(n + 1, last), kv_blk)),
            pl.BlockSpec((3 * CHUNK, HEAD_DIM), lambda b, n: (n, 0)),
            pl.BlockSpec((3 * CHUNK, HEAD_DIM), lambda b, n: (n, 0)),
            pl.BlockSpec((1, 1, RET_WIDTH, HEAD_DIM), lambda b, n: (b, n, 0, 0)),
            pl.BlockSpec((1, SGU_WIDTH), const2),
            pl.BlockSpec((1, SGU_WIDTH), const2),
            pl.BlockSpec((N_SGU_GROUPS, CHUNK, CHUNK), lambda b, n: (0, 0, 0)),
            pl.BlockSpec((N_SGU_GROUPS, CHUNK, HEAD_DIM), lambda b, n: (0, 0, 0)),
            pl.BlockSpec((2 * N_RET_HEADS, HEAD_DIM), const2),
            pl.BlockSpec((N_Q_HEADS, HEAD_DIM), const2),
            *side_in,
        ],
        out_specs=[pl.BlockSpec((CHUNK, MIX_WIDTH), lambda b, n: (b * n_chunks + n, 0)), *side_out],
        out_shape=[jax.ShapeDtypeStruct((m, MIX_WIDTH), BF16), *side_shape],
        scratch_shapes=[pltpu.VMEM((RET_WIDTH, HEAD_DIM), F32),
                        pltpu.VMEM((N_RET_HEADS * N_RET_TABLES * CHUNK, HEAD_DIM), F32)],
        compiler_params=pltpu.CompilerParams(
            dimension_semantics=("arbitrary", "arbitrary"),
            vmem_limit_bytes=_vmem_limit(est)),
        name="mixers",
    )(proj, proj, proj, cos3, sin3, sb, lng, lnb, ws, bias, raw_rows, sink_rows, side_cast[0])


def _rope_tables(seq):
    pos = jnp.arange(seq, dtype=F32)
    inv = ROPE_THETA ** (-jnp.arange(0, HEAD_DIM, 2, dtype=F32) / HEAD_DIM)
    ang = pos[:, None] * inv[None, :]
    ang = jnp.concatenate([ang, ang], axis=-1)
    sgn = jnp.where(jnp.arange(HEAD_DIM) < HEAD_DIM // 2, -1.0, 1.0).astype(F32)
    return jnp.cos(ang), jnp.sin(ang) * sgn[None, :]


def _three_chunk_view(t, n_chunks):
    seq = t.shape[0]
    pad = jnp.zeros((CHUNK, HEAD_DIM), t.dtype)
    tp = jnp.concatenate([pad, t, pad], axis=0)
    parts = [tp[s * CHUNK: s * CHUNK + seq].reshape(n_chunks, CHUNK, HEAD_DIM) for s in range(3)]
    return jnp.concatenate(parts, axis=1).reshape(n_chunks * 3 * CHUNK, HEAD_DIM)


def kernel(x, ln_mix_g, w_in, sgu_ln_g, sgu_ln_b, sgu_w, sgu_b, ret_log_decay, attn_sink,
           w_out, ln_mlp_g, w_up, w_down, final_norm_g):
    batch, seq, d_model = x.shape
    depth = w_in.shape[0]
    assert w_in.shape[2] == IN_WIDTH and w_out.shape[1] == MIX_WIDTH and seq % CHUNK == 0
    n_chunks = seq // CHUNK
    m = batch * seq

    cos, sgn_sin = _rope_tables(seq)
    cos3 = _three_chunk_view(cos, n_chunks)
    sin3 = _three_chunk_view(sgn_sin, n_chunks)

    next_gains = [ln_mix_g[l] for l in range(1, depth)] + [final_norm_g]

    w_in_b = _cast_layer(w_in, 0)
    x2 = x.reshape(m, d_model)
    xg, ss = _prenorm(x2, ln_mix_g[0])
    for l in range(depth):
        raw_rows = jnp.broadcast_to(
            ret_log_decay[l].astype(F32).reshape(2 * N_RET_HEADS, 1), (2 * N_RET_HEADS, HEAD_DIM))
        sink_rows = jnp.broadcast_to(
            attn_sink[l].astype(F32).reshape(N_Q_HEADS, 1), (N_Q_HEADS, HEAD_DIM))
        sgu_bias = jnp.broadcast_to(
            sgu_b[l].astype(F32)[:, :, None], (N_SGU_GROUPS, CHUNK, HEAD_DIM))
        proj, w_out_b = _scaled_matmul(
            xg, ss, w_in_b, relu2=False, out_dtype=F32, side_casts=((w_out, l),))
        sb = _ret_bwd_states(proj, cos, sgn_sin, raw_rows, batch=batch, n_chunks=n_chunks)
        mix, w_up_b = _mixers(proj, sb, cos3, sin3,
                              sgu_ln_g[l].reshape(1, SGU_WIDTH), sgu_ln_b[l].reshape(1, SGU_WIDTH),
                              sgu_w[l].astype(BF16), sgu_bias, raw_rows, sink_rows, (w_up, l),
                              batch=batch, n_chunks=n_chunks)
        x2, ss, xg = _matmul_res(mix, w_out_b, x2, ln_mlp_g[l], tm=1024, tn=512, tk=MIX_WIDTH)
        last = l + 1 == depth
        hid, w_down_b, *maybe_w_in = _scaled_matmul(
            xg, ss, w_up_b, relu2=True, out_dtype=BF16,
            side_casts=((w_down, l),) if last else ((w_down, l), (w_in, l + 1)))
        w_in_b = None if last else maybe_w_in[0]
        x2, ss, *maybe_xg = _matmul_res(hid, w_down_b, x2, next_gains[l], tm=1024, tn=1024, tk=2048,
                                       emit_xg=not last)
        xg = None if last else maybe_xg[0]
    return _final_scale(x2, ss, final_norm_g).reshape(batch, seq, d_model)
```

```python
import functools

import numpy as np
import jax
import jax.numpy as jnp
from jax import lax
from jax.experimental import pallas as pl
from jax.experimental.pallas import tpu as pltpu

HEAD_DIM = 128
CHUNK = 128
WINDOW = 128
ROPE_THETA = 10000.0
NORM_EPS = 1e-5
NEG_INF = -1e30
LOG2_E = float(np.log2(np.e))
N_SGU_GROUPS = 8
N_RET_HEADS = 8
N_Q_HEADS = 16
N_KV_HEADS = 4
Q_PER_KV = N_Q_HEADS // N_KV_HEADS

SGU_WIDTH = N_SGU_GROUPS * HEAD_DIM
RET_WIDTH = N_RET_HEADS * HEAD_DIM
ATT_WIDTH = N_Q_HEADS * HEAD_DIM
KV_WIDTH = N_KV_HEADS * HEAD_DIM
OFF_U = 0
OFF_V = OFF_U + SGU_WIDTH
OFF_RQ = OFF_V + SGU_WIDTH
OFF_RK = OFF_RQ + RET_WIDTH
OFF_RV = OFF_RK + RET_WIDTH
OFF_RG = OFF_RV + RET_WIDTH
OFF_AQ = OFF_RG + RET_WIDTH
OFF_AK = OFF_AQ + ATT_WIDTH
OFF_AV = OFF_AK + KV_WIDTH
IN_WIDTH = OFF_AV + KV_WIDTH
MIX_WIDTH = SGU_WIDTH + RET_WIDTH + ATT_WIDTH

V7X_VMEM_BYTES = 64 * 1024 * 1024
V7X_VMEM_BUDGET = 56 * 1024 * 1024
CAST_SLAB_BYTES = 4 * 1024 * 1024
BF16_SUBLANES = 16

F32 = jnp.float32
BF16 = jnp.bfloat16


def _vmem_limit(est_bytes):
    assert est_bytes <= V7X_VMEM_BUDGET, est_bytes
    return min(max(int(est_bytes * 1.25), 16 << 20), V7X_VMEM_BUDGET)


def _dot(a, b):
    return jnp.dot(a, b, preferred_element_type=F32)


def _dot_nt(a, b):
    return lax.dot_general(a, b, (((1,), (1,)), ((), ())), preferred_element_type=F32)


def _dot_tn(a, b):
    return lax.dot_general(a, b, (((0,), (0,)), ((), ())), preferred_element_type=F32)


def _cast_kernel(w_ref, o_ref):
    o_ref[...] = w_ref[...].astype(o_ref.dtype)


def _cast_layer(w, layer):
    _, k, n = w.shape
    tr = 1
    while tr * 2 * n * 4 <= CAST_SLAB_BYTES and k % (tr * 2) == 0:
        tr *= 2
    return pl.pallas_call(
        _cast_kernel,
        grid=(k // tr,),
        in_specs=[pl.BlockSpec((None, tr, n), lambda i: (layer, i, 0))],
        out_specs=pl.BlockSpec((tr, n), lambda i: (i, 0)),
        out_shape=jax.ShapeDtypeStruct((k, n), BF16),
        compiler_params=pltpu.CompilerParams(
            dimension_semantics=("parallel",),
            vmem_limit_bytes=_vmem_limit(2 * tr * n * 6 + tr * n * 4)),
        name="cast_bf16",
    )(w)


def _side_cast_specs(side_casts, steps, step_index):
    in_specs, out_specs, out_shape, vmem = [], [], [], 0
    for w, layer in side_casts:
        _, rows, cols = w.shape
        n_slabs = 1
        while n_slabs * 2 <= steps and rows % (n_slabs * 2) == 0 and (rows // (n_slabs * 2)) % BF16_SUBLANES == 0:
            n_slabs *= 2
        slab = rows // n_slabs

        def slab_index(*ids, _last=n_slabs - 1):
            return jnp.minimum(step_index(*ids), _last)

        in_specs.append(pl.BlockSpec(
            (None, slab, cols), lambda *ids, _l=layer, _s=slab_index: (_l, _s(*ids), 0)))
        out_specs.append(pl.BlockSpec((slab, cols), lambda *ids, _s=slab_index: (_s(*ids), 0)))
        out_shape.append(jax.ShapeDtypeStruct((rows, cols), BF16))
        vmem += 2 * slab * cols * 6 + slab * cols * 4
    return in_specs, out_specs, out_shape, vmem


def _row_rsqrt(ss, width):
    return lax.rsqrt(ss / float(width) + NORM_EPS)


def _prenorm_kernel(x_ref, g_ref, xg_ref, ss_ref):
    x = x_ref[...]
    xg_ref[...] = (x * g_ref[...]).astype(BF16)
    ss_ref[...] = jnp.sum(x * x, axis=-1, keepdims=True)


def _prenorm(x, g, *, tm=256):
    m, k = x.shape
    return pl.pallas_call(
        _prenorm_kernel,
        grid=(m // tm,),
        in_specs=[pl.BlockSpec((tm, k), lambda i: (i, 0)),
                  pl.BlockSpec((1, k), lambda i: (0, 0))],
        out_specs=[pl.BlockSpec((tm, k), lambda i: (i, 0)),
                   pl.BlockSpec((tm, 1), lambda i: (i, 0))],
        out_shape=[jax.ShapeDtypeStruct((m, k), BF16), jax.ShapeDtypeStruct((m, 1), F32)],
        compiler_params=pltpu.CompilerParams(
            dimension_semantics=("parallel",),
            vmem_limit_bytes=_vmem_limit(2 * tm * k * 6 + 2 * tm * k * 4)),
        name="prenorm",
    )(x, g.reshape(1, k))


def _scaled_matmul_kernel(a_ref, ss_ref, w_ref, *rest, relu2, n_side):
    side_in, o_ref, side_out = rest[:n_side], rest[n_side], rest[n_side + 1:]
    acc = _dot(a_ref[...], w_ref[...]) * _row_rsqrt(ss_ref[...], a_ref.shape[1])
    if relu2:
        acc = jnp.square(jnp.maximum(acc, 0.0))
    o_ref[...] = acc.astype(o_ref.dtype)
    for src_ref, dst_ref in zip(side_in, side_out):
        dst_ref[...] = src_ref[...].astype(dst_ref.dtype)


def _scaled_matmul(a, ss, w, *, relu2, out_dtype, side_casts=(), tm=1024, tn=1024):
    m, k = a.shape
    n = w.shape[1]
    assert m % tm == 0 and n % tn == 0
    nj = n // tn
    out_bytes = jnp.dtype(out_dtype).itemsize
    side_in, side_out, side_shape, side_vmem = _side_cast_specs(
        side_casts, (m // tm) * nj, lambda i, j: i * nj + j)
    est = 2 * tm * k * 2 + 2 * k * tn * 2 + 2 * tm * tn * out_bytes + 2 * tm * tn * 4 + side_vmem
    return pl.pallas_call(
        functools.partial(_scaled_matmul_kernel, relu2=relu2, n_side=len(side_casts)),
        grid=(m // tm, nj),
        in_specs=[
            pl.BlockSpec((tm, k), lambda i, j: (i, 0)),
            pl.BlockSpec((tm, 1), lambda i, j: (i, 0)),
            pl.BlockSpec((k, tn), lambda i, j: (0, j)),
            *side_in,
        ],
        out_specs=[pl.BlockSpec((tm, tn), lambda i, j: (i, j)), *side_out],
        out_shape=[jax.ShapeDtypeStruct((m, n), out_dtype), *side_shape],
        compiler_params=pltpu.CompilerParams(
            dimension_semantics=("arbitrary", "arbitrary"),
            vmem_limit_bytes=_vmem_limit(est)),
        name="scaled_matmul_relu2" if relu2 else "scaled_matmul",
    )(a, ss, w, *[src for src, _ in side_casts])


def _matmul_res_kernel(a_ref, w_ref, res_ref, g_ref, o_ref, ss_ref, *rest, nk):
    maybe_xg_ref, acc_ref = (rest[0], rest[1]) if len(rest) == 2 else (None, rest[0])
    j = pl.program_id(1)
    kk = pl.program_id(2)

    @pl.when(jnp.logical_and(j == 0, kk == 0))
    def _():
        ss_ref[...] = jnp.zeros_like(ss_ref)

    def finish(x_new):
        o_ref[...] = x_new
        if maybe_xg_ref is not None:
            maybe_xg_ref[...] = (x_new * g_ref[...]).astype(BF16)
        ss_ref[...] += jnp.sum(x_new * x_new, axis=-1, keepdims=True)

    if nk == 1:
        finish(res_ref[...] + _dot(a_ref[...], w_ref[...]))
        return

    @pl.when(kk == 0)
    def _():
        acc_ref[...] = res_ref[...] + _dot(a_ref[...], w_ref[...])

    @pl.when(jnp.logical_and(kk > 0, kk < nk - 1))
    def _():
        acc_ref[...] += _dot(a_ref[...], w_ref[...])

    @pl.when(kk == nk - 1)
    def _():
        finish(acc_ref[...] + _dot(a_ref[...], w_ref[...]))


def _matmul_res(a, w, res, g_next, *, tm, tn, tk, emit_xg=True):
    m, k = a.shape
    n = w.shape[1]
    assert m % tm == 0 and n % tn == 0 and k % tk == 0
    nk, nj = k // tk, n // tn
    acc_shape = (tm, tn) if nk > 1 else (8, 128)
    est = (2 * tm * tk * 2 + 2 * tk * tn * 2 + 2 * tm * tn * (4 + 4 + 2 * emit_xg)
           + acc_shape[0] * acc_shape[1] * 4 + 2 * tm * tn * 4)
    tile = pl.BlockSpec((tm, tn), lambda i, j, kk: (i, j))
    out_specs = [tile, pl.BlockSpec((tm, 1), lambda i, j, kk: (i, 0))]
    out_shape = [jax.ShapeDtypeStruct((m, n), F32), jax.ShapeDtypeStruct((m, 1), F32)]
    if emit_xg:
        out_specs.append(tile)
        out_shape.append(jax.ShapeDtypeStruct((m, n), BF16))
    return pl.pallas_call(
        functools.partial(_matmul_res_kernel, nk=nk),
        grid=(m // tm, nj, nk),
        in_specs=[
            pl.BlockSpec((tm, tk), lambda i, j, kk: (i, kk)),
            pl.BlockSpec((tk, tn), lambda i, j, kk: (kk, j)),
            tile,
            pl.BlockSpec((1, tn), lambda i, j, kk: (0, j)),
        ],
        out_specs=out_specs,
        out_shape=out_shape,
        scratch_shapes=[pltpu.VMEM(acc_shape, F32)],
        compiler_params=pltpu.CompilerParams(
            dimension_semantics=("parallel", "arbitrary", "arbitrary"),
            vmem_limit_bytes=_vmem_limit(est)),
        name="matmul_res_k%d" % nk,
    )(a, w, res, g_next.reshape(1, n))


def _final_scale_kernel(x_ref, ss_ref, g_ref, o_ref):
    o_ref[...] = x_ref[...] * _row_rsqrt(ss_ref[...], x_ref.shape[1]) * g_ref[...]


def _final_scale(x, ss, g, *, tm=512):
    m, k = x.shape
    return pl.pallas_call(
        _final_scale_kernel,
        grid=(m // tm,),
        in_specs=[pl.BlockSpec((tm, k), lambda i: (i, 0)),
                  pl.BlockSpec((tm, 1), lambda i: (i, 0)),
                  pl.BlockSpec((1, k), lambda i: (0, 0))],
        out_specs=pl.BlockSpec((tm, k), lambda i: (i, 0)),
        out_shape=jax.ShapeDtypeStruct((m, k), F32),
        compiler_params=pltpu.CompilerParams(
            dimension_semantics=("parallel",),
            vmem_limit_bytes=_vmem_limit(5 * tm * k * 4)),
        name="final_scale",
    )(x, ss, g.reshape(1, k))


def _rope(t, cos, sgn_sin):
    return t * cos + pltpu.roll(t, HEAD_DIM // 2, 1) * sgn_sin


def _row_index_f32():
    return lax.broadcasted_iota(jnp.int32, (CHUNK, HEAD_DIM), 0).astype(F32)


def _head(ref, off, h):
    return ref[:, off + h * HEAD_DIM: off + (h + 1) * HEAD_DIM]


BWD_CHUNKS_PER_STEP = 4


def _ret_bwd_state_kernel(k_ref, v_ref, cos_ref, sin_ref, raw_ref, sb_ref, st_ref):
    @pl.when(pl.program_id(1) == 0)
    def _():
        st_ref[...] = jnp.zeros_like(st_ref)

    row = _row_index_f32()
    scale = HEAD_DIM ** -0.5
    for c in reversed(range(BWD_CHUNKS_PER_STEP)):
        pos = slice(c * CHUNK, (c + 1) * CHUNK)
        cos = cos_ref[pos, :]
        sin = sin_ref[pos, :]
        for h in range(N_RET_HEADS):
            rows = slice(h * HEAD_DIM, (h + 1) * HEAD_DIM)
            cols = slice(h * HEAD_DIM, (h + 1) * HEAD_DIM)
            st = st_ref[rows, :]
            sb_ref[0, c, rows, :] = st.astype(BF16)
            lgb = -jnp.exp(raw_ref[N_RET_HEADS + h: N_RET_HEADS + h + 1, :])
            kh = _rope(k_ref[pos, cols], cos, sin) * scale
            kw = (kh * jnp.exp(lgb * row)).astype(BF16)
            kv = _dot_tn(kw, v_ref[pos, cols].astype(BF16))
            st_ref[rows, :] = jnp.exp(lgb * float(CHUNK)) * st + kv


def _ret_bwd_states(proj, cos, sgn_sin, raw_rows, *, batch, n_chunks):
    assert OFF_RK % RET_WIDTH == 0 and OFF_RV % RET_WIDTH == 0
    kblk, vblk = OFF_RK // RET_WIDTH, OFF_RV // RET_WIDTH
    per = BWD_CHUNKS_PER_STEP
    assert n_chunks % per == 0
    n_steps = n_chunks // per
    last = n_steps - 1
    return pl.pallas_call(
        _ret_bwd_state_kernel,
        grid=(batch, n_steps),
        in_specs=[
            pl.BlockSpec((per * CHUNK, RET_WIDTH), lambda b, t: (b * n_steps + last - t, kblk)),
            pl.BlockSpec((per * CHUNK, RET_WIDTH), lambda b, t: (b * n_steps + last - t, vblk)),
            pl.BlockSpec((per * CHUNK, HEAD_DIM), lambda b, t: (last - t, 0)),
            pl.BlockSpec((per * CHUNK, HEAD_DIM), lambda b, t: (last - t, 0)),
            pl.BlockSpec((2 * N_RET_HEADS, HEAD_DIM), lambda b, t: (0, 0)),
        ],
        out_specs=pl.BlockSpec((1, per, RET_WIDTH, HEAD_DIM), lambda b, t: (b, last - t, 0, 0)),
        out_shape=jax.ShapeDtypeStruct((batch, n_chunks, RET_WIDTH, HEAD_DIM), BF16),
        scratch_shapes=[pltpu.VMEM((RET_WIDTH, HEAD_DIM), F32)],
        compiler_params=pltpu.CompilerParams(
            dimension_semantics=("parallel", "arbitrary"),
            vmem_limit_bytes=_vmem_limit(4 * per * CHUNK * RET_WIDTH * 4 + (6 << 20))),
        name="ret_bwd_states",
    )(proj, proj, cos, sgn_sin, raw_rows)


def _sgu(p_ref, lng_ref, lnb_ref, ws_ref, bs_ref, o_ref):
    sqrt_half = np.sqrt(0.5).astype(np.float32)

    def gelu(t):
        return 0.5 * t * (1.0 + lax.erf(t * sqrt_half))

    u = gelu(p_ref[:, OFF_U:OFF_U + SGU_WIDTH])
    v = gelu(p_ref[:, OFF_V:OFF_V + SGU_WIDTH])
    mu = jnp.mean(v, axis=-1, keepdims=True)
    var = jnp.mean(jnp.square(v - mu), axis=-1, keepdims=True)
    vn = (v - mu) * lax.rsqrt(var + NORM_EPS) * lng_ref[...] + lnb_ref[...]
    for g in range(N_SGU_GROUPS):
        cols = slice(g * HEAD_DIM, (g + 1) * HEAD_DIM)
        s = _dot(ws_ref[g], vn[:, cols].astype(BF16)) + bs_ref[g]
        o_ref[:, cols] = (u[:, cols] * s).astype(o_ref.dtype)


N_RET_TABLES = 4


def _ret_table(tab_ref, h, t):
    base = (h * N_RET_TABLES + t) * CHUNK
    return tab_ref.at[base:base + CHUNK, :]


def _fill_ret_tables(raw_ref, tab_ref):
    row = _row_index_f32()
    col = lax.broadcasted_iota(jnp.int32, (CHUNK, CHUNK), 1).astype(F32)
    delta = row - col
    for h in range(N_RET_HEADS):
        lgf = -jnp.exp(raw_ref[h:h + 1, :])
        lgb = -jnp.exp(raw_ref[N_RET_HEADS + h: N_RET_HEADS + h + 1, :])
        _ret_table(tab_ref, h, 0)[...] = jnp.where(
            delta >= 0.0,
            jnp.exp(lgf * jnp.maximum(delta, 0.0)),
            jnp.exp(lgb * jnp.maximum(-delta, 0.0)))
        _ret_table(tab_ref, h, 1)[...] = jnp.exp(lgf * (float(CHUNK - 1) - row))
        _ret_table(tab_ref, h, 2)[...] = jnp.exp(lgf * (row + 1.0))
        _ret_table(tab_ref, h, 3)[...] = jnp.exp(lgb * (float(CHUNK) - row))


def _retention(p_ref, sb_ref, raw_ref, cos, sin, o_ref, sf_ref, tab_ref):
    scale = HEAD_DIM ** -0.5
    for h in range(N_RET_HEADS):
        rows = slice(h * HEAD_DIM, (h + 1) * HEAD_DIM)
        lgf = -jnp.exp(raw_ref[h:h + 1, :])
        q = _rope(_head(p_ref, OFF_RQ, h), cos, sin)
        k = _rope(_head(p_ref, OFF_RK, h), cos, sin) * scale
        v = _head(p_ref, OFF_RV, h).astype(BF16)
        scores = _dot_nt(q.astype(BF16), k.astype(BF16)) * _ret_table(tab_ref, h, 0)[...]
        inner = _dot(scores.astype(BF16), v)
        sf_prev = sf_ref[rows, :]
        kv_f = _dot_tn((k * _ret_table(tab_ref, h, 1)[...]).astype(BF16), v)
        cross_f = _dot((q * _ret_table(tab_ref, h, 2)[...]).astype(BF16), sf_prev.astype(BF16))
        sf_ref[rows, :] = jnp.exp(lgf * float(CHUNK)) * sf_prev + kv_f
        cross_b = _dot((q * _ret_table(tab_ref, h, 3)[...]).astype(BF16), sb_ref[0, 0, rows, :])
        rf = (inner + cross_f) + cross_b
        rn = rf * lax.rsqrt(jnp.mean(rf * rf, axis=-1, keepdims=True) + NORM_EPS)
        gate = _head(p_ref, OFF_RG, h)
        o_ref[:, SGU_WIDTH + h * HEAD_DIM: SGU_WIDTH + (h + 1) * HEAD_DIM] = (
            gate * jax.nn.sigmoid(gate) * rn).astype(o_ref.dtype)


def _attention(p_ref, kvp_ref, kvn_ref, cos3_ref, sin3_ref, sink_ref, o_ref, chunk, seq):
    cos3 = cos3_ref[...]
    sin3 = sin3_ref[...]
    q_scale = HEAD_DIM ** -0.5 * LOG2_E
    cos_q = cos3[CHUNK:2 * CHUNK] * q_scale
    sin_q = sin3[CHUNK:2 * CHUNK] * q_scale
    kj = lax.broadcasted_iota(jnp.int32, (3 * CHUNK, CHUNK), 0)
    qi = lax.broadcasted_iota(jnp.int32, (3 * CHUNK, CHUNK), 1)
    rel = kj - CHUNK - qi
    kpos = (chunk - 1) * CHUNK + kj
    valid = (jnp.abs(rel) <= WINDOW) & (kpos >= 0) & (kpos < seq)
    valid = jnp.concatenate([valid] * Q_PER_KV, axis=1)
    out_off = SGU_WIDTH + RET_WIDTH
    for kh in range(N_KV_HEADS):
        kcols = slice(kh * HEAD_DIM, (kh + 1) * HEAD_DIM)
        vcols = slice(KV_WIDTH + kh * HEAD_DIM, KV_WIDTH + (kh + 1) * HEAD_DIM)
        k3 = jnp.concatenate([kvp_ref[:, kcols], _head(p_ref, OFF_AK, kh), kvn_ref[:, kcols]], axis=0)
        v3 = jnp.concatenate([kvp_ref[:, vcols], _head(p_ref, OFF_AV, kh), kvn_ref[:, vcols]], axis=0)
        k3 = _rope(k3, cos3, sin3).astype(BF16)
        qs = jnp.concatenate(
            [_rope(_head(p_ref, OFF_AQ, kh * Q_PER_KV + g), cos_q, sin_q) for g in range(Q_PER_KV)],
            axis=0)
        sink = LOG2_E * jnp.concatenate(
            [sink_ref[kh * Q_PER_KV + g: kh * Q_PER_KV + g + 1, :] for g in range(Q_PER_KV)], axis=1)
        s = _dot_nt(k3, qs.astype(BF16))
        s = jnp.concatenate([
            jnp.where(valid[:CHUNK], s[:CHUNK], NEG_INF),
            s[CHUNK:2 * CHUNK],
            jnp.where(valid[2 * CHUNK:], s[2 * CHUNK:], NEG_INF)], axis=0)
        m = jnp.maximum(jnp.max(s, axis=0, keepdims=True), sink)
        p = jnp.exp2(s - m)
        inv = 1.0 / (jnp.sum(p, axis=0, keepdims=True) + jnp.exp2(sink - m))
        o_t = _dot_tn(v3.astype(BF16), (p * inv).astype(BF16)).astype(o_ref.dtype)
        for g in range(Q_PER_KV):
            hq = kh * Q_PER_KV + g
            o_ref[:, out_off + hq * HEAD_DIM: out_off + (hq + 1) * HEAD_DIM] = (
                o_t[:, g * CHUNK:(g + 1) * CHUNK].T)


def _mixer_kernel(p_ref, kvp_ref, kvn_ref, cos3_ref, sin3_ref, sb_ref, lng_ref, lnb_ref,
                  ws_ref, bs_ref, raw_ref, sink_ref, cast_src_ref, o_ref, cast_dst_ref,
                  sf_ref, tab_ref, *, seq):
    chunk = pl.program_id(1)

    @pl.when(chunk == 0)
    def _():
        sf_ref[...] = jnp.zeros_like(sf_ref)
        _fill_ret_tables(raw_ref, tab_ref)

    cast_dst_ref[...] = cast_src_ref[...].astype(cast_dst_ref.dtype)
    cos = cos3_ref[CHUNK:2 * CHUNK, :]
    sin = sin3_ref[CHUNK:2 * CHUNK, :]
    _attention(p_ref, kvp_ref, kvn_ref, cos3_ref, sin3_ref, sink_ref, o_ref, chunk, seq)
    _retention(p_ref, sb_ref, raw_ref, cos, sin, o_ref, sf_ref, tab_ref)
    _sgu(p_ref, lng_ref, lnb_ref, ws_ref, bs_ref, o_ref)


def _mixers(proj, sb, cos3, sin3, lng, lnb, ws, bias, raw_rows, sink_rows, side_cast,
            *, batch, n_chunks):
    m = proj.shape[0]
    kv_blk = OFF_AK // (2 * KV_WIDTH)
    assert kv_blk * 2 * KV_WIDTH == OFF_AK
    last = n_chunks - 1
    const2 = lambda b, n: (0, 0)
    side_in, side_out, side_shape, side_vmem = _side_cast_specs(
        (side_cast,), batch * n_chunks, lambda b, n: b * n_chunks + n)
    est = (2 * CHUNK * IN_WIDTH * 4 + 4 * CHUNK * 2 * KV_WIDTH * 4 + 2 * CHUNK * MIX_WIDTH * 2
           + (12 << 20) + side_vmem)
    return pl.pallas_call(
        functools.partial(_mixer_kernel, seq=n_chunks * CHUNK),
        grid=(batch, n_chunks),
        in_specs=[
            pl.BlockSpec((CHUNK, IN_WIDTH), lambda b, n: (b * n_chunks + n, 0)),
            pl.BlockSpec((CHUNK, 2 * KV_WIDTH),
                         lambda b, n: (b * n_chunks + jnp.maximum(n - 1, 0), kv_blk)),
            pl.BlockSpec((CHUNK, 2 * KV_WIDTH),
                         lambda b, n: (b * n_chunks + jnp.minimum(n + 1, last), kv_blk)),
            pl.BlockSpec((3 * CHUNK, HEAD_DIM), lambda b, n: (n, 0)),
            pl.BlockSpec((3 * CHUNK, HEAD_DIM), lambda b, n: (n, 0)),
            pl.BlockSpec((1, 1, RET_WIDTH, HEAD_DIM), lambda b, n: (b, n, 0, 0)),
            pl.BlockSpec((1, SGU_WIDTH), const2),
            pl.BlockSpec((1, SGU_WIDTH), const2),
            pl.BlockSpec((N_SGU_GROUPS, CHUNK, CHUNK), lambda b, n: (0, 0, 0)),
            pl.BlockSpec((N_SGU_GROUPS, CHUNK, HEAD_DIM), lambda b, n: (0, 0, 0)),
            pl.BlockSpec((2 * N_RET_HEADS, HEAD_DIM), const2),
            pl.BlockSpec((N_Q_HEADS, HEAD_DIM), const2),
            *side_in,
        ],
        out_specs=[pl.BlockSpec((CHUNK, MIX_WIDTH), lambda b, n: (b * n_chunks + n, 0)), *side_out],
        out_shape=[jax.ShapeDtypeStruct((m, MIX_WIDTH), BF16), *side_shape],
        scratch_shapes=[pltpu.VMEM((RET_WIDTH, HEAD_DIM), F32),
                        pltpu.VMEM((N_RET_HEADS * N_RET_TABLES * CHUNK, HEAD_DIM), F32)],
        compiler_params=pltpu.CompilerParams(
            dimension_semantics=("arbitrary", "arbitrary"),
            vmem_limit_bytes=_vmem_limit(est)),
        name="mixers",
    )(proj, proj, proj, cos3, sin3, sb, lng, lnb, ws, bias, raw_rows, sink_rows, side_cast[0])


def _rope_tables(seq):
    pos = jnp.arange(seq, dtype=F32)
    inv = ROPE_THETA ** (-jnp.arange(0, HEAD_DIM, 2, dtype=F32) / HEAD_DIM)
    ang = pos[:, None] * inv[None, :]
    ang = jnp.concatenate([ang, ang], axis=-1)
    sgn = jnp.where(jnp.arange(HEAD_DIM) < HEAD_DIM // 2, -1.0, 1.0).astype(F32)
    return jnp.cos(ang), jnp.sin(ang) * sgn[None, :]


def _three_chunk_view(t, n_chunks):
    seq = t.shape[0]
    pad = jnp.zeros((CHUNK, HEAD_DIM), t.dtype)
    tp = jnp.concatenate([pad, t, pad], axis=0)
    parts = [tp[s * CHUNK: s * CHUNK + seq].reshape(n_chunks, CHUNK, HEAD_DIM) for s in range(3)]
    return jnp.concatenate(parts, axis=1).reshape(n_chunks * 3 * CHUNK, HEAD_DIM)


def kernel(x, ln_mix_g, w_in, sgu_ln_g, sgu_ln_b, sgu_w, sgu_b, ret_log_decay, attn_sink,
           w_out, ln_mlp_g, w_up, w_down, final_norm_g):
    batch, seq, d_model = x.shape
    depth = w_in.shape[0]
    assert w_in.shape[2] == IN_WIDTH and w_out.shape[1] == MIX_WIDTH and seq % CHUNK == 0
    n_chunks = seq // CHUNK
    m = batch * seq

    cos, sgn_sin = _rope_tables(seq)
    cos3 = _three_chunk_view(cos, n_chunks)
    sin3 = _three_chunk_view(sgn_sin, n_chunks)

    next_gains = [ln_mix_g[l] for l in range(1, depth)] + [final_norm_g]

    w_in_b = _cast_layer(w_in, 0)
    x2 = x.reshape(m, d_model)
    xg, ss = _prenorm(x2, ln_mix_g[0])
    for l in range(depth):
        raw_rows = jnp.broadcast_to(
            ret_log_decay[l].astype(F32).reshape(2 * N_RET_HEADS, 1), (2 * N_RET_HEADS, HEAD_DIM))
        sink_rows = jnp.broadcast_to(
            attn_sink[l].astype(F32).reshape(N_Q_HEADS, 1), (N_Q_HEADS, HEAD_DIM))
        sgu_bias = jnp.broadcast_to(
            sgu_b[l].astype(F32)[:, :, None], (N_SGU_GROUPS, CHUNK, HEAD_DIM))
        proj, w_out_b = _scaled_matmul(
            xg, ss, w_in_b, relu2=False, out_dtype=F32, side_casts=((w_out, l),))
        sb = _ret_bwd_states(proj, cos, sgn_sin, raw_rows, batch=batch, n_chunks=n_chunks)
        mix, w_up_b = _mixers(proj, sb, cos3, sin3,
                              sgu_ln_g[l].reshape(1, SGU_WIDTH), sgu_ln_b[l].reshape(1, SGU_WIDTH),
                              sgu_w[l].astype(BF16), sgu_bias, raw_rows, sink_rows, (w_up, l),
                              batch=batch, n_chunks=n_chunks)
        x2, ss, xg = _matmul_res(mix, w_out_b, x2, ln_mlp_g[l], tm=1024, tn=512, tk=MIX_WIDTH)
        last = l + 1 == depth
        hid, w_down_b, *maybe_w_in = _scaled_matmul(
            xg, ss, w_up_b, relu2=True, out_dtype=BF16,
            side_casts=((w_down, l),) if last else ((w_down, l), (w_in, l + 1)))
        w_in_b = None if last else maybe_w_in[0]
        x2, ss, *maybe_xg = _matmul_res(hid, w_down_b, x2, next_gains[l], tm=1024, tn=1024, tk=2048,
                                       emit_xg=not last)
        xg = None if last else maybe_xg[0]
    return _final_scale(x2, ss, final_norm_g).reshape(batch, seq, d_model)
```

```python
import functools

import numpy as np
import jax
import jax.numpy as jnp
from jax import lax
from jax.experimental import pallas as pl
from jax.experimental.pallas import tpu as pltpu

HEAD_DIM = 128
CHUNK = 128
WINDOW = 128
ROPE_THETA = 10000.0
NORM_EPS = 1e-5
NEG_INF = -1e30
LOG2_E = float(np.log2(np.e))
N_SGU_GROUPS = 8
N_RET_HEADS = 8
N_Q_HEADS = 16
N_KV_HEADS = 4
Q_PER_KV = N_Q_HEADS // N_KV_HEADS

SGU_WIDTH = N_SGU_GROUPS * HEAD_DIM
RET_WIDTH = N_RET_HEADS * HEAD_DIM
ATT_WIDTH = N_Q_HEADS * HEAD_DIM
KV_WIDTH = N_KV_HEADS * HEAD_DIM
OFF_U = 0
OFF_V = OFF_U + SGU_WIDTH
OFF_RQ = OFF_V + SGU_WIDTH
OFF_RK = OFF_RQ + RET_WIDTH
OFF_RV = OFF_RK + RET_WIDTH
OFF_RG = OFF_RV + RET_WIDTH
OFF_AQ = OFF_RG + RET_WIDTH
OFF_AK = OFF_AQ + ATT_WIDTH
OFF_AV = OFF_AK + KV_WIDTH
IN_WIDTH = OFF_AV + KV_WIDTH
MIX_WIDTH = SGU_WIDTH + RET_WIDTH + ATT_WIDTH

V7X_VMEM_BYTES = 64 * 1024 * 1024
V7X_VMEM_BUDGET = 56 * 1024 * 1024
CAST_SLAB_BYTES = 4 * 1024 * 1024
BF16_SUBLANES = 16

F32 = jnp.float32
BF16 = jnp.bfloat16


def _vmem_limit(est_bytes):
    assert est_bytes <= V7X_VMEM_BUDGET, est_bytes
    return min(max(int(est_bytes * 1.25), 16 << 20), V7X_VMEM_BUDGET)


def _dot(a, b):
    return jnp.dot(a, b, preferred_element_type=F32)


def _dot_nt(a, b):
    return lax.dot_general(a, b, (((1,), (1,)), ((), ())), preferred_element_type=F32)


def _dot_tn(a, b):
    return lax.dot_general(a, b, (((0,), (0,)), ((), ())), preferred_element_type=F32)


def _cast_kernel(w_ref, o_ref):
    o_ref[...] = w_ref[...].astype(o_ref.dtype)


def _cast_layer(w, layer):
    _, k, n = w.shape
    tr = 1
    while tr * 2 * n * 4 <= CAST_SLAB_BYTES and k % (tr * 2) == 0:
        tr *= 2
    return pl.pallas_call(
        _cast_kernel,
        grid=(k // tr,),
        in_specs=[pl.BlockSpec((None, tr, n), lambda i: (layer, i, 0))],
        out_specs=pl.BlockSpec((tr, n), lambda i: (i, 0)),
        out_shape=jax.ShapeDtypeStruct((k, n), BF16),
        compiler_params=pltpu.CompilerParams(
            dimension_semantics=("parallel",),
            vmem_limit_bytes=_vmem_limit(2 * tr * n * 6 + tr * n * 4)),
        name="cast_bf16",
    )(w)


def _side_cast_specs(side_casts, steps, step_index):
    in_specs, out_specs, out_shape, vmem = [], [], [], 0
    for w, layer in side_casts:
        _, rows, cols = w.shape
        n_slabs = 1
        while n_slabs * 2 <= steps and rows % (n_slabs * 2) == 0 and (rows // (n_slabs * 2)) % BF16_SUBLANES == 0:
            n_slabs *= 2
        slab = rows // n_slabs

        def slab_index(*ids, _last=n_slabs - 1):
            return jnp.minimum(step_index(*ids), _last)

        in_specs.append(pl.BlockSpec(
            (None, slab, cols), lambda *ids, _l=layer, _s=slab_index: (_l, _s(*ids), 0)))
        out_specs.append(pl.BlockSpec((slab, cols), lambda *ids, _s=slab_index: (_s(*ids), 0)))
        out_shape.append(jax.ShapeDtypeStruct((rows, cols), BF16))
        vmem += 2 * slab * cols * 6 + slab * cols * 4
    return in_specs, out_specs, out_shape, vmem


def _row_rsqrt(ss, width):
    return lax.rsqrt(ss / float(width) + NORM_EPS)


def _prenorm_kernel(x_ref, g_ref, xg_ref, ss_ref):
    x = x_ref[...]
    xg_ref[...] = (x * g_ref[...]).astype(BF16)
    ss_ref[...] = jnp.sum(x * x, axis=-1, keepdims=True)


def _prenorm(x, g, *, tm=256):
    m, k = x.shape
    return pl.pallas_call(
        _prenorm_kernel,
        grid=(m // tm,),
        in_specs=[pl.BlockSpec((tm, k), lambda i: (i, 0)),
                  pl.BlockSpec((1, k), lambda i: (0, 0))],
        out_specs=[pl.BlockSpec((tm, k), lambda i: (i, 0)),
                   pl.BlockSpec((tm, 1), lambda i: (i, 0))],
        out_shape=[jax.ShapeDtypeStruct((m, k), BF16), jax.ShapeDtypeStruct((m, 1), F32)],
        compiler_params=pltpu.CompilerParams(
            dimension_semantics=("parallel",),
            vmem_limit_bytes=_vmem_limit(2 * tm * k * 6 + 2 * tm * k * 4)),
        name="prenorm",
    )(x, g.reshape(1, k))


def _scaled_matmul_kernel(a_ref, ss_ref, w_ref, *rest, relu2, n_side):
    side_in, o_ref, side_out = rest[:n_side], rest[n_side], rest[n_side + 1:]
    acc = _dot(a_ref[...], w_ref[...]) * _row_rsqrt(ss_ref[...], a_ref.shape[1])
    if relu2:
        acc = jnp.square(jnp.maximum(acc, 0.0))
    o_ref[...] = acc.astype(o_ref.dtype)
    for src_ref, dst_ref in zip(side_in, side_out):
        dst_ref[...] = src_ref[...].astype(dst_ref.dtype)


def _scaled_matmul(a, ss, w, *, relu2, out_dtype, side_casts=(), tm=1024, tn=1024):
    m, k = a.shape
    n = w.shape[1]
    assert m % tm == 0 and n % tn == 0
    nj = n // tn
    out_bytes = jnp.dtype(out_dtype).itemsize
    side_in, side_out, side_shape, side_vmem = _side_cast_specs(
        side_casts, (m // tm) * nj, lambda i, j: i * nj + j)
    est = 2 * tm * k * 2 + 2 * k * tn * 2 + 2 * tm * tn * out_bytes + 2 * tm * tn * 4 + side_vmem
    return pl.pallas_call(
        functools.partial(_scaled_matmul_kernel, relu2=relu2, n_side=len(side_casts)),
        grid=(m // tm, nj),
        in_specs=[
            pl.BlockSpec((tm, k), lambda i, j: (i, 0)),
            pl.BlockSpec((tm, 1), lambda i, j: (i, 0)),
            pl.BlockSpec((k, tn), lambda i, j: (0, j)),
            *side_in,
        ],
        out_specs=[pl.BlockSpec((tm, tn), lambda i, j: (i, j)), *side_out],
        out_shape=[jax.ShapeDtypeStruct((m, n), out_dtype), *side_shape],
        compiler_params=pltpu.CompilerParams(
            dimension_semantics=("arbitrary", "arbitrary"),
            vmem_limit_bytes=_vmem_limit(est)),
        name="scaled_matmul_relu2" if relu2 else "scaled_matmul",
    )(a, ss, w, *[src for src, _ in side_casts])


def _matmul_res_kernel(a_ref, w_ref, res_ref, g_ref, o_ref, ss_ref, *rest, nk, emit_xg):
    maybe_xg_ref = rest[0] if emit_xg else None
    i = pl.program_id(0)
    j = pl.program_id(1)
    kk = pl.program_id(2)
    tm, tn = o_ref.shape

    @pl.when(jnp.logical_and(j == 0, kk == 0))
    def _():
        ss_ref[...] = jnp.zeros_like(ss_ref)

    def emit_norm_inputs(rows, x_new):
        if maybe_xg_ref is not None:
            maybe_xg_ref[rows, :] = (x_new * g_ref[...]).astype(BF16)
        ss_ref[rows, :] += jnp.sum(x_new * x_new, axis=-1, keepdims=True)

    if nk == 1:
        x_new = res_ref[...] + _dot(a_ref[...], w_ref[...])
        o_ref[...] = x_new
        emit_norm_inputs(slice(None), x_new)
        return

    res_sem = rest[-1]

    def res_copy():
        rows = pl.ds(pl.multiple_of(i * tm, tm), tm)
        cols = pl.ds(pl.multiple_of(j * tn, tn), tn)
        return pltpu.make_async_copy(res_ref.at[rows, cols], o_ref, res_sem)

    @pl.when(kk == 0)
    def _():
        res_copy().start()
        part = _dot(a_ref[...], w_ref[...])
        res_copy().wait()
        o_ref[...] += part

    @pl.when(jnp.logical_and(kk > 0, kk < nk - 1))
    def _():
        o_ref[...] += _dot(a_ref[...], w_ref[...])

    @pl.when(kk == nk - 1)
    def _():
        x_new = o_ref[...] + _dot(a_ref[...], w_ref[...])
        o_ref[...] = x_new
        emit_norm_inputs(slice(None), x_new)


def _matmul_res(a, w, res, g_next, *, tm, tn, tk, emit_xg=True):
    m, k = a.shape
    n = w.shape[1]
    assert m % tm == 0 and n % tn == 0 and k % tk == 0
    nk, nj = k // tk, n // tn
    k_tiled = nk > 1
    est = (2 * tm * tk * 2 + 2 * tk * tn * 2 + 2 * tm * tn * (4 + 4 * (not k_tiled) + 2 * emit_xg)
           + 2 * tm * tn * 4)
    tile = pl.BlockSpec((tm, tn), lambda i, j, kk: (i, j))
    out_specs = [tile, pl.BlockSpec((tm, 1), lambda i, j, kk: (i, 0))]
    out_shape = [jax.ShapeDtypeStruct((m, n), F32), jax.ShapeDtypeStruct((m, 1), F32)]
    if emit_xg:
        out_specs.append(tile)
        out_shape.append(jax.ShapeDtypeStruct((m, n), BF16))
    return pl.pallas_call(
        functools.partial(_matmul_res_kernel, nk=nk, emit_xg=emit_xg),
        grid=(m // tm, nj, nk),
        in_specs=[
            pl.BlockSpec((tm, tk), lambda i, j, kk: (i, kk)),
            pl.BlockSpec((tk, tn), lambda i, j, kk: (kk, j)),
            pl.BlockSpec(memory_space=pl.ANY) if k_tiled else tile,
            pl.BlockSpec((1, tn), lambda i, j, kk: (0, j)),
        ],
        out_specs=out_specs,
        out_shape=out_shape,
        scratch_shapes=[pltpu.SemaphoreType.DMA(())] if k_tiled else [],
        compiler_params=pltpu.CompilerParams(
            dimension_semantics=("arbitrary", "arbitrary", "arbitrary"),
            vmem_limit_bytes=_vmem_limit(est)),
        name="matmul_res_k%d" % nk,
    )(a, w, res, g_next.reshape(1, n))


def _final_scale_kernel(x_ref, ss_ref, g_ref, o_ref):
    o_ref[...] = x_ref[...] * _row_rsqrt(ss_ref[...], x_ref.shape[1]) * g_ref[...]


def _final_scale(x, ss, g, *, tm=512):
    m, k = x.shape
    return pl.pallas_call(
        _final_scale_kernel,
        grid=(m // tm,),
        in_specs=[pl.BlockSpec((tm, k), lambda i: (i, 0)),
                  pl.BlockSpec((tm, 1), lambda i: (i, 0)),
                  pl.BlockSpec((1, k), lambda i: (0, 0))],
        out_specs=pl.BlockSpec((tm, k), lambda i: (i, 0)),
        out_shape=jax.ShapeDtypeStruct((m, k), F32),
        compiler_params=pltpu.CompilerParams(
            dimension_semantics=("parallel",),
            vmem_limit_bytes=_vmem_limit(5 * tm * k * 4)),
        name="final_scale",
    )(x, ss, g.reshape(1, k))


def _rope(t, cos, sgn_sin):
    return t * cos + pltpu.roll(t, HEAD_DIM // 2, 1) * sgn_sin


def _row_index_f32():
    return lax.broadcasted_iota(jnp.int32, (CHUNK, HEAD_DIM), 0).astype(F32)


def _head(ref, off, h):
    return ref[:, off + h * HEAD_DIM: off + (h + 1) * HEAD_DIM]


BWD_CHUNKS_PER_STEP = 4


def _ret_bwd_state_kernel(k_ref, v_ref, cos_ref, sin_ref, raw_ref, sb_ref, st_ref):
    @pl.when(pl.program_id(1) == 0)
    def _():
        st_ref[...] = jnp.zeros_like(st_ref)

    row = _row_index_f32()
    scale = HEAD_DIM ** -0.5
    for c in reversed(range(BWD_CHUNKS_PER_STEP)):
        pos = slice(c * CHUNK, (c + 1) * CHUNK)
        cos = cos_ref[pos, :]
        sin = sin_ref[pos, :]
        for h in range(N_RET_HEADS):
            rows = slice(h * HEAD_DIM, (h + 1) * HEAD_DIM)
            cols = slice(h * HEAD_DIM, (h + 1) * HEAD_DIM)
            st = st_ref[rows, :]
            sb_ref[0, c, rows, :] = st.astype(BF16)
            lgb = -jnp.exp(raw_ref[N_RET_HEADS + h: N_RET_HEADS + h + 1, :])
            kh = _rope(k_ref[pos, cols], cos, sin) * scale
            kw = (kh * jnp.exp(lgb * row)).astype(BF16)
            kv = _dot_tn(kw, v_ref[pos, cols].astype(BF16))
            st_ref[rows, :] = jnp.exp(lgb * float(CHUNK)) * st + kv


def _ret_bwd_states(proj, cos, sgn_sin, raw_rows, *, batch, n_chunks):
    assert OFF_RK % RET_WIDTH == 0 and OFF_RV % RET_WIDTH == 0
    kblk, vblk = OFF_RK // RET_WIDTH, OFF_RV // RET_WIDTH
    per = BWD_CHUNKS_PER_STEP
    assert n_chunks % per == 0
    n_steps = n_chunks // per
    last = n_steps - 1
    return pl.pallas_call(
        _ret_bwd_state_kernel,
        grid=(batch, n_steps),
        in_specs=[
            pl.BlockSpec((per * CHUNK, RET_WIDTH), lambda b, t: (b * n_steps + last - t, kblk)),
            pl.BlockSpec((per * CHUNK, RET_WIDTH), lambda b, t: (b * n_steps + last - t, vblk)),
            pl.BlockSpec((per * CHUNK, HEAD_DIM), lambda b, t: (last - t, 0)),
            pl.BlockSpec((per * CHUNK, HEAD_DIM), lambda b, t: (last - t, 0)),
            pl.BlockSpec((2 * N_RET_HEADS, HEAD_DIM), lambda b, t: (0, 0)),
        ],
        out_specs=pl.BlockSpec((1, per, RET_WIDTH, HEAD_DIM), lambda b, t: (b, last - t, 0, 0)),
        out_shape=jax.ShapeDtypeStruct((batch, n_chunks, RET_WIDTH, HEAD_DIM), BF16),
        scratch_shapes=[pltpu.VMEM((RET_WIDTH, HEAD_DIM), F32)],
        compiler_params=pltpu.CompilerParams(
            dimension_semantics=("parallel", "arbitrary"),
            vmem_limit_bytes=_vmem_limit(4 * per * CHUNK * RET_WIDTH * 4 + (6 << 20))),
        name="ret_bwd_states",
    )(proj, proj, cos, sgn_sin, raw_rows)


def _sgu(p_ref, lng_ref, lnb_ref, ws_ref, bs_ref, o_ref):
    sqrt_half = np.sqrt(0.5).astype(np.float32)

    def gelu(t):
        return 0.5 * t * (1.0 + lax.erf(t * sqrt_half))

    u = gelu(p_ref[:, OFF_U:OFF_U + SGU_WIDTH])
    v = gelu(p_ref[:, OFF_V:OFF_V + SGU_WIDTH])
    mu = jnp.mean(v, axis=-1, keepdims=True)
    var = jnp.mean(jnp.square(v - mu), axis=-1, keepdims=True)
    vn = (v - mu) * lax.rsqrt(var + NORM_EPS) * lng_ref[...] + lnb_ref[...]
    for g in range(N_SGU_GROUPS):
        cols = slice(g * HEAD_DIM, (g + 1) * HEAD_DIM)
        s = _dot(ws_ref[g], vn[:, cols].astype(BF16)) + bs_ref[g]
        o_ref[:, cols] = (u[:, cols] * s).astype(o_ref.dtype)


N_RET_TABLES = 4


def _ret_table(tab_ref, h, t):
    base = (h * N_RET_TABLES + t) * CHUNK
    return tab_ref.at[base:base + CHUNK, :]


def _fill_ret_tables(raw_ref, tab_ref):
    row = _row_index_f32()
    col = lax.broadcasted_iota(jnp.int32, (CHUNK, CHUNK), 1).astype(F32)
    delta = row - col
    for h in range(N_RET_HEADS):
        lgf = -jnp.exp(raw_ref[h:h + 1, :])
        lgb = -jnp.exp(raw_ref[N_RET_HEADS + h: N_RET_HEADS + h + 1, :])
        _ret_table(tab_ref, h, 0)[...] = jnp.where(
            delta >= 0.0,
            jnp.exp(lgf * jnp.maximum(delta, 0.0)),
            jnp.exp(lgb * jnp.maximum(-delta, 0.0)))
        _ret_table(tab_ref, h, 1)[...] = jnp.exp(lgf * (float(CHUNK - 1) - row))
        _ret_table(tab_ref, h, 2)[...] = jnp.exp(lgf * (row + 1.0))
        _ret_table(tab_ref, h, 3)[...] = jnp.exp(lgb * (float(CHUNK) - row))


def _retention(p_ref, sb_ref, raw_ref, cos, sin, o_ref, sf_ref, tab_ref):
    scale = HEAD_DIM ** -0.5
    for h in range(N_RET_HEADS):
        rows = slice(h * HEAD_DIM, (h + 1) * HEAD_DIM)
        lgf = -jnp.exp(raw_ref[h:h + 1, :])
        q = _rope(_head(p_ref, OFF_RQ, h), cos, sin)
        k = _rope(_head(p_ref, OFF_RK, h), cos, sin) * scale
        v = _head(p_ref, OFF_RV, h).astype(BF16)
        scores = _dot_nt(q.astype(BF16), k.astype(BF16)) * _ret_table(tab_ref, h, 0)[...]
        inner = _dot(scores.astype(BF16), v)
        sf_prev = sf_ref[rows, :]
        kv_f = _dot_tn((k * _ret_table(tab_ref, h, 1)[...]).astype(BF16), v)
        cross_f = _dot((q * _ret_table(tab_ref, h, 2)[...]).astype(BF16), sf_prev.astype(BF16))
        sf_ref[rows, :] = jnp.exp(lgf * float(CHUNK)) * sf_prev + kv_f
        cross_b = _dot((q * _ret_table(tab_ref, h, 3)[...]).astype(BF16), sb_ref[0, 0, rows, :])
        rf = (inner + cross_f) + cross_b
        rn = rf * lax.rsqrt(jnp.mean(rf * rf, axis=-1, keepdims=True) + NORM_EPS)
        gate = _head(p_ref, OFF_RG, h)
        o_ref[:, SGU_WIDTH + h * HEAD_DIM: SGU_WIDTH + (h + 1) * HEAD_DIM] = (
            gate * jax.nn.sigmoid(gate) * rn).astype(o_ref.dtype)


def _attention(p_ref, kvp_ref, kvn_ref, cos3_ref, sin3_ref, sink_ref, o_ref, chunk, seq):
    cos3 = cos3_ref[...]
    sin3 = sin3_ref[...]
    q_scale = HEAD_DIM ** -0.5 * LOG2_E
    cos_q = cos3[CHUNK:2 * CHUNK] * q_scale
    sin_q = sin3[CHUNK:2 * CHUNK] * q_scale
    kj = lax.broadcasted_iota(jnp.int32, (3 * CHUNK, CHUNK), 0)
    qi = lax.broadcasted_iota(jnp.int32, (3 * CHUNK, CHUNK), 1)
    rel = kj - CHUNK - qi
    kpos = (chunk - 1) * CHUNK + kj
    valid = (jnp.abs(rel) <= WINDOW) & (kpos >= 0) & (kpos < seq)
    valid = jnp.concatenate([valid] * Q_PER_KV, axis=1)
    out_off = SGU_WIDTH + RET_WIDTH
    for kh in range(N_KV_HEADS):
        kcols = slice(kh * HEAD_DIM, (kh + 1) * HEAD_DIM)
        vcols = slice(KV_WIDTH + kh * HEAD_DIM, KV_WIDTH + (kh + 1) * HEAD_DIM)
        k3 = jnp.concatenate([kvp_ref[:, kcols], _head(p_ref, OFF_AK, kh), kvn_ref[:, kcols]], axis=0)
        v3 = jnp.concatenate([kvp_ref[:, vcols], _head(p_ref, OFF_AV, kh), kvn_ref[:, vcols]], axis=0)
        k3 = _rope(k3, cos3, sin3).astype(BF16)
        qs = jnp.concatenate(
            [_rope(_head(p_ref, OFF_AQ, kh * Q_PER_KV + g), cos_q, sin_q) for g in range(Q_PER_KV)],
            axis=0)
        sink = LOG2_E * jnp.concatenate(
            [sink_ref[kh * Q_PER_KV + g: kh * Q_PER_KV + g + 1, :] for g in range(Q_PER_KV)], axis=1)
        s = _dot_nt(k3, qs.astype(BF16))
        s = jnp.concatenate([
            jnp.where(valid[:CHUNK], s[:CHUNK], NEG_INF),
            s[CHUNK:2 * CHUNK],
            jnp.where(valid[2 * CHUNK:], s[2 * CHUNK:], NEG_INF)], axis=0)
        m = jnp.maximum(jnp.max(s, axis=0, keepdims=True), sink)
        p = jnp.exp2(s - m)
        inv = 1.0 / (jnp.sum(p, axis=0, keepdims=True) + jnp.exp2(sink - m))
        o_t = _dot_tn(v3.astype(BF16), (p * inv).astype(BF16)).astype(o_ref.dtype)
        for g in range(Q_PER_KV):
            hq = kh * Q_PER_KV + g
            o_ref[:, out_off + hq * HEAD_DIM: out_off + (hq + 1) * HEAD_DIM] = (
                o_t[:, g * CHUNK:(g + 1) * CHUNK].T)


def _mixer_kernel(p_ref, kvp_ref, kvn_ref, cos3_ref, sin3_ref, sb_ref, lng_ref, lnb_ref,
                  ws_ref, bs_ref, raw_ref, sink_ref, cast_src_ref, o_ref, cast_dst_ref,
                  sf_ref, tab_ref, *, seq):
    chunk = pl.program_id(1)

    @pl.when(chunk == 0)
    def _():
        sf_ref[...] = jnp.zeros_like(sf_ref)
        _fill_ret_tables(raw_ref, tab_ref)

    cast_dst_ref[...] = cast_src_ref[...].astype(cast_dst_ref.dtype)
    cos = cos3_ref[CHUNK:2 * CHUNK, :]
    sin = sin3_ref[CHUNK:2 * CHUNK, :]
    _attention(p_ref, kvp_ref, kvn_ref, cos3_ref, sin3_ref, sink_ref, o_ref, chunk, seq)
    _retention(p_ref, sb_ref, raw_ref, cos, sin, o_ref, sf_ref, tab_ref)
    _sgu(p_ref, lng_ref, lnb_ref, ws_ref, bs_ref, o_ref)


def _mixers(proj, sb, cos3, sin3, lng, lnb, ws, bias, raw_rows, sink_rows, side_cast,
            *, batch, n_chunks):
    m = proj.shape[0]
    kv_blk = OFF_AK // (2 * KV_WIDTH)
    assert kv_blk * 2 * KV_WIDTH == OFF_AK
    last = n_chunks - 1
    const2 = lambda b, n: (0, 0)
    side_in, side_out, side_shape, side_vmem = _side_cast_specs(
        (side_cast,), batch * n_chunks, lambda b, n: b * n_chunks + n)
    est = (2 * CHUNK * IN_WIDTH * 4 + 4 * CHUNK * 2 * KV_WIDTH * 4 + 2 * CHUNK * MIX_WIDTH * 2
           + (12 << 20) + side_vmem)
    return pl.pallas_call(
        functools.partial(_mixer_kernel, seq=n_chunks * CHUNK),
        grid=(batch, n_chunks),
        in_specs=[
            pl.BlockSpec((CHUNK, IN_WIDTH), lambda b, n: (b * n_chunks + n, 0)),
            pl.BlockSpec((CHUNK, 2 * KV_WIDTH),
                         lambda b, n: (b * n_chunks + jnp.maximum(n - 1, 0), kv_blk)),
            pl.BlockSpec((CHUNK, 2 * KV_WIDTH),
                         lambda b, n: (b * n_chunks + jnp.minimum(n + 1, last), kv_blk)),
            pl.BlockSpec((3 * CHUNK, HEAD_DIM), lambda b, n: (n, 0)),
            pl.BlockSpec((3 * CHUNK, HEAD_DIM), lambda b, n: (n, 0)),
            pl.BlockSpec((1, 1, RET_WIDTH, HEAD_DIM), lambda b, n: (b, n, 0, 0)),
            pl.BlockSpec((1, SGU_WIDTH), const2),
            pl.BlockSpec((1, SGU_WIDTH), const2),
            pl.BlockSpec((N_SGU_GROUPS, CHUNK, CHUNK), lambda b, n: (0, 0, 0)),
            pl.BlockSpec((N_SGU_GROUPS, CHUNK, HEAD_DIM), lambda b, n: (0, 0, 0)),
            pl.BlockSpec((2 * N_RET_HEADS, HEAD_DIM), const2),
            pl.BlockSpec((N_Q_HEADS, HEAD_DIM), const2),
            *side_in,
        ],
        out_specs=[pl.BlockSpec((CHUNK, MIX_WIDTH), lambda b, n: (b * n_chunks + n, 0)), *side_out],
        out_shape=[jax.ShapeDtypeStruct((m, MIX_WIDTH), BF16), *side_shape],
        scratch_shapes=[pltpu.VMEM((RET_WIDTH, HEAD_DIM), F32),
                        pltpu.VMEM((N_RET_HEADS * N_RET_TABLES * CHUNK, HEAD_DIM), F32)],
        compiler_params=pltpu.CompilerParams(
            dimension_semantics=("arbitrary", "arbitrary"),
            vmem_limit_bytes=_vmem_limit(est)),
        name="mixers",
    )(proj, proj, proj, cos3, sin3, sb, lng, lnb, ws, bias, raw_rows, sink_rows, side_cast[0])


def _rope_tables(seq):
    pos = jnp.arange(seq, dtype=F32)
    inv = ROPE_THETA ** (-jnp.arange(0, HEAD_DIM, 2, dtype=F32) / HEAD_DIM)
    ang = pos[:, None] * inv[None, :]
    ang = jnp.concatenate([ang, ang], axis=-1)
    sgn = jnp.where(jnp.arange(HEAD_DIM) < HEAD_DIM // 2, -1.0, 1.0).astype(F32)
    return jnp.cos(ang), jnp.sin(ang) * sgn[None, :]


def _three_chunk_view(t, n_chunks):
    seq = t.shape[0]
    pad = jnp.zeros((CHUNK, HEAD_DIM), t.dtype)
    tp = jnp.concatenate([pad, t, pad], axis=0)
    parts = [tp[s * CHUNK: s * CHUNK + seq].reshape(n_chunks, CHUNK, HEAD_DIM) for s in range(3)]
    return jnp.concatenate(parts, axis=1).reshape(n_chunks * 3 * CHUNK, HEAD_DIM)


def kernel(x, ln_mix_g, w_in, sgu_ln_g, sgu_ln_b, sgu_w, sgu_b, ret_log_decay, attn_sink,
           w_out, ln_mlp_g, w_up, w_down, final_norm_g):
    batch, seq, d_model = x.shape
    depth = w_in.shape[0]
    assert w_in.shape[2] == IN_WIDTH and w_out.shape[1] == MIX_WIDTH and seq % CHUNK == 0
    n_chunks = seq // CHUNK
    m = batch * seq

    cos, sgn_sin = _rope_tables(seq)
    cos3 = _three_chunk_view(cos, n_chunks)
    sin3 = _three_chunk_view(sgn_sin, n_chunks)

    next_gains = [ln_mix_g[l] for l in range(1, depth)] + [final_norm_g]

    w_in_b = _cast_layer(w_in, 0)
    x2 = x.reshape(m, d_model)
    xg, ss = _prenorm(x2, ln_mix_g[0])
    for l in range(depth):
        raw_rows = jnp.broadcast_to(
            ret_log_decay[l].astype(F32).reshape(2 * N_RET_HEADS, 1), (2 * N_RET_HEADS, HEAD_DIM))
        sink_rows = jnp.broadcast_to(
            attn_sink[l].astype(F32).reshape(N_Q_HEADS, 1), (N_Q_HEADS, HEAD_DIM))
        sgu_bias = jnp.broadcast_to(
            sgu_b[l].astype(F32)[:, :, None], (N_SGU_GROUPS, CHUNK, HEAD_DIM))
        proj, w_out_b = _scaled_matmul(
            xg, ss, w_in_b, relu2=False, out_dtype=F32, side_casts=((w_out, l),))
        sb = _ret_bwd_states(proj, cos, sgn_sin, raw_rows, batch=batch, n_chunks=n_chunks)
        mix, w_up_b = _mixers(proj, sb, cos3, sin3,
                              sgu_ln_g[l].reshape(1, SGU_WIDTH), sgu_ln_b[l].reshape(1, SGU_WIDTH),
                              sgu_w[l].astype(BF16), sgu_bias, raw_rows, sink_rows, (w_up, l),
                              batch=batch, n_chunks=n_chunks)
        x2, ss, xg = _matmul_res(mix, w_out_b, x2, ln_mlp_g[l], tm=1024, tn=512, tk=MIX_WIDTH)
        last = l + 1 == depth
        hid, w_down_b, *maybe_w_in = _scaled_matmul(
            xg, ss, w_up_b, relu2=True, out_dtype=BF16,
            side_casts=((w_down, l),) if last else ((w_down, l), (w_in, l + 1)))
        w_in_b = None if last else maybe_w_in[0]
        x2, ss, *maybe_xg = _matmul_res(hid, w_down_b, x2, next_gains[l], tm=1024, tn=1024,
                                       tk=4096 if last else 2048, emit_xg=not last)
        xg = None if last else maybe_xg[0]
    return _final_scale(x2, ss, final_norm_g).reshape(batch, seq, d_model)
```

```python
import functools

import numpy as np
import jax
import jax.numpy as jnp
from jax import lax
from jax.experimental import pallas as pl
from jax.experimental.pallas import tpu as pltpu

HEAD_DIM = 128
CHUNK = 128
WINDOW = 128
ROPE_THETA = 10000.0
NORM_EPS = 1e-5
NEG_INF = -1e30
LOG2_E = float(np.log2(np.e))
N_SGU_GROUPS = 8
N_RET_HEADS = 8
N_Q_HEADS = 16
N_KV_HEADS = 4
Q_PER_KV = N_Q_HEADS // N_KV_HEADS

SGU_WIDTH = N_SGU_GROUPS * HEAD_DIM
RET_WIDTH = N_RET_HEADS * HEAD_DIM
ATT_WIDTH = N_Q_HEADS * HEAD_DIM
KV_WIDTH = N_KV_HEADS * HEAD_DIM
OFF_U = 0
OFF_V = OFF_U + SGU_WIDTH
OFF_RQ = OFF_V + SGU_WIDTH
OFF_RK = OFF_RQ + RET_WIDTH
OFF_RV = OFF_RK + RET_WIDTH
OFF_RG = OFF_RV + RET_WIDTH
OFF_AQ = OFF_RG + RET_WIDTH
OFF_AK = OFF_AQ + ATT_WIDTH
OFF_AV = OFF_AK + KV_WIDTH
IN_WIDTH = OFF_AV + KV_WIDTH
MIX_WIDTH = SGU_WIDTH + RET_WIDTH + ATT_WIDTH

V7X_VMEM_BUDGET = 56 * 1024 * 1024
VMEM_PLAN_MARGIN = 4 * 1024 * 1024
CAST_SLAB_BYTES = 4 * 1024 * 1024
BF16_SUBLANES = 16

F32 = jnp.float32
BF16 = jnp.bfloat16


def _vmem_limit(est_bytes):
    assert est_bytes <= V7X_VMEM_BUDGET, est_bytes
    return min(max(int(est_bytes * 1.25), 16 << 20), V7X_VMEM_BUDGET)


def _dot(a, b):
    return jnp.dot(a, b, preferred_element_type=F32)


def _dot_nt(a, b):
    return lax.dot_general(a, b, (((1,), (1,)), ((), ())), preferred_element_type=F32)


def _dot_tn(a, b):
    return lax.dot_general(a, b, (((0,), (0,)), ((), ())), preferred_element_type=F32)


def _cast_kernel(w_ref, o_ref):
    o_ref[...] = w_ref[...].astype(o_ref.dtype)


def _cast_layer(w, layer):
    _, k, n = w.shape
    tr = 1
    while tr * 2 * n * 4 <= CAST_SLAB_BYTES and k % (tr * 2) == 0:
        tr *= 2
    return pl.pallas_call(
        _cast_kernel,
        grid=(k // tr,),
        in_specs=[pl.BlockSpec((None, tr, n), lambda i: (layer, i, 0))],
        out_specs=pl.BlockSpec((tr, n), lambda i: (i, 0)),
        out_shape=jax.ShapeDtypeStruct((k, n), BF16),
        compiler_params=pltpu.CompilerParams(
            dimension_semantics=("parallel",),
            vmem_limit_bytes=_vmem_limit(2 * tr * n * 6 + tr * n * 4)),
        name="cast_bf16",
    )(w)


def _side_cast_specs(side_casts, steps, step_index):
    in_specs, out_specs, out_shape, vmem = [], [], [], 0
    for w, layer in side_casts:
        _, rows, cols = w.shape
        n_slabs = 1
        while n_slabs * 2 <= steps and rows % (n_slabs * 2) == 0 and (rows // (n_slabs * 2)) % BF16_SUBLANES == 0:
            n_slabs *= 2
        slab = rows // n_slabs

        def slab_index(*ids, _last=n_slabs - 1):
            return jnp.minimum(step_index(*ids), _last)

        in_specs.append(pl.BlockSpec(
            (None, slab, cols), lambda *ids, _l=layer, _s=slab_index: (_l, _s(*ids), 0)))
        out_specs.append(pl.BlockSpec((slab, cols), lambda *ids, _s=slab_index: (_s(*ids), 0)))
        out_shape.append(jax.ShapeDtypeStruct((rows, cols), BF16))
        vmem += 2 * slab * cols * 6 + slab * cols * 4
    return in_specs, out_specs, out_shape, vmem


def _row_rsqrt(ss, width):
    return lax.rsqrt(ss / float(width) + NORM_EPS)


def _prenorm_kernel(x_ref, g_ref, xg_ref, ss_ref):
    x = x_ref[...]
    xg_ref[...] = (x * g_ref[...]).astype(BF16)
    ss_ref[...] = jnp.sum(x * x, axis=-1, keepdims=True)


def _prenorm(x, g, *, tm=256):
    m, k = x.shape
    return pl.pallas_call(
        _prenorm_kernel,
        grid=(m // tm,),
        in_specs=[pl.BlockSpec((tm, k), lambda i: (i, 0)),
                  pl.BlockSpec((1, k), lambda i: (0, 0))],
        out_specs=[pl.BlockSpec((tm, k), lambda i: (i, 0)),
                   pl.BlockSpec((tm, 1), lambda i: (i, 0))],
        out_shape=[jax.ShapeDtypeStruct((m, k), BF16), jax.ShapeDtypeStruct((m, 1), F32)],
        compiler_params=pltpu.CompilerParams(
            dimension_semantics=("parallel",),
            vmem_limit_bytes=_vmem_limit(2 * tm * k * 6 + 2 * tm * k * 4)),
        name="prenorm",
    )(x, g.reshape(1, k))


def _scaled_matmul_kernel(a_ref, ss_ref, w_ref, *rest, relu2, n_side):
    side_in, o_ref, side_out = rest[:n_side], rest[n_side], rest[n_side + 1:]
    acc = _dot(a_ref[...], w_ref[...]) * _row_rsqrt(ss_ref[...], a_ref.shape[1])
    if relu2:
        acc = jnp.square(jnp.maximum(acc, 0.0))
    o_ref[...] = acc.astype(o_ref.dtype)
    for src_ref, dst_ref in zip(side_in, side_out):
        dst_ref[...] = src_ref[...].astype(dst_ref.dtype)


def _scaled_matmul(a, ss, w, *, relu2, out_dtype, side_casts=(), tm=1024, tn=1024):
    m, k = a.shape
    n = w.shape[1]
    assert m % tm == 0 and n % tn == 0
    nj = n // tn
    out_bytes = jnp.dtype(out_dtype).itemsize
    side_in, side_out, side_shape, side_vmem = _side_cast_specs(
        side_casts, (m // tm) * nj, lambda i, j: i * nj + j)
    est = 2 * tm * k * 2 + 2 * k * tn * 2 + 2 * tm * tn * out_bytes + 2 * tm * tn * 4 + side_vmem
    return pl.pallas_call(
        functools.partial(_scaled_matmul_kernel, relu2=relu2, n_side=len(side_casts)),
        grid=(m // tm, nj),
        in_specs=[
            pl.BlockSpec((tm, k), lambda i, j: (i, 0)),
            pl.BlockSpec((tm, 1), lambda i, j: (i, 0)),
            pl.BlockSpec((k, tn), lambda i, j: (0, j)),
            *side_in,
        ],
        out_specs=[pl.BlockSpec((tm, tn), lambda i, j: (i, j)), *side_out],
        out_shape=[jax.ShapeDtypeStruct((m, n), out_dtype), *side_shape],
        compiler_params=pltpu.CompilerParams(
            dimension_semantics=("arbitrary", "arbitrary"),
            vmem_limit_bytes=_vmem_limit(est)),
        name="scaled_matmul_relu2" if relu2 else "scaled_matmul",
    )(a, ss, w, *[src for src, _ in side_casts])


def _matmul_res_kernel(a_ref, w_ref, res_ref, g_ref, o_ref, ss_ref, *rest, nk, emit_xg, res_by_dma):
    maybe_xg_ref = rest[0] if emit_xg else None
    i = pl.program_id(0)
    j = pl.program_id(1)
    kk = pl.program_id(2)
    tm, tn = o_ref.shape

    @pl.when(jnp.logical_and(j == 0, kk == 0))
    def _():
        ss_ref[...] = jnp.zeros_like(ss_ref)

    def emit_norm_inputs(rows, x_new):
        if maybe_xg_ref is not None:
            maybe_xg_ref[rows, :] = (x_new * g_ref[...]).astype(BF16)
        ss_ref[rows, :] += jnp.sum(x_new * x_new, axis=-1, keepdims=True)

    if nk == 1:
        x_new = res_ref[...] + _dot(a_ref[...], w_ref[...])
        o_ref[...] = x_new
        emit_norm_inputs(slice(None), x_new)
        return

    acc_ref = o_ref if res_by_dma else rest[-1]

    def res_copy():
        rows = pl.ds(pl.multiple_of(i * tm, tm), tm)
        cols = pl.ds(pl.multiple_of(j * tn, tn), tn)
        return pltpu.make_async_copy(res_ref.at[rows, cols], o_ref, rest[-1])

    @pl.when(kk == 0)
    def _():
        if res_by_dma:
            res_copy().start()
            part = _dot(a_ref[...], w_ref[...])
            res_copy().wait()
            acc_ref[...] += part
        else:
            acc_ref[...] = res_ref[...] + _dot(a_ref[...], w_ref[...])

    @pl.when(jnp.logical_and(kk > 0, kk < nk - 1))
    def _():
        acc_ref[...] += _dot(a_ref[...], w_ref[...])

    @pl.when(kk == nk - 1)
    def _():
        x_new = acc_ref[...] + _dot(a_ref[...], w_ref[...])
        o_ref[...] = x_new
        emit_norm_inputs(slice(None), x_new)


def _plan_matmul_res(tm, tn, k, tk_options, emit_xg):
    for tk in sorted(tk_options, reverse=True):
        for res_by_dma in (False, True):
            if res_by_dma and tk == k:
                continue
            est = (2 * tm * tk * 2 + 2 * tk * tn * 2 + 2 * tm * tn * (4 + 2 * emit_xg)
                   + (0 if res_by_dma else 2 * tm * tn * 4 + (tk < k) * tm * tn * 4)
                   + 5 * tm * tn * 2)
            if est + VMEM_PLAN_MARGIN <= V7X_VMEM_BUDGET:
                return tk, res_by_dma, est
    raise ValueError("no residual-matmul tiling fits VMEM")


def _matmul_res(a, w, res, g_next, *, tm, tn, tk_options, emit_xg=True):
    m, k = a.shape
    n = w.shape[1]
    tk, res_by_dma, est = _plan_matmul_res(tm, tn, k, tk_options, emit_xg)
    assert m % tm == 0 and n % tn == 0 and k % tk == 0
    nk, nj = k // tk, n // tn
    if res_by_dma:
        scratch = [pltpu.SemaphoreType.DMA(())]
    else:
        scratch = [pltpu.VMEM((tm, tn), F32)] if nk > 1 else []
    tile = pl.BlockSpec((tm, tn), lambda i, j, kk: (i, j))
    out_specs = [tile, pl.BlockSpec((tm, 1), lambda i, j, kk: (i, 0))]
    out_shape = [jax.ShapeDtypeStruct((m, n), F32), jax.ShapeDtypeStruct((m, 1), F32)]
    if emit_xg:
        out_specs.append(tile)
        out_shape.append(jax.ShapeDtypeStruct((m, n), BF16))
    return pl.pallas_call(
        functools.partial(_matmul_res_kernel, nk=nk, emit_xg=emit_xg, res_by_dma=res_by_dma),
        grid=(m // tm, nj, nk),
        in_specs=[
            pl.BlockSpec((tm, tk), lambda i, j, kk: (i, kk)),
            pl.BlockSpec((tk, tn), lambda i, j, kk: (kk, j)),
            pl.BlockSpec(memory_space=pl.ANY) if res_by_dma else tile,
            pl.BlockSpec((1, tn), lambda i, j, kk: (0, j)),
        ],
        out_specs=out_specs,
        out_shape=out_shape,
        scratch_shapes=scratch,
        compiler_params=pltpu.CompilerParams(
            dimension_semantics=("arbitrary", "arbitrary", "arbitrary"),
            vmem_limit_bytes=_vmem_limit(est)),
        name="matmul_res_k%d" % nk,
    )(a, w, res, g_next.reshape(1, n))


def _final_scale_kernel(x_ref, ss_ref, g_ref, o_ref):
    o_ref[...] = x_ref[...] * _row_rsqrt(ss_ref[...], x_ref.shape[1]) * g_ref[...]


def _final_scale(x, ss, g, *, tm=512):
    m, k = x.shape
    return pl.pallas_call(
        _final_scale_kernel,
        grid=(m // tm,),
        in_specs=[pl.BlockSpec((tm, k), lambda i: (i, 0)),
                  pl.BlockSpec((tm, 1), lambda i: (i, 0)),
                  pl.BlockSpec((1, k), lambda i: (0, 0))],
        out_specs=pl.BlockSpec((tm, k), lambda i: (i, 0)),
        out_shape=jax.ShapeDtypeStruct((m, k), F32),
        compiler_params=pltpu.CompilerParams(
            dimension_semantics=("parallel",),
            vmem_limit_bytes=_vmem_limit(5 * tm * k * 4)),
        name="final_scale",
    )(x, ss, g.reshape(1, k))


def _rope(t, cos, sgn_sin):
    return t * cos + pltpu.roll(t, HEAD_DIM // 2, 1) * sgn_sin


def _row_index_f32():
    return lax.broadcasted_iota(jnp.int32, (CHUNK, HEAD_DIM), 0).astype(F32)


def _head(ref, off, h):
    return ref[:, off + h * HEAD_DIM: off + (h + 1) * HEAD_DIM]


BWD_CHUNKS_PER_STEP = 4


def _ret_bwd_state_kernel(k_ref, v_ref, cos_ref, sin_ref, raw_ref, sb_ref, st_ref):
    @pl.when(pl.program_id(1) == 0)
    def _():
        st_ref[...] = jnp.zeros_like(st_ref)

    row = _row_index_f32()
    scale = HEAD_DIM ** -0.5
    for c in reversed(range(BWD_CHUNKS_PER_STEP)):
        pos = slice(c * CHUNK, (c + 1) * CHUNK)
        cos = cos_ref[pos, :]
        sin = sin_ref[pos, :]
        for h in range(N_RET_HEADS):
            rows = slice(h * HEAD_DIM, (h + 1) * HEAD_DIM)
            cols = slice(h * HEAD_DIM, (h + 1) * HEAD_DIM)
            st = st_ref[rows, :]
            sb_ref[0, c, rows, :] = st.astype(BF16)
            lgb = -jnp.exp(raw_ref[N_RET_HEADS + h: N_RET_HEADS + h + 1, :])
            kh = _rope(k_ref[pos, cols], cos, sin) * scale
            kw = (kh * jnp.exp(lgb * row)).astype(BF16)
            kv = _dot_tn(kw, v_ref[pos, cols].astype(BF16))
            st_ref[rows, :] = jnp.exp(lgb * float(CHUNK)) * st + kv


def _ret_bwd_states(proj, cos, sgn_sin, raw_rows, *, batch, n_chunks):
    assert OFF_RK % RET_WIDTH == 0 and OFF_RV % RET_WIDTH == 0
    kblk, vblk = OFF_RK // RET_WIDTH, OFF_RV // RET_WIDTH
    per = BWD_CHUNKS_PER_STEP
    assert n_chunks % per == 0
    n_steps = n_chunks // per
    last = n_steps - 1
    return pl.pallas_call(
        _ret_bwd_state_kernel,
        grid=(batch, n_steps),
        in_specs=[
            pl.BlockSpec((per * CHUNK, RET_WIDTH), lambda b, t: (b * n_steps + last - t, kblk)),
            pl.BlockSpec((per * CHUNK, RET_WIDTH), lambda b, t: (b * n_steps + last - t, vblk)),
            pl.BlockSpec((per * CHUNK, HEAD_DIM), lambda b, t: (last - t, 0)),
            pl.BlockSpec((per * CHUNK, HEAD_DIM), lambda b, t: (last - t, 0)),
            pl.BlockSpec((2 * N_RET_HEADS, HEAD_DIM), lambda b, t: (0, 0)),
        ],
        out_specs=pl.BlockSpec((1, per, RET_WIDTH, HEAD_DIM), lambda b, t: (b, last - t, 0, 0)),
        out_shape=jax.ShapeDtypeStruct((batch, n_chunks, RET_WIDTH, HEAD_DIM), BF16),
        scratch_shapes=[pltpu.VMEM((RET_WIDTH, HEAD_DIM), F32)],
        compiler_params=pltpu.CompilerParams(
            dimension_semantics=("parallel", "arbitrary"),
            vmem_limit_bytes=_vmem_limit(4 * per * CHUNK * RET_WIDTH * 4 + (6 << 20))),
        name="ret_bwd_states",
    )(proj, proj, cos, sgn_sin, raw_rows)


def _sgu(p_ref, lng_ref, lnb_ref, ws_ref, bs_ref, o_ref):
    sqrt_half = np.sqrt(0.5).astype(np.float32)

    def gelu(t):
        return 0.5 * t * (1.0 + lax.erf(t * sqrt_half))

    u = gelu(p_ref[:, OFF_U:OFF_U + SGU_WIDTH])
    v = gelu(p_ref[:, OFF_V:OFF_V + SGU_WIDTH])
    mu = jnp.mean(v, axis=-1, keepdims=True)
    var = jnp.mean(jnp.square(v - mu), axis=-1, keepdims=True)
    vn = (v - mu) * lax.rsqrt(var + NORM_EPS) * lng_ref[...] + lnb_ref[...]
    for g in range(N_SGU_GROUPS):
        cols = slice(g * HEAD_DIM, (g + 1) * HEAD_DIM)
        s = _dot(ws_ref[g], vn[:, cols].astype(BF16)) + bs_ref[g]
        o_ref[:, cols] = (u[:, cols] * s).astype(o_ref.dtype)


N_RET_TABLES = 4


def _ret_table(tab_ref, h, t):
    base = (h * N_RET_TABLES + t) * CHUNK
    return tab_ref.at[base:base + CHUNK, :]


def _fill_ret_tables(raw_ref, tab_ref):
    row = _row_index_f32()
    col = lax.broadcasted_iota(jnp.int32, (CHUNK, CHUNK), 1).astype(F32)
    delta = row - col
    for h in range(N_RET_HEADS):
        lgf = -jnp.exp(raw_ref[h:h + 1, :])
        lgb = -jnp.exp(raw_ref[N_RET_HEADS + h: N_RET_HEADS + h + 1, :])
        _ret_table(tab_ref, h, 0)[...] = jnp.where(
            delta >= 0.0,
            jnp.exp(lgf * jnp.maximum(delta, 0.0)),
            jnp.exp(lgb * jnp.maximum(-delta, 0.0)))
        _ret_table(tab_ref, h, 1)[...] = jnp.exp(lgf * (float(CHUNK - 1) - row))
        _ret_table(tab_ref, h, 2)[...] = jnp.exp(lgf * (row + 1.0))
        _ret_table(tab_ref, h, 3)[...] = jnp.exp(lgb * (float(CHUNK) - row))


def _retention(p_ref, sb_ref, raw_ref, cos, sin, o_ref, sf_ref, tab_ref):
    scale = HEAD_DIM ** -0.5
    for h in range(N_RET_HEADS):
        rows = slice(h * HEAD_DIM, (h + 1) * HEAD_DIM)
        lgf = -jnp.exp(raw_ref[h:h + 1, :])
        q = _rope(_head(p_ref, OFF_RQ, h), cos, sin)
        k = _rope(_head(p_ref, OFF_RK, h), cos, sin) * scale
        v = _head(p_ref, OFF_RV, h).astype(BF16)
        scores = _dot_nt(q.astype(BF16), k.astype(BF16)) * _ret_table(tab_ref, h, 0)[...]
        inner = _dot(scores.astype(BF16), v)
        sf_prev = sf_ref[rows, :]
        kv_f = _dot_tn((k * _ret_table(tab_ref, h, 1)[...]).astype(BF16), v)
        cross_f = _dot((q * _ret_table(tab_ref, h, 2)[...]).astype(BF16), sf_prev.astype(BF16))
        sf_ref[rows, :] = jnp.exp(lgf * float(CHUNK)) * sf_prev + kv_f
        cross_b = _dot((q * _ret_table(tab_ref, h, 3)[...]).astype(BF16), sb_ref[0, 0, rows, :])
        rf = (inner + cross_f) + cross_b
        rn = rf * lax.rsqrt(jnp.mean(rf * rf, axis=-1, keepdims=True) + NORM_EPS)
        gate = _head(p_ref, OFF_RG, h)
        o_ref[:, SGU_WIDTH + h * HEAD_DIM: SGU_WIDTH + (h + 1) * HEAD_DIM] = (
            gate * jax.nn.sigmoid(gate) * rn).astype(o_ref.dtype)


def _attention(p_ref, kvp_ref, kvn_ref, cos3_ref, sin3_ref, sink_ref, o_ref, chunk, seq):
    cos3 = cos3_ref[...]
    sin3 = sin3_ref[...]
    q_scale = HEAD_DIM ** -0.5 * LOG2_E
    cos_q = cos3[CHUNK:2 * CHUNK] * q_scale
    sin_q = sin3[CHUNK:2 * CHUNK] * q_scale
    kj = lax.broadcasted_iota(jnp.int32, (3 * CHUNK, CHUNK), 0)
    qi = lax.broadcasted_iota(jnp.int32, (3 * CHUNK, CHUNK), 1)
    rel = kj - CHUNK - qi
    kpos = (chunk - 1) * CHUNK + kj
    valid = (jnp.abs(rel) <= WINDOW) & (kpos >= 0) & (kpos < seq)
    valid = jnp.concatenate([valid] * Q_PER_KV, axis=1)
    out_off = SGU_WIDTH + RET_WIDTH
    for kh in range(N_KV_HEADS):
        kcols = slice(kh * HEAD_DIM, (kh + 1) * HEAD_DIM)
        vcols = slice(KV_WIDTH + kh * HEAD_DIM, KV_WIDTH + (kh + 1) * HEAD_DIM)
        k3 = jnp.concatenate([kvp_ref[:, kcols], _head(p_ref, OFF_AK, kh), kvn_ref[:, kcols]], axis=0)
        v3 = jnp.concatenate([kvp_ref[:, vcols], _head(p_ref, OFF_AV, kh), kvn_ref[:, vcols]], axis=0)
        k3 = _rope(k3, cos3, sin3).astype(BF16)
        qs = jnp.concatenate(
            [_rope(_head(p_ref, OFF_AQ, kh * Q_PER_KV + g), cos_q, sin_q) for g in range(Q_PER_KV)],
            axis=0)
        sink = LOG2_E * jnp.concatenate(
            [sink_ref[kh * Q_PER_KV + g: kh * Q_PER_KV + g + 1, :] for g in range(Q_PER_KV)], axis=1)
        s = _dot_nt(k3, qs.astype(BF16))
        s = jnp.concatenate([
            jnp.where(valid[:CHUNK], s[:CHUNK], NEG_INF),
            s[CHUNK:2 * CHUNK],
            jnp.where(valid[2 * CHUNK:], s[2 * CHUNK:], NEG_INF)], axis=0)
        m = jnp.maximum(jnp.max(s, axis=0, keepdims=True), sink)
        p = jnp.exp2(s - m)
        inv = 1.0 / (jnp.sum(p, axis=0, keepdims=True) + jnp.exp2(sink - m))
        o_t = _dot_tn(v3.astype(BF16), (p * inv).astype(BF16)).astype(o_ref.dtype)
        for g in range(Q_PER_KV):
            hq = kh * Q_PER_KV + g
            o_ref[:, out_off + hq * HEAD_DIM: out_off + (hq + 1) * HEAD_DIM] = (
                o_t[:, g * CHUNK:(g + 1) * CHUNK].T)


def _mixer_kernel(p_ref, kvp_ref, kvn_ref, cos3_ref, sin3_ref, sb_ref, lng_ref, lnb_ref,
                  ws_ref, bs_ref, raw_ref, sink_ref, cast_src_ref, o_ref, cast_dst_ref,
                  sf_ref, tab_ref, *, seq):
    chunk = pl.program_id(1)

    @pl.when(chunk == 0)
    def _():
        sf_ref[...] = jnp.zeros_like(sf_ref)
        _fill_ret_tables(raw_ref, tab_ref)

    cast_dst_ref[...] = cast_src_ref[...].astype(cast_dst_ref.dtype)
    cos = cos3_ref[CHUNK:2 * CHUNK, :]
    sin = sin3_ref[CHUNK:2 * CHUNK, :]
    _attention(p_ref, kvp_ref, kvn_ref, cos3_ref, sin3_ref, sink_ref, o_ref, chunk, seq)
    _retention(p_ref, sb_ref, raw_ref, cos, sin, o_ref, sf_ref, tab_ref)
    _sgu(p_ref, lng_ref, lnb_ref, ws_ref, bs_ref, o_ref)


def _mixers(proj, sb, cos3, sin3, lng, lnb, ws, bias, raw_rows, sink_rows, side_cast,
            *, batch, n_chunks):
    m = proj.shape[0]
    kv_blk = OFF_AK // (2 * KV_WIDTH)
    assert kv_blk * 2 * KV_WIDTH == OFF_AK
    last = n_chunks - 1
    const2 = lambda b, n: (0, 0)
    side_in, side_out, side_shape, side_vmem = _side_cast_specs(
        (side_cast,), batch * n_chunks, lambda b, n: b * n_chunks + n)
    est = (2 * CHUNK * IN_WIDTH * 4 + 4 * CHUNK * 2 * KV_WIDTH * 4 + 2 * CHUNK * MIX_WIDTH * 2
           + (12 << 20) + side_vmem)
    return pl.pallas_call(
        functools.partial(_mixer_kernel, seq=n_chunks * CHUNK),
        grid=(batch, n_chunks),
        in_specs=[
            pl.BlockSpec((CHUNK, IN_WIDTH), lambda b, n: (b * n_chunks + n, 0)),
            pl.BlockSpec((CHUNK, 2 * KV_WIDTH),
                         lambda b, n: (b * n_chunks + jnp.maximum(n - 1, 0), kv_blk)),
            pl.BlockSpec((CHUNK, 2 * KV_WIDTH),
                         lambda b, n: (b * n_chunks + jnp.minimum(n + 1, last), kv_blk)),
            pl.BlockSpec((3 * CHUNK, HEAD_DIM), lambda b, n: (n, 0)),
            pl.BlockSpec((3 * CHUNK, HEAD_DIM), lambda b, n: (n, 0)),
            pl.BlockSpec((1, 1, RET_WIDTH, HEAD_DIM), lambda b, n: (b, n, 0, 0)),
            pl.BlockSpec((1, SGU_WIDTH), const2),
            pl.BlockSpec((1, SGU_WIDTH), const2),
            pl.BlockSpec((N_SGU_GROUPS, CHUNK, CHUNK), lambda b, n: (0, 0, 0)),
            pl.BlockSpec((N_SGU_GROUPS, CHUNK, HEAD_DIM), lambda b, n: (0, 0, 0)),
            pl.BlockSpec((2 * N_RET_HEADS, HEAD_DIM), const2),
            pl.BlockSpec((N_Q_HEADS, HEAD_DIM), const2),
            *side_in,
        ],
        out_specs=[pl.BlockSpec((CHUNK, MIX_WIDTH), lambda b, n: (b * n_chunks + n, 0)), *side_out],
        out_shape=[jax.ShapeDtypeStruct((m, MIX_WIDTH), BF16), *side_shape],
        scratch_shapes=[pltpu.VMEM((RET_WIDTH, HEAD_DIM), F32),
                        pltpu.VMEM((N_RET_HEADS * N_RET_TABLES * CHUNK, HEAD_DIM), F32)],
        compiler_params=pltpu.CompilerParams(
            dimension_semantics=("arbitrary", "arbitrary"),
            vmem_limit_bytes=_vmem_limit(est)),
        name="mixers",
    )(proj, proj, proj, cos3, sin3, sb, lng, lnb, ws, bias, raw_rows, sink_rows, side_cast[0])


def _rope_tables(seq):
    pos = jnp.arange(seq, dtype=F32)
    inv = ROPE_THETA ** (-jnp.arange(0, HEAD_DIM, 2, dtype=F32) / HEAD_DIM)
    ang = pos[:, None] * inv[None, :]
    ang = jnp.concatenate([ang, ang], axis=-1)
    sgn = jnp.where(jnp.arange(HEAD_DIM) < HEAD_DIM // 2, -1.0, 1.0).astype(F32)
    return jnp.cos(ang), jnp.sin(ang) * sgn[None, :]


def _three_chunk_view(t, n_chunks):
    seq = t.shape[0]
    pad = jnp.zeros((CHUNK, HEAD_DIM), t.dtype)
    tp = jnp.concatenate([pad, t, pad], axis=0)
    parts = [tp[s * CHUNK: s * CHUNK + seq].reshape(n_chunks, CHUNK, HEAD_DIM) for s in range(3)]
    return jnp.concatenate(parts, axis=1).reshape(n_chunks * 3 * CHUNK, HEAD_DIM)


def kernel(x, ln_mix_g, w_in, sgu_ln_g, sgu_ln_b, sgu_w, sgu_b, ret_log_decay, attn_sink,
           w_out, ln_mlp_g, w_up, w_down, final_norm_g):
    batch, seq, d_model = x.shape
    depth = w_in.shape[0]
    assert w_in.shape[2] == IN_WIDTH and w_out.shape[1] == MIX_WIDTH and seq % CHUNK == 0
    n_chunks = seq // CHUNK
    m = batch * seq

    cos, sgn_sin = _rope_tables(seq)
    cos3 = _three_chunk_view(cos, n_chunks)
    sin3 = _three_chunk_view(sgn_sin, n_chunks)

    next_gains = [ln_mix_g[l] for l in range(1, depth)] + [final_norm_g]

    w_in_b = _cast_layer(w_in, 0)
    x2 = x.reshape(m, d_model)
    xg, ss = _prenorm(x2, ln_mix_g[0])
    for l in range(depth):
        raw_rows = jnp.broadcast_to(
            ret_log_decay[l].astype(F32).reshape(2 * N_RET_HEADS, 1), (2 * N_RET_HEADS, HEAD_DIM))
        sink_rows = jnp.broadcast_to(
            attn_sink[l].astype(F32).reshape(N_Q_HEADS, 1), (N_Q_HEADS, HEAD_DIM))
        sgu_bias = jnp.broadcast_to(
            sgu_b[l].astype(F32)[:, :, None], (N_SGU_GROUPS, CHUNK, HEAD_DIM))
        proj, w_out_b = _scaled_matmul(
            xg, ss, w_in_b, relu2=False, out_dtype=F32, side_casts=((w_out, l),))
        sb = _ret_bwd_states(proj, cos, sgn_sin, raw_rows, batch=batch, n_chunks=n_chunks)
        mix, w_up_b = _mixers(proj, sb, cos3, sin3,
                              sgu_ln_g[l].reshape(1, SGU_WIDTH), sgu_ln_b[l].reshape(1, SGU_WIDTH),
                              sgu_w[l].astype(BF16), sgu_bias, raw_rows, sink_rows, (w_up, l),
                              batch=batch, n_chunks=n_chunks)
        x2, ss, xg = _matmul_res(mix, w_out_b, x2, ln_mlp_g[l], tm=1024, tn=512,
                                 tk_options=(MIX_WIDTH,))
        last = l + 1 == depth
        hid, w_down_b, *maybe_w_in = _scaled_matmul(
            xg, ss, w_up_b, relu2=True, out_dtype=BF16,
            side_casts=((w_down, l),) if last else ((w_down, l), (w_in, l + 1)))
        w_in_b = None if last else maybe_w_in[0]
        x2, ss, *maybe_xg = _matmul_res(hid, w_down_b, x2, next_gains[l], tm=1024, tn=1024,
                                       tk_options=(2048, 4096), emit_xg=not last)
        xg = None if last else maybe_xg[0]
    return _final_scale(x2, ss, final_norm_g).reshape(batch, seq, d_model)
```

```python
import functools

import numpy as np
import jax
import jax.numpy as jnp
from jax import lax
from jax.experimental import pallas as pl
from jax.experimental.pallas import tpu as pltpu

HEAD_DIM = 128
CHUNK = 128
WINDOW = 128
ROPE_THETA = 10000.0
NORM_EPS = 1e-5
NEG_INF = -1e30
LOG2_E = float(np.log2(np.e))
N_SGU_GROUPS = 8
N_RET_HEADS = 8
N_Q_HEADS = 16
N_KV_HEADS = 4
Q_PER_KV = N_Q_HEADS // N_KV_HEADS

SGU_WIDTH = N_SGU_GROUPS * HEAD_DIM
RET_WIDTH = N_RET_HEADS * HEAD_DIM
ATT_WIDTH = N_Q_HEADS * HEAD_DIM
KV_WIDTH = N_KV_HEADS * HEAD_DIM
OFF_U = 0
OFF_V = OFF_U + SGU_WIDTH
OFF_RQ = OFF_V + SGU_WIDTH
OFF_RK = OFF_RQ + RET_WIDTH
OFF_RV = OFF_RK + RET_WIDTH
OFF_RG = OFF_RV + RET_WIDTH
OFF_AQ = OFF_RG + RET_WIDTH
OFF_AK = OFF_AQ + ATT_WIDTH
OFF_AV = OFF_AK + KV_WIDTH
IN_WIDTH = OFF_AV + KV_WIDTH
MIX_WIDTH = SGU_WIDTH + RET_WIDTH + ATT_WIDTH

V7X_VMEM_BUDGET = 56 * 1024 * 1024
VMEM_PLAN_MARGIN = 4 * 1024 * 1024
CAST_SLAB_BYTES = 4 * 1024 * 1024
BF16_SUBLANES = 16

F32 = jnp.float32
BF16 = jnp.bfloat16


def _vmem_limit(est_bytes):
    assert est_bytes <= V7X_VMEM_BUDGET, est_bytes
    return min(max(int(est_bytes * 1.25), 16 << 20), V7X_VMEM_BUDGET)


def _dot(a, b):
    return jnp.dot(a, b, preferred_element_type=F32)


def _dot_nt(a, b):
    return lax.dot_general(a, b, (((1,), (1,)), ((), ())), preferred_element_type=F32)


def _dot_tn(a, b):
    return lax.dot_general(a, b, (((0,), (0,)), ((), ())), preferred_element_type=F32)


def _cast_kernel(w_ref, o_ref):
    o_ref[...] = w_ref[...].astype(o_ref.dtype)


def _cast_layer(w, layer):
    _, k, n = w.shape
    tr = 1
    while tr * 2 * n * 4 <= CAST_SLAB_BYTES and k % (tr * 2) == 0:
        tr *= 2
    return pl.pallas_call(
        _cast_kernel,
        grid=(k // tr,),
        in_specs=[pl.BlockSpec((None, tr, n), lambda i: (layer, i, 0))],
        out_specs=pl.BlockSpec((tr, n), lambda i: (i, 0)),
        out_shape=jax.ShapeDtypeStruct((k, n), BF16),
        compiler_params=pltpu.CompilerParams(
            dimension_semantics=("parallel",),
            vmem_limit_bytes=_vmem_limit(2 * tr * n * 6 + tr * n * 4)),
        name="cast_bf16",
    )(w)


def _side_cast_specs(side_casts, steps, step_index):
    in_specs, out_specs, out_shape, vmem = [], [], [], 0
    for w, layer in side_casts:
        _, rows, cols = w.shape
        n_slabs = 1
        while n_slabs * 2 <= steps and rows % (n_slabs * 2) == 0 and (rows // (n_slabs * 2)) % BF16_SUBLANES == 0:
            n_slabs *= 2
        slab = rows // n_slabs

        def slab_index(*ids, _last=n_slabs - 1):
            return jnp.minimum(step_index(*ids), _last)

        in_specs.append(pl.BlockSpec(
            (None, slab, cols), lambda *ids, _l=layer, _s=slab_index: (_l, _s(*ids), 0)))
        out_specs.append(pl.BlockSpec((slab, cols), lambda *ids, _s=slab_index: (_s(*ids), 0)))
        out_shape.append(jax.ShapeDtypeStruct((rows, cols), BF16))
        vmem += 2 * slab * cols * 6 + slab * cols * 4
    return in_specs, out_specs, out_shape, vmem


def _row_rsqrt(ss, width):
    return lax.rsqrt(ss / float(width) + NORM_EPS)


def _prenorm_kernel(x_ref, g_ref, xg_ref, ss_ref):
    x = x_ref[...]
    xg_ref[...] = (x * g_ref[...]).astype(BF16)
    ss_ref[...] = jnp.sum(x * x, axis=-1, keepdims=True)


def _prenorm(x, g, *, tm=512):
    m, k = x.shape
    return pl.pallas_call(
        _prenorm_kernel,
        grid=(m // tm,),
        in_specs=[pl.BlockSpec((tm, k), lambda i: (i, 0)),
                  pl.BlockSpec((1, k), lambda i: (0, 0))],
        out_specs=[pl.BlockSpec((tm, k), lambda i: (i, 0)),
                   pl.BlockSpec((tm, 1), lambda i: (i, 0))],
        out_shape=[jax.ShapeDtypeStruct((m, k), BF16), jax.ShapeDtypeStruct((m, 1), F32)],
        compiler_params=pltpu.CompilerParams(
            dimension_semantics=("parallel",),
            vmem_limit_bytes=_vmem_limit(2 * tm * k * 6 + 2 * tm * k * 4)),
        name="prenorm",
    )(x, g.reshape(1, k))


def _scaled_matmul_kernel(a_ref, ss_ref, w_ref, *rest, relu2, n_side):
    side_in, o_ref, side_out = rest[:n_side], rest[n_side], rest[n_side + 1:]
    acc = _dot(a_ref[...], w_ref[...]) * _row_rsqrt(ss_ref[...], a_ref.shape[1])
    if relu2:
        acc = jnp.square(jnp.maximum(acc, 0.0))
    o_ref[...] = acc.astype(o_ref.dtype)
    for src_ref, dst_ref in zip(side_in, side_out):
        dst_ref[...] = src_ref[...].astype(dst_ref.dtype)


def _scaled_matmul(a, ss, w, *, relu2, out_dtype, side_casts=(), tm=1024, tn=1024):
    m, k = a.shape
    n = w.shape[1]
    assert m % tm == 0 and n % tn == 0
    nj = n // tn
    out_bytes = jnp.dtype(out_dtype).itemsize
    side_in, side_out, side_shape, side_vmem = _side_cast_specs(
        side_casts, (m // tm) * nj, lambda i, j: i * nj + j)
    est = 2 * tm * k * 2 + 2 * k * tn * 2 + 2 * tm * tn * out_bytes + 2 * tm * tn * 4 + side_vmem
    return pl.pallas_call(
        functools.partial(_scaled_matmul_kernel, relu2=relu2, n_side=len(side_casts)),
        grid=(m // tm, nj),
        in_specs=[
            pl.BlockSpec((tm, k), lambda i, j: (i, 0)),
            pl.BlockSpec((tm, 1), lambda i, j: (i, 0)),
            pl.BlockSpec((k, tn), lambda i, j: (0, j)),
            *side_in,
        ],
        out_specs=[pl.BlockSpec((tm, tn), lambda i, j: (i, j)), *side_out],
        out_shape=[jax.ShapeDtypeStruct((m, n), out_dtype), *side_shape],
        compiler_params=pltpu.CompilerParams(
            dimension_semantics=("arbitrary", "arbitrary"),
            vmem_limit_bytes=_vmem_limit(est)),
        name="scaled_matmul_relu2" if relu2 else "scaled_matmul",
    )(a, ss, w, *[src for src, _ in side_casts])


def _matmul_res_kernel(a_ref, w_ref, res_ref, g_ref, o_ref, ss_ref, *rest, nk, emit_xg, res_by_dma):
    maybe_xg_ref = rest[0] if emit_xg else None
    i = pl.program_id(0)
    j = pl.program_id(1)
    kk = pl.program_id(2)
    tm, tn = o_ref.shape

    @pl.when(jnp.logical_and(j == 0, kk == 0))
    def _():
        ss_ref[...] = jnp.zeros_like(ss_ref)

    def emit_norm_inputs(rows, x_new):
        if maybe_xg_ref is not None:
            maybe_xg_ref[rows, :] = (x_new * g_ref[...]).astype(BF16)
        ss_ref[rows, :] += jnp.sum(x_new * x_new, axis=-1, keepdims=True)

    if nk == 1:
        x_new = res_ref[...] + _dot(a_ref[...], w_ref[...])
        o_ref[...] = x_new
        emit_norm_inputs(slice(None), x_new)
        return

    acc_ref = o_ref if res_by_dma else rest[-1]

    def res_copy():
        rows = pl.ds(pl.multiple_of(i * tm, tm), tm)
        cols = pl.ds(pl.multiple_of(j * tn, tn), tn)
        return pltpu.make_async_copy(res_ref.at[rows, cols], o_ref, rest[-1])

    @pl.when(kk == 0)
    def _():
        if res_by_dma:
            res_copy().start()
            part = _dot(a_ref[...], w_ref[...])
            res_copy().wait()
            acc_ref[...] += part
        else:
            acc_ref[...] = res_ref[...] + _dot(a_ref[...], w_ref[...])

    @pl.when(jnp.logical_and(kk > 0, kk < nk - 1))
    def _():
        acc_ref[...] += _dot(a_ref[...], w_ref[...])

    @pl.when(kk == nk - 1)
    def _():
        x_new = acc_ref[...] + _dot(a_ref[...], w_ref[...])
        o_ref[...] = x_new
        emit_norm_inputs(slice(None), x_new)


def _plan_matmul_res(tm, tn, k, tk_options, emit_xg):
    for tk in sorted(tk_options, reverse=True):
        for res_by_dma in (False, True):
            if res_by_dma and tk == k:
                continue
            est = (2 * tm * tk * 2 + 2 * tk * tn * 2 + 2 * tm * tn * (4 + 2 * emit_xg)
                   + (0 if res_by_dma else 2 * tm * tn * 4 + (tk < k) * tm * tn * 4)
                   + 5 * tm * tn * 2)
            if est + VMEM_PLAN_MARGIN <= V7X_VMEM_BUDGET:
                return tk, res_by_dma, est
    raise ValueError("no residual-matmul tiling fits VMEM")


def _matmul_res(a, w, res, g_next, *, tm, tn, tk_options, emit_xg=True):
    m, k = a.shape
    n = w.shape[1]
    tk, res_by_dma, est = _plan_matmul_res(tm, tn, k, tk_options, emit_xg)
    assert m % tm == 0 and n % tn == 0 and k % tk == 0
    nk, nj = k // tk, n // tn
    if res_by_dma:
        scratch = [pltpu.SemaphoreType.DMA(())]
    else:
        scratch = [pltpu.VMEM((tm, tn), F32)] if nk > 1 else []
    tile = pl.BlockSpec((tm, tn), lambda i, j, kk: (i, j))
    out_specs = [tile, pl.BlockSpec((tm, 1), lambda i, j, kk: (i, 0))]
    out_shape = [jax.ShapeDtypeStruct((m, n), F32), jax.ShapeDtypeStruct((m, 1), F32)]
    if emit_xg:
        out_specs.append(tile)
        out_shape.append(jax.ShapeDtypeStruct((m, n), BF16))
    return pl.pallas_call(
        functools.partial(_matmul_res_kernel, nk=nk, emit_xg=emit_xg, res_by_dma=res_by_dma),
        grid=(m // tm, nj, nk),
        in_specs=[
            pl.BlockSpec((tm, tk), lambda i, j, kk: (i, kk)),
            pl.BlockSpec((tk, tn), lambda i, j, kk: (kk, j)),
            pl.BlockSpec(memory_space=pl.ANY) if res_by_dma else tile,
            pl.BlockSpec((1, tn), lambda i, j, kk: (0, j)),
        ],
        out_specs=out_specs,
        out_shape=out_shape,
        scratch_shapes=scratch,
        compiler_params=pltpu.CompilerParams(
            dimension_semantics=("arbitrary", "arbitrary", "arbitrary"),
            vmem_limit_bytes=_vmem_limit(est)),
        name="matmul_res_k%d" % nk,
    )(a, w, res, g_next.reshape(1, n))


def _final_scale_kernel(x_ref, ss_ref, g_ref, o_ref):
    o_ref[...] = x_ref[...] * _row_rsqrt(ss_ref[...], x_ref.shape[1]) * g_ref[...]


def _final_scale(x, ss, g, *, tm=512):
    m, k = x.shape
    return pl.pallas_call(
        _final_scale_kernel,
        grid=(m // tm,),
        in_specs=[pl.BlockSpec((tm, k), lambda i: (i, 0)),
                  pl.BlockSpec((tm, 1), lambda i: (i, 0)),
                  pl.BlockSpec((1, k), lambda i: (0, 0))],
        out_specs=pl.BlockSpec((tm, k), lambda i: (i, 0)),
        out_shape=jax.ShapeDtypeStruct((m, k), F32),
        compiler_params=pltpu.CompilerParams(
            dimension_semantics=("parallel",),
            vmem_limit_bytes=_vmem_limit(5 * tm * k * 4)),
        name="final_scale",
    )(x, ss, g.reshape(1, k))


def _rope(t, cos, sgn_sin):
    return t * cos + pltpu.roll(t, HEAD_DIM // 2, 1) * sgn_sin


def _row_index_f32():
    return lax.broadcasted_iota(jnp.int32, (CHUNK, HEAD_DIM), 0).astype(F32)


def _head(ref, off, h):
    return ref[:, off + h * HEAD_DIM: off + (h + 1) * HEAD_DIM]


BWD_CHUNKS_PER_STEP = 8


def _ret_bwd_state_kernel(k_ref, v_ref, cos_ref, sin_ref, raw_ref, sb_ref, st_ref):
    @pl.when(pl.program_id(1) == 0)
    def _():
        st_ref[...] = jnp.zeros_like(st_ref)

    row = _row_index_f32()
    scale = HEAD_DIM ** -0.5
    for c in reversed(range(BWD_CHUNKS_PER_STEP)):
        pos = slice(c * CHUNK, (c + 1) * CHUNK)
        cos = cos_ref[pos, :]
        sin = sin_ref[pos, :]
        for h in range(N_RET_HEADS):
            rows = slice(h * HEAD_DIM, (h + 1) * HEAD_DIM)
            cols = slice(h * HEAD_DIM, (h + 1) * HEAD_DIM)
            st = st_ref[rows, :]
            sb_ref[0, c, rows, :] = st.astype(BF16)
            lgb = -jnp.exp(raw_ref[N_RET_HEADS + h: N_RET_HEADS + h + 1, :])
            kh = _rope(k_ref[pos, cols], cos, sin) * scale
            kw = (kh * jnp.exp(lgb * row)).astype(BF16)
            kv = _dot_tn(kw, v_ref[pos, cols].astype(BF16))
            st_ref[rows, :] = jnp.exp(lgb * float(CHUNK)) * st + kv


def _ret_bwd_states(proj, cos, sgn_sin, raw_rows, *, batch, n_chunks):
    assert OFF_RK % RET_WIDTH == 0 and OFF_RV % RET_WIDTH == 0
    kblk, vblk = OFF_RK // RET_WIDTH, OFF_RV // RET_WIDTH
    per = BWD_CHUNKS_PER_STEP
    assert n_chunks % per == 0
    n_steps = n_chunks // per
    last = n_steps - 1
    return pl.pallas_call(
        _ret_bwd_state_kernel,
        grid=(batch, n_steps),
        in_specs=[
            pl.BlockSpec((per * CHUNK, RET_WIDTH), lambda b, t: (b * n_steps + last - t, kblk)),
            pl.BlockSpec((per * CHUNK, RET_WIDTH), lambda b, t: (b * n_steps + last - t, vblk)),
            pl.BlockSpec((per * CHUNK, HEAD_DIM), lambda b, t: (last - t, 0)),
            pl.BlockSpec((per * CHUNK, HEAD_DIM), lambda b, t: (last - t, 0)),
            pl.BlockSpec((2 * N_RET_HEADS, HEAD_DIM), lambda b, t: (0, 0)),
        ],
        out_specs=pl.BlockSpec((1, per, RET_WIDTH, HEAD_DIM), lambda b, t: (b, last - t, 0, 0)),
        out_shape=jax.ShapeDtypeStruct((batch, n_chunks, RET_WIDTH, HEAD_DIM), BF16),
        scratch_shapes=[pltpu.VMEM((RET_WIDTH, HEAD_DIM), F32)],
        compiler_params=pltpu.CompilerParams(
            dimension_semantics=("parallel", "arbitrary"),
            vmem_limit_bytes=_vmem_limit(4 * per * CHUNK * RET_WIDTH * 4 + (6 << 20))),
        name="ret_bwd_states",
    )(proj, proj, cos, sgn_sin, raw_rows)


def _sgu(p_ref, lng_ref, lnb_ref, ws_ref, bs_ref, o_ref):
    sqrt_half = np.sqrt(0.5).astype(np.float32)

    def gelu(t):
        return 0.5 * t * (1.0 + lax.erf(t * sqrt_half))

    u = gelu(p_ref[:, OFF_U:OFF_U + SGU_WIDTH])
    v = gelu(p_ref[:, OFF_V:OFF_V + SGU_WIDTH])
    mu = jnp.mean(v, axis=-1, keepdims=True)
    var = jnp.mean(jnp.square(v - mu), axis=-1, keepdims=True)
    vn = (v - mu) * lax.rsqrt(var + NORM_EPS) * lng_ref[...] + lnb_ref[...]
    for g in range(N_SGU_GROUPS):
        cols = slice(g * HEAD_DIM, (g + 1) * HEAD_DIM)
        s = _dot(ws_ref[g], vn[:, cols].astype(BF16)) + bs_ref[g]
        o_ref[:, cols] = (u[:, cols] * s).astype(o_ref.dtype)


N_RET_TABLES = 4


def _ret_table(tab_ref, h, t):
    base = (h * N_RET_TABLES + t) * CHUNK
    return tab_ref.at[base:base + CHUNK, :]


def _fill_ret_tables(raw_ref, tab_ref):
    row = _row_index_f32()
    col = lax.broadcasted_iota(jnp.int32, (CHUNK, CHUNK), 1).astype(F32)
    delta = row - col
    for h in range(N_RET_HEADS):
        lgf = -jnp.exp(raw_ref[h:h + 1, :])
        lgb = -jnp.exp(raw_ref[N_RET_HEADS + h: N_RET_HEADS + h + 1, :])
        _ret_table(tab_ref, h, 0)[...] = jnp.where(
            delta >= 0.0,
            jnp.exp(lgf * jnp.maximum(delta, 0.0)),
            jnp.exp(lgb * jnp.maximum(-delta, 0.0)))
        _ret_table(tab_ref, h, 1)[...] = jnp.exp(lgf * (float(CHUNK - 1) - row))
        _ret_table(tab_ref, h, 2)[...] = jnp.exp(lgf * (row + 1.0))
        _ret_table(tab_ref, h, 3)[...] = jnp.exp(lgb * (float(CHUNK) - row))


def _retention(p_ref, sb_ref, raw_ref, cos, sin, o_ref, sf_ref, tab_ref):
    scale = HEAD_DIM ** -0.5
    for h in range(N_RET_HEADS):
        rows = slice(h * HEAD_DIM, (h + 1) * HEAD_DIM)
        lgf = -jnp.exp(raw_ref[h:h + 1, :])
        q = _rope(_head(p_ref, OFF_RQ, h), cos, sin)
        k = _rope(_head(p_ref, OFF_RK, h), cos, sin) * scale
        v = _head(p_ref, OFF_RV, h).astype(BF16)
        scores = _dot_nt(q.astype(BF16), k.astype(BF16)) * _ret_table(tab_ref, h, 0)[...]
        inner = _dot(scores.astype(BF16), v)
        sf_prev = sf_ref[rows, :]
        kv_f = _dot_tn((k * _ret_table(tab_ref, h, 1)[...]).astype(BF16), v)
        cross_f = _dot((q * _ret_table(tab_ref, h, 2)[...]).astype(BF16), sf_prev.astype(BF16))
        sf_ref[rows, :] = jnp.exp(lgf * float(CHUNK)) * sf_prev + kv_f
        cross_b = _dot((q * _ret_table(tab_ref, h, 3)[...]).astype(BF16), sb_ref[0, 0, rows, :])
        rf = (inner + cross_f) + cross_b
        rn = rf * lax.rsqrt(jnp.mean(rf * rf, axis=-1, keepdims=True) + NORM_EPS)
        gate = _head(p_ref, OFF_RG, h)
        o_ref[:, SGU_WIDTH + h * HEAD_DIM: SGU_WIDTH + (h + 1) * HEAD_DIM] = (
            gate * jax.nn.sigmoid(gate) * rn).astype(o_ref.dtype)


def _attention(p_ref, kvp_ref, kvn_ref, cos3_ref, sin3_ref, sink_ref, o_ref, chunk, seq):
    cos3 = cos3_ref[...]
    sin3 = sin3_ref[...]
    q_scale = HEAD_DIM ** -0.5 * LOG2_E
    cos_q = cos3[CHUNK:2 * CHUNK] * q_scale
    sin_q = sin3[CHUNK:2 * CHUNK] * q_scale
    kj = lax.broadcasted_iota(jnp.int32, (3 * CHUNK, CHUNK), 0)
    qi = lax.broadcasted_iota(jnp.int32, (3 * CHUNK, CHUNK), 1)
    rel = kj - CHUNK - qi
    kpos = (chunk - 1) * CHUNK + kj
    valid = (jnp.abs(rel) <= WINDOW) & (kpos >= 0) & (kpos < seq)
    valid = jnp.concatenate([valid] * Q_PER_KV, axis=1)
    out_off = SGU_WIDTH + RET_WIDTH
    for kh in range(N_KV_HEADS):
        kcols = slice(kh * HEAD_DIM, (kh + 1) * HEAD_DIM)
        vcols = slice(KV_WIDTH + kh * HEAD_DIM, KV_WIDTH + (kh + 1) * HEAD_DIM)
        k3 = jnp.concatenate([kvp_ref[:, kcols], _head(p_ref, OFF_AK, kh), kvn_ref[:, kcols]], axis=0)
        v3 = jnp.concatenate([kvp_ref[:, vcols], _head(p_ref, OFF_AV, kh), kvn_ref[:, vcols]], axis=0)
        k3 = _rope(k3, cos3, sin3).astype(BF16)
        qs = jnp.concatenate(
            [_rope(_head(p_ref, OFF_AQ, kh * Q_PER_KV + g), cos_q, sin_q) for g in range(Q_PER_KV)],
            axis=0)
        sink = LOG2_E * jnp.concatenate(
            [sink_ref[kh * Q_PER_KV + g: kh * Q_PER_KV + g + 1, :] for g in range(Q_PER_KV)], axis=1)
        s = _dot_nt(k3, qs.astype(BF16))
        s = jnp.concatenate([
            jnp.where(valid[:CHUNK], s[:CHUNK], NEG_INF),
            s[CHUNK:2 * CHUNK],
            jnp.where(valid[2 * CHUNK:], s[2 * CHUNK:], NEG_INF)], axis=0)
        m = jnp.maximum(jnp.max(s, axis=0, keepdims=True), sink)
        p = jnp.exp2(s - m)
        inv = 1.0 / (jnp.sum(p, axis=0, keepdims=True) + jnp.exp2(sink - m))
        o_t = _dot_tn(v3.astype(BF16), (p * inv).astype(BF16)).astype(o_ref.dtype)
        for g in range(Q_PER_KV):
            hq = kh * Q_PER_KV + g
            o_ref[:, out_off + hq * HEAD_DIM: out_off + (hq + 1) * HEAD_DIM] = (
                o_t[:, g * CHUNK:(g + 1) * CHUNK].T)


def _mixer_kernel(p_ref, kvp_ref, kvn_ref, cos3_ref, sin3_ref, sb_ref, lng_ref, lnb_ref,
                  ws_ref, bs_ref, raw_ref, sink_ref, cast_src_ref, o_ref, cast_dst_ref,
                  sf_ref, tab_ref, *, seq):
    chunk = pl.program_id(1)

    @pl.when(chunk == 0)
    def _():
        sf_ref[...] = jnp.zeros_like(sf_ref)
        _fill_ret_tables(raw_ref, tab_ref)

    cast_dst_ref[...] = cast_src_ref[...].astype(cast_dst_ref.dtype)
    cos = cos3_ref[CHUNK:2 * CHUNK, :]
    sin = sin3_ref[CHUNK:2 * CHUNK, :]
    _attention(p_ref, kvp_ref, kvn_ref, cos3_ref, sin3_ref, sink_ref, o_ref, chunk, seq)
    _retention(p_ref, sb_ref, raw_ref, cos, sin, o_ref, sf_ref, tab_ref)
    _sgu(p_ref, lng_ref, lnb_ref, ws_ref, bs_ref, o_ref)


def _mixers(proj, sb, cos3, sin3, lng, lnb, ws, bias, raw_rows, sink_rows, side_cast,
            *, batch, n_chunks):
    m = proj.shape[0]
    kv_blk = OFF_AK // (2 * KV_WIDTH)
    assert kv_blk * 2 * KV_WIDTH == OFF_AK
    last = n_chunks - 1
    const2 = lambda b, n: (0, 0)
    side_in, side_out, side_shape, side_vmem = _side_cast_specs(
        (side_cast,), batch * n_chunks, lambda b, n: b * n_chunks + n)
    est = (2 * CHUNK * IN_WIDTH * 4 + 4 * CHUNK * 2 * KV_WIDTH * 4 + 2 * CHUNK * MIX_WIDTH * 2
           + (12 << 20) + side_vmem)
    return pl.pallas_call(
        functools.partial(_mixer_kernel, seq=n_chunks * CHUNK),
        grid=(batch, n_chunks),
        in_specs=[
            pl.BlockSpec((CHUNK, IN_WIDTH), lambda b, n: (b * n_chunks + n, 0)),
            pl.BlockSpec((CHUNK, 2 * KV_WIDTH),
                         lambda b, n: (b * n_chunks + jnp.maximum(n - 1, 0), kv_blk)),
            pl.BlockSpec((CHUNK, 2 * KV_WIDTH),
                         lambda b, n: (b * n_chunks + jnp.minimum(n + 1, last), kv_blk)),
            pl.BlockSpec((3 * CHUNK, HEAD_DIM), lambda b, n: (n, 0)),
            pl.BlockSpec((3 * CHUNK, HEAD_DIM), lambda b, n: (n, 0)),
            pl.BlockSpec((1, 1, RET_WIDTH, HEAD_DIM), lambda b, n: (b, n, 0, 0)),
            pl.BlockSpec((1, SGU_WIDTH), const2),
            pl.BlockSpec((1, SGU_WIDTH), const2),
            pl.BlockSpec((N_SGU_GROUPS, CHUNK, CHUNK), lambda b, n: (0, 0, 0)),
            pl.BlockSpec((N_SGU_GROUPS, CHUNK, HEAD_DIM), lambda b, n: (0, 0, 0)),
            pl.BlockSpec((2 * N_RET_HEADS, HEAD_DIM), const2),
            pl.BlockSpec((N_Q_HEADS, HEAD_DIM), const2),
            *side_in,
        ],
        out_specs=[pl.BlockSpec((CHUNK, MIX_WIDTH), lambda b, n: (b * n_chunks + n, 0)), *side_out],
        out_shape=[jax.ShapeDtypeStruct((m, MIX_WIDTH), BF16), *side_shape],
        scratch_shapes=[pltpu.VMEM((RET_WIDTH, HEAD_DIM), F32),
                        pltpu.VMEM((N_RET_HEADS * N_RET_TABLES * CHUNK, HEAD_DIM), F32)],
        compiler_params=pltpu.CompilerParams(
            dimension_semantics=("arbitrary", "arbitrary"),
            vmem_limit_bytes=_vmem_limit(est)),
        name="mixers",
    )(proj, proj, proj, cos3, sin3, sb, lng, lnb, ws, bias, raw_rows, sink_rows, side_cast[0])


def _rope_tables(seq):
    pos = jnp.arange(seq, dtype=F32)
    inv = ROPE_THETA ** (-jnp.arange(0, HEAD_DIM, 2, dtype=F32) / HEAD_DIM)
    ang = pos[:, None] * inv[None, :]
    ang = jnp.concatenate([ang, ang], axis=-1)
    sgn = jnp.where(jnp.arange(HEAD_DIM) < HEAD_DIM // 2, -1.0, 1.0).astype(F32)
    return jnp.cos(ang), jnp.sin(ang) * sgn[None, :]


def _three_chunk_view(t, n_chunks):
    seq = t.shape[0]
    pad = jnp.zeros((CHUNK, HEAD_DIM), t.dtype)
    tp = jnp.concatenate([pad, t, pad], axis=0)
    parts = [tp[s * CHUNK: s * CHUNK + seq].reshape(n_chunks, CHUNK, HEAD_DIM) for s in range(3)]
    return jnp.concatenate(parts, axis=1).reshape(n_chunks * 3 * CHUNK, HEAD_DIM)


def kernel(x, ln_mix_g, w_in, sgu_ln_g, sgu_ln_b, sgu_w, sgu_b, ret_log_decay, attn_sink,
           w_out, ln_mlp_g, w_up, w_down, final_norm_g):
    batch, seq, d_model = x.shape
    depth = w_in.shape[0]
    assert w_in.shape[2] == IN_WIDTH and w_out.shape[1] == MIX_WIDTH and seq % CHUNK == 0
    n_chunks = seq // CHUNK
    m = batch * seq

    cos, sgn_sin = _rope_tables(seq)
    cos3 = _three_chunk_view(cos, n_chunks)
    sin3 = _three_chunk_view(sgn_sin, n_chunks)

    next_gains = [ln_mix_g[l] for l in range(1, depth)] + [final_norm_g]

    w_in_b = _cast_layer(w_in, 0)
    x2 = x.reshape(m, d_model)
    xg, ss = _prenorm(x2, ln_mix_g[0])
    for l in range(depth):
        raw_rows = jnp.broadcast_to(
            ret_log_decay[l].astype(F32).reshape(2 * N_RET_HEADS, 1), (2 * N_RET_HEADS, HEAD_DIM))
        sink_rows = jnp.broadcast_to(
            attn_sink[l].astype(F32).reshape(N_Q_HEADS, 1), (N_Q_HEADS, HEAD_DIM))
        sgu_bias = jnp.broadcast_to(
            sgu_b[l].astype(F32)[:, :, None], (N_SGU_GROUPS, CHUNK, HEAD_DIM))
        proj, w_out_b = _scaled_matmul(
            xg, ss, w_in_b, relu2=False, out_dtype=F32, side_casts=((w_out, l),))
        sb = _ret_bwd_states(proj, cos, sgn_sin, raw_rows, batch=batch, n_chunks=n_chunks)
        mix, w_up_b = _mixers(proj, sb, cos3, sin3,
                              sgu_ln_g[l].reshape(1, SGU_WIDTH), sgu_ln_b[l].reshape(1, SGU_WIDTH),
                              sgu_w[l].astype(BF16), sgu_bias, raw_rows, sink_rows, (w_up, l),
                              batch=batch, n_chunks=n_chunks)
        x2, ss, xg = _matmul_res(mix, w_out_b, x2, ln_mlp_g[l], tm=1024, tn=512,
                                 tk_options=(MIX_WIDTH,))
        last = l + 1 == depth
        hid, w_down_b, *maybe_w_in = _scaled_matmul(
            xg, ss, w_up_b, relu2=True, out_dtype=BF16,
            side_casts=((w_down, l),) if last else ((w_down, l), (w_in, l + 1)))
        w_in_b = None if last else maybe_w_in[0]
        x2, ss, *maybe_xg = _matmul_res(hid, w_down_b, x2, next_gains[l], tm=1024, tn=1024,
                                       tk_options=(2048, 4096), emit_xg=not last)
        xg = None if last else maybe_xg[0]
    return _final_scale(x2, ss, final_norm_g).reshape(batch, seq, d_model)
```

```python
import functools

import numpy as np
import jax
import jax.numpy as jnp
from jax import lax
from jax.experimental import pallas as pl
from jax.experimental.pallas import tpu as pltpu

HEAD_DIM = 128
CHUNK = 128
WINDOW = 128
ROPE_THETA = 10000.0
NORM_EPS = 1e-5
NEG_INF = -1e30
LOG2_E = float(np.log2(np.e))
N_SGU_GROUPS = 8
N_RET_HEADS = 8
N_Q_HEADS = 16
N_KV_HEADS = 4
Q_PER_KV = N_Q_HEADS // N_KV_HEADS

SGU_WIDTH = N_SGU_GROUPS * HEAD_DIM
RET_WIDTH = N_RET_HEADS * HEAD_DIM
ATT_WIDTH = N_Q_HEADS * HEAD_DIM
KV_WIDTH = N_KV_HEADS * HEAD_DIM
OFF_U = 0
OFF_V = OFF_U + SGU_WIDTH
OFF_RQ = OFF_V + SGU_WIDTH
OFF_RK = OFF_RQ + RET_WIDTH
OFF_RV = OFF_RK + RET_WIDTH
OFF_RG = OFF_RV + RET_WIDTH
OFF_AQ = OFF_RG + RET_WIDTH
OFF_AK = OFF_AQ + ATT_WIDTH
OFF_AV = OFF_AK + KV_WIDTH
IN_WIDTH = OFF_AV + KV_WIDTH
MIX_WIDTH = SGU_WIDTH + RET_WIDTH + ATT_WIDTH

V7X_VMEM_BUDGET = 56 * 1024 * 1024
VMEM_PLAN_MARGIN = 4 * 1024 * 1024
CAST_SLAB_BYTES = 4 * 1024 * 1024
BF16_SUBLANES = 16

F32 = jnp.float32
BF16 = jnp.bfloat16


def _vmem_limit(est_bytes):
    assert est_bytes <= V7X_VMEM_BUDGET, est_bytes
    return min(max(int(est_bytes * 1.25), 16 << 20), V7X_VMEM_BUDGET)


def _dot(a, b):
    return jnp.dot(a, b, preferred_element_type=F32)


def _dot_nt(a, b):
    return lax.dot_general(a, b, (((1,), (1,)), ((), ())), preferred_element_type=F32)


def _dot_tn(a, b):
    return lax.dot_general(a, b, (((0,), (0,)), ((), ())), preferred_element_type=F32)


def _cast_kernel(w_ref, o_ref):
    o_ref[...] = w_ref[...].astype(o_ref.dtype)


def _cast_layer(w, layer):
    _, k, n = w.shape
    tr = 1
    while tr * 2 * n * 4 <= CAST_SLAB_BYTES and k % (tr * 2) == 0:
        tr *= 2
    return pl.pallas_call(
        _cast_kernel,
        grid=(k // tr,),
        in_specs=[pl.BlockSpec((None, tr, n), lambda i: (layer, i, 0))],
        out_specs=pl.BlockSpec((tr, n), lambda i: (i, 0)),
        out_shape=jax.ShapeDtypeStruct((k, n), BF16),
        compiler_params=pltpu.CompilerParams(
            dimension_semantics=("parallel",),
            vmem_limit_bytes=_vmem_limit(2 * tr * n * 6 + tr * n * 4)),
        name="cast_bf16",
    )(w)


def _side_cast_specs(side_casts, steps, step_index):
    in_specs, out_specs, out_shape, vmem = [], [], [], 0
    for w, layer in side_casts:
        _, rows, cols = w.shape
        n_slabs = 1
        while n_slabs * 2 <= steps and rows % (n_slabs * 2) == 0 and (rows // (n_slabs * 2)) % BF16_SUBLANES == 0:
            n_slabs *= 2
        slab = rows // n_slabs

        def slab_index(*ids, _last=n_slabs - 1):
            return jnp.minimum(step_index(*ids), _last)

        in_specs.append(pl.BlockSpec(
            (None, slab, cols), lambda *ids, _l=layer, _s=slab_index: (_l, _s(*ids), 0)))
        out_specs.append(pl.BlockSpec((slab, cols), lambda *ids, _s=slab_index: (_s(*ids), 0)))
        out_shape.append(jax.ShapeDtypeStruct((rows, cols), BF16))
        vmem += 2 * slab * cols * 6 + slab * cols * 4
    return in_specs, out_specs, out_shape, vmem


def _row_rsqrt(ss, width):
    return lax.rsqrt(ss / float(width) + NORM_EPS)


def _prenorm_kernel(x_ref, g_ref, xg_ref, ss_ref):
    x = x_ref[...]
    xg_ref[...] = (x * g_ref[...]).astype(BF16)
    ss_ref[...] = jnp.sum(x * x, axis=-1, keepdims=True)


def _prenorm(x, g, *, tm=512):
    m, k = x.shape
    return pl.pallas_call(
        _prenorm_kernel,
        grid=(m // tm,),
        in_specs=[pl.BlockSpec((tm, k), lambda i: (i, 0)),
                  pl.BlockSpec((1, k), lambda i: (0, 0))],
        out_specs=[pl.BlockSpec((tm, k), lambda i: (i, 0)),
                   pl.BlockSpec((tm, 1), lambda i: (i, 0))],
        out_shape=[jax.ShapeDtypeStruct((m, k), BF16), jax.ShapeDtypeStruct((m, 1), F32)],
        compiler_params=pltpu.CompilerParams(
            dimension_semantics=("parallel",),
            vmem_limit_bytes=_vmem_limit(2 * tm * k * 6 + 2 * tm * k * 4)),
        name="prenorm",
    )(x, g.reshape(1, k))


def _scaled_matmul_kernel(a_ref, ss_ref, w_ref, *rest, relu2, n_side):
    side_in, o_ref, side_out = rest[:n_side], rest[n_side], rest[n_side + 1:]
    acc = _dot(a_ref[...], w_ref[...]) * _row_rsqrt(ss_ref[...], a_ref.shape[1])
    if relu2:
        acc = jnp.square(jnp.maximum(acc, 0.0))
    o_ref[...] = acc.astype(o_ref.dtype)
    for src_ref, dst_ref in zip(side_in, side_out):
        dst_ref[...] = src_ref[...].astype(dst_ref.dtype)


def _scaled_matmul(a, ss, w, *, relu2, out_dtype, side_casts=(), tm=1024, tn=1024):
    m, k = a.shape
    n = w.shape[1]
    assert m % tm == 0 and n % tn == 0
    nj = n // tn
    out_bytes = jnp.dtype(out_dtype).itemsize
    side_in, side_out, side_shape, side_vmem = _side_cast_specs(
        side_casts, (m // tm) * nj, lambda i, j: i * nj + j)
    est = 2 * tm * k * 2 + 2 * k * tn * 2 + 2 * tm * tn * out_bytes + 2 * tm * tn * 4 + side_vmem
    return pl.pallas_call(
        functools.partial(_scaled_matmul_kernel, relu2=relu2, n_side=len(side_casts)),
        grid=(m // tm, nj),
        in_specs=[
            pl.BlockSpec((tm, k), lambda i, j: (i, 0)),
            pl.BlockSpec((tm, 1), lambda i, j: (i, 0)),
            pl.BlockSpec((k, tn), lambda i, j: (0, j)),
            *side_in,
        ],
        out_specs=[pl.BlockSpec((tm, tn), lambda i, j: (i, j)), *side_out],
        out_shape=[jax.ShapeDtypeStruct((m, n), out_dtype), *side_shape],
        compiler_params=pltpu.CompilerParams(
            dimension_semantics=("arbitrary", "arbitrary"),
            vmem_limit_bytes=_vmem_limit(est)),
        name="scaled_matmul_relu2" if relu2 else "scaled_matmul",
    )(a, ss, w, *[src for src, _ in side_casts])


def _matmul_res_kernel(a_ref, w_ref, res_ref, g_ref, *refs, nk, emit_xg, res_by_dma, n_side):
    side_in, (o_ref, ss_ref), rest = refs[:n_side], refs[n_side:n_side + 2], refs[n_side + 2:]
    maybe_xg_ref = rest[0] if emit_xg else None
    side_out = rest[emit_xg:emit_xg + n_side]
    i = pl.program_id(0)
    j = pl.program_id(1)
    kk = pl.program_id(2)
    tm, tn = o_ref.shape

    @pl.when(jnp.logical_and(j == 0, kk == 0))
    def _():
        ss_ref[...] = jnp.zeros_like(ss_ref)

    def emit_norm_inputs(rows, x_new):
        if maybe_xg_ref is not None:
            maybe_xg_ref[rows, :] = (x_new * g_ref[...]).astype(BF16)
        ss_ref[rows, :] += jnp.sum(x_new * x_new, axis=-1, keepdims=True)

    if nk == 1:
        x_new = res_ref[...] + _dot(a_ref[...], w_ref[...])
        o_ref[...] = x_new
        emit_norm_inputs(slice(None), x_new)
        for src_ref, dst_ref in zip(side_in, side_out):
            dst_ref[...] = src_ref[...].astype(dst_ref.dtype)
        return

    acc_ref = o_ref if res_by_dma else rest[-1]

    def res_copy():
        rows = pl.ds(pl.multiple_of(i * tm, tm), tm)
        cols = pl.ds(pl.multiple_of(j * tn, tn), tn)
        return pltpu.make_async_copy(res_ref.at[rows, cols], o_ref, rest[-1])

    @pl.when(kk == 0)
    def _():
        if res_by_dma:
            res_copy().start()
            part = _dot(a_ref[...], w_ref[...])
            res_copy().wait()
            acc_ref[...] += part
        else:
            acc_ref[...] = res_ref[...] + _dot(a_ref[...], w_ref[...])

    @pl.when(jnp.logical_and(kk > 0, kk < nk - 1))
    def _():
        acc_ref[...] += _dot(a_ref[...], w_ref[...])

    @pl.when(kk == nk - 1)
    def _():
        x_new = acc_ref[...] + _dot(a_ref[...], w_ref[...])
        o_ref[...] = x_new
        emit_norm_inputs(slice(None), x_new)


def _plan_matmul_res(tm, tn, k, tk_options, emit_xg):
    for tk in sorted(tk_options, reverse=True):
        for res_by_dma in (False, True):
            if res_by_dma and tk == k:
                continue
            est = (2 * tm * tk * 2 + 2 * tk * tn * 2 + 2 * tm * tn * (4 + 2 * emit_xg)
                   + (0 if res_by_dma else 2 * tm * tn * 4 + (tk < k) * tm * tn * 4)
                   + 5 * tm * tn * 2)
            if est + VMEM_PLAN_MARGIN <= V7X_VMEM_BUDGET:
                return tk, res_by_dma, est
    raise ValueError("no residual-matmul tiling fits VMEM")


def _matmul_res(a, w, res, g_next, *, tm, tn, tk_options, emit_xg=True, side_casts=()):
    m, k = a.shape
    n = w.shape[1]
    tk, res_by_dma, est = _plan_matmul_res(tm, tn, k, tk_options, emit_xg)
    assert m % tm == 0 and n % tn == 0 and k % tk == 0
    nk, nj = k // tk, n // tn
    assert nk == 1 or not side_casts
    side_in, side_out, side_shape, side_vmem = _side_cast_specs(
        side_casts, (m // tm) * nj, lambda i, j, kk: i * nj + j)
    est += side_vmem
    if res_by_dma:
        scratch = [pltpu.SemaphoreType.DMA(())]
    else:
        scratch = [pltpu.VMEM((tm, tn), F32)] if nk > 1 else []
    tile = pl.BlockSpec((tm, tn), lambda i, j, kk: (i, j))
    out_specs = [tile, pl.BlockSpec((tm, 1), lambda i, j, kk: (i, 0))]
    out_shape = [jax.ShapeDtypeStruct((m, n), F32), jax.ShapeDtypeStruct((m, 1), F32)]
    if emit_xg:
        out_specs.append(tile)
        out_shape.append(jax.ShapeDtypeStruct((m, n), BF16))
    return pl.pallas_call(
        functools.partial(_matmul_res_kernel, nk=nk, emit_xg=emit_xg, res_by_dma=res_by_dma,
                          n_side=len(side_casts)),
        grid=(m // tm, nj, nk),
        in_specs=[
            pl.BlockSpec((tm, tk), lambda i, j, kk: (i, kk)),
            pl.BlockSpec((tk, tn), lambda i, j, kk: (kk, j)),
            pl.BlockSpec(memory_space=pl.ANY) if res_by_dma else tile,
            pl.BlockSpec((1, tn), lambda i, j, kk: (0, j)),
            *side_in,
        ],
        out_specs=[*out_specs, *side_out],
        out_shape=[*out_shape, *side_shape],
        scratch_shapes=scratch,
        compiler_params=pltpu.CompilerParams(
            dimension_semantics=("arbitrary", "arbitrary", "arbitrary"),
            vmem_limit_bytes=_vmem_limit(est)),
        name="matmul_res_k%d" % nk,
    )(a, w, res, g_next.reshape(1, n), *[src for src, _ in side_casts])


def _final_scale_kernel(x_ref, ss_ref, g_ref, o_ref):
    o_ref[...] = x_ref[...] * _row_rsqrt(ss_ref[...], x_ref.shape[1]) * g_ref[...]


def _final_scale(x, ss, g, *, tm=512):
    m, k = x.shape
    return pl.pallas_call(
        _final_scale_kernel,
        grid=(m // tm,),
        in_specs=[pl.BlockSpec((tm, k), lambda i: (i, 0)),
                  pl.BlockSpec((tm, 1), lambda i: (i, 0)),
                  pl.BlockSpec((1, k), lambda i: (0, 0))],
        out_specs=pl.BlockSpec((tm, k), lambda i: (i, 0)),
        out_shape=jax.ShapeDtypeStruct((m, k), F32),
        compiler_params=pltpu.CompilerParams(
            dimension_semantics=("parallel",),
            vmem_limit_bytes=_vmem_limit(5 * tm * k * 4)),
        name="final_scale",
    )(x, ss, g.reshape(1, k))


def _rope(t, cos, sgn_sin):
    return t * cos + pltpu.roll(t, HEAD_DIM // 2, 1) * sgn_sin


def _row_index_f32():
    return lax.broadcasted_iota(jnp.int32, (CHUNK, HEAD_DIM), 0).astype(F32)


def _head(ref, off, h):
    return ref[:, off + h * HEAD_DIM: off + (h + 1) * HEAD_DIM]


BWD_CHUNKS_PER_STEP = 8


def _ret_bwd_state_kernel(k_ref, v_ref, cos_ref, sin_ref, raw_ref, sb_ref, st_ref):
    @pl.when(pl.program_id(1) == 0)
    def _():
        st_ref[...] = jnp.zeros_like(st_ref)

    row = _row_index_f32()
    scale = HEAD_DIM ** -0.5
    for c in reversed(range(BWD_CHUNKS_PER_STEP)):
        pos = slice(c * CHUNK, (c + 1) * CHUNK)
        cos = cos_ref[pos, :]
        sin = sin_ref[pos, :]
        for h in range(N_RET_HEADS):
            rows = slice(h * HEAD_DIM, (h + 1) * HEAD_DIM)
            cols = slice(h * HEAD_DIM, (h + 1) * HEAD_DIM)
            st = st_ref[rows, :]
            sb_ref[0, c, rows, :] = st.astype(BF16)
            lgb = -jnp.exp(raw_ref[N_RET_HEADS + h: N_RET_HEADS + h + 1, :])
            kh = _rope(k_ref[pos, cols], cos, sin) * scale
            kw = (kh * jnp.exp(lgb * row)).astype(BF16)
            kv = _dot_tn(kw, v_ref[pos, cols].astype(BF16))
            st_ref[rows, :] = jnp.exp(lgb * float(CHUNK)) * st + kv


def _ret_bwd_states(proj, cos, sgn_sin, raw_rows, *, batch, n_chunks):
    assert OFF_RK % RET_WIDTH == 0 and OFF_RV % RET_WIDTH == 0
    kblk, vblk = OFF_RK // RET_WIDTH, OFF_RV // RET_WIDTH
    per = BWD_CHUNKS_PER_STEP
    assert n_chunks % per == 0
    n_steps = n_chunks // per
    last = n_steps - 1
    return pl.pallas_call(
        _ret_bwd_state_kernel,
        grid=(batch, n_steps),
        in_specs=[
            pl.BlockSpec((per * CHUNK, RET_WIDTH), lambda b, t: (b * n_steps + last - t, kblk)),
            pl.BlockSpec((per * CHUNK, RET_WIDTH), lambda b, t: (b * n_steps + last - t, vblk)),
            pl.BlockSpec((per * CHUNK, HEAD_DIM), lambda b, t: (last - t, 0)),
            pl.BlockSpec((per * CHUNK, HEAD_DIM), lambda b, t: (last - t, 0)),
            pl.BlockSpec((2 * N_RET_HEADS, HEAD_DIM), lambda b, t: (0, 0)),
        ],
        out_specs=pl.BlockSpec((1, per, RET_WIDTH, HEAD_DIM), lambda b, t: (b, last - t, 0, 0)),
        out_shape=jax.ShapeDtypeStruct((batch, n_chunks, RET_WIDTH, HEAD_DIM), BF16),
        scratch_shapes=[pltpu.VMEM((RET_WIDTH, HEAD_DIM), F32)],
        compiler_params=pltpu.CompilerParams(
            dimension_semantics=("parallel", "arbitrary"),
            vmem_limit_bytes=_vmem_limit(4 * per * CHUNK * RET_WIDTH * 4 + (6 << 20))),
        name="ret_bwd_states",
    )(proj, proj, cos, sgn_sin, raw_rows)


def _sgu(p_ref, lng_ref, lnb_ref, ws_ref, bs_ref, o_ref):
    sqrt_half = np.sqrt(0.5).astype(np.float32)

    def gelu(t):
        return 0.5 * t * (1.0 + lax.erf(t * sqrt_half))

    u = gelu(p_ref[:, OFF_U:OFF_U + SGU_WIDTH])
    v = gelu(p_ref[:, OFF_V:OFF_V + SGU_WIDTH])
    mu = jnp.mean(v, axis=-1, keepdims=True)
    var = jnp.mean(jnp.square(v - mu), axis=-1, keepdims=True)
    vn = (v - mu) * lax.rsqrt(var + NORM_EPS) * lng_ref[...] + lnb_ref[...]
    for g in range(N_SGU_GROUPS):
        cols = slice(g * HEAD_DIM, (g + 1) * HEAD_DIM)
        s = _dot(ws_ref[g], vn[:, cols].astype(BF16)) + bs_ref[g]
        o_ref[:, cols] = (u[:, cols] * s).astype(o_ref.dtype)


N_RET_TABLES = 4


def _ret_table(tab_ref, h, t):
    base = (h * N_RET_TABLES + t) * CHUNK
    return tab_ref.at[base:base + CHUNK, :]


def _fill_ret_tables(raw_ref, tab_ref):
    row = _row_index_f32()
    col = lax.broadcasted_iota(jnp.int32, (CHUNK, CHUNK), 1).astype(F32)
    delta = row - col
    for h in range(N_RET_HEADS):
        lgf = -jnp.exp(raw_ref[h:h + 1, :])
        lgb = -jnp.exp(raw_ref[N_RET_HEADS + h: N_RET_HEADS + h + 1, :])
        _ret_table(tab_ref, h, 0)[...] = jnp.where(
            delta >= 0.0,
            jnp.exp(lgf * jnp.maximum(delta, 0.0)),
            jnp.exp(lgb * jnp.maximum(-delta, 0.0)))
        _ret_table(tab_ref, h, 1)[...] = jnp.exp(lgf * (float(CHUNK - 1) - row))
        _ret_table(tab_ref, h, 2)[...] = jnp.exp(lgf * (row + 1.0))
        _ret_table(tab_ref, h, 3)[...] = jnp.exp(lgb * (float(CHUNK) - row))


def _retention(p_ref, sb_ref, raw_ref, cos, sin, o_ref, sf_ref, tab_ref):
    scale = HEAD_DIM ** -0.5
    for h in range(N_RET_HEADS):
        rows = slice(h * HEAD_DIM, (h + 1) * HEAD_DIM)
        lgf = -jnp.exp(raw_ref[h:h + 1, :])
        q = _rope(_head(p_ref, OFF_RQ, h), cos, sin)
        k = _rope(_head(p_ref, OFF_RK, h), cos, sin) * scale
        v = _head(p_ref, OFF_RV, h).astype(BF16)
        scores = _dot_nt(q.astype(BF16), k.astype(BF16)) * _ret_table(tab_ref, h, 0)[...]
        inner = _dot(scores.astype(BF16), v)
        sf_prev = sf_ref[rows, :]
        kv_f = _dot_tn((k * _ret_table(tab_ref, h, 1)[...]).astype(BF16), v)
        cross_f = _dot((q * _ret_table(tab_ref, h, 2)[...]).astype(BF16), sf_prev.astype(BF16))
        sf_ref[rows, :] = jnp.exp(lgf * float(CHUNK)) * sf_prev + kv_f
        cross_b = _dot((q * _ret_table(tab_ref, h, 3)[...]).astype(BF16), sb_ref[0, 0, rows, :])
        rf = (inner + cross_f) + cross_b
        rn = rf * lax.rsqrt(jnp.mean(rf * rf, axis=-1, keepdims=True) + NORM_EPS)
        gate = _head(p_ref, OFF_RG, h)
        o_ref[:, SGU_WIDTH + h * HEAD_DIM: SGU_WIDTH + (h + 1) * HEAD_DIM] = (
            gate * jax.nn.sigmoid(gate) * rn).astype(o_ref.dtype)


def _attention(p_ref, kvp_ref, kvn_ref, cos3_ref, sin3_ref, sink_ref, o_ref, chunk, seq):
    cos3 = cos3_ref[...]
    sin3 = sin3_ref[...]
    q_scale = HEAD_DIM ** -0.5 * LOG2_E
    cos_q = cos3[CHUNK:2 * CHUNK] * q_scale
    sin_q = sin3[CHUNK:2 * CHUNK] * q_scale
    kj = lax.broadcasted_iota(jnp.int32, (3 * CHUNK, CHUNK), 0)
    qi = lax.broadcasted_iota(jnp.int32, (3 * CHUNK, CHUNK), 1)
    rel = kj - CHUNK - qi
    kpos = (chunk - 1) * CHUNK + kj
    valid = (jnp.abs(rel) <= WINDOW) & (kpos >= 0) & (kpos < seq)
    valid = jnp.concatenate([valid] * Q_PER_KV, axis=1)
    out_off = SGU_WIDTH + RET_WIDTH
    for kh in range(N_KV_HEADS):
        kcols = slice(kh * HEAD_DIM, (kh + 1) * HEAD_DIM)
        vcols = slice(KV_WIDTH + kh * HEAD_DIM, KV_WIDTH + (kh + 1) * HEAD_DIM)
        k3 = jnp.concatenate([kvp_ref[:, kcols], _head(p_ref, OFF_AK, kh), kvn_ref[:, kcols]], axis=0)
        v3 = jnp.concatenate([kvp_ref[:, vcols], _head(p_ref, OFF_AV, kh), kvn_ref[:, vcols]], axis=0)
        k3 = _rope(k3, cos3, sin3).astype(BF16)
        qs = jnp.concatenate(
            [_rope(_head(p_ref, OFF_AQ, kh * Q_PER_KV + g), cos_q, sin_q) for g in range(Q_PER_KV)],
            axis=0)
        sink = LOG2_E * jnp.concatenate(
            [sink_ref[kh * Q_PER_KV + g: kh * Q_PER_KV + g + 1, :] for g in range(Q_PER_KV)], axis=1)
        s = _dot_nt(k3, qs.astype(BF16))
        s = jnp.concatenate([
            jnp.where(valid[:CHUNK], s[:CHUNK], NEG_INF),
            s[CHUNK:2 * CHUNK],
            jnp.where(valid[2 * CHUNK:], s[2 * CHUNK:], NEG_INF)], axis=0)
        m = jnp.maximum(jnp.max(s, axis=0, keepdims=True), sink)
        p = jnp.exp2(s - m)
        inv = 1.0 / (jnp.sum(p, axis=0, keepdims=True) + jnp.exp2(sink - m))
        o_t = _dot_tn(v3.astype(BF16), (p * inv).astype(BF16)).astype(o_ref.dtype)
        for g in range(Q_PER_KV):
            hq = kh * Q_PER_KV + g
            o_ref[:, out_off + hq * HEAD_DIM: out_off + (hq + 1) * HEAD_DIM] = (
                o_t[:, g * CHUNK:(g + 1) * CHUNK].T)


def _mixer_kernel(p_ref, kvp_ref, kvn_ref, cos3_ref, sin3_ref, sb_ref, lng_ref, lnb_ref,
                  ws_ref, bs_ref, raw_ref, sink_ref, o_ref, sf_ref, tab_ref, *, seq):
    chunk = pl.program_id(1)

    @pl.when(chunk == 0)
    def _():
        sf_ref[...] = jnp.zeros_like(sf_ref)
        _fill_ret_tables(raw_ref, tab_ref)

    cos = cos3_ref[CHUNK:2 * CHUNK, :]
    sin = sin3_ref[CHUNK:2 * CHUNK, :]
    _attention(p_ref, kvp_ref, kvn_ref, cos3_ref, sin3_ref, sink_ref, o_ref, chunk, seq)
    _retention(p_ref, sb_ref, raw_ref, cos, sin, o_ref, sf_ref, tab_ref)
    _sgu(p_ref, lng_ref, lnb_ref, ws_ref, bs_ref, o_ref)


def _mixers(proj, sb, cos3, sin3, lng, lnb, ws, bias, raw_rows, sink_rows, *, batch, n_chunks):
    m = proj.shape[0]
    kv_blk = OFF_AK // (2 * KV_WIDTH)
    assert kv_blk * 2 * KV_WIDTH == OFF_AK
    last = n_chunks - 1
    const2 = lambda b, n: (0, 0)
    est = (2 * CHUNK * IN_WIDTH * 4 + 4 * CHUNK * 2 * KV_WIDTH * 4 + 2 * CHUNK * MIX_WIDTH * 2
           + (12 << 20))
    return pl.pallas_call(
        functools.partial(_mixer_kernel, seq=n_chunks * CHUNK),
        grid=(batch, n_chunks),
        in_specs=[
            pl.BlockSpec((CHUNK, IN_WIDTH), lambda b, n: (b * n_chunks + n, 0)),
            pl.BlockSpec((CHUNK, 2 * KV_WIDTH),
                         lambda b, n: (b * n_chunks + jnp.maximum(n - 1, 0), kv_blk)),
            pl.BlockSpec((CHUNK, 2 * KV_WIDTH),
                         lambda b, n: (b * n_chunks + jnp.minimum(n + 1, last), kv_blk)),
            pl.BlockSpec((3 * CHUNK, HEAD_DIM), lambda b, n: (n, 0)),
            pl.BlockSpec((3 * CHUNK, HEAD_DIM), lambda b, n: (n, 0)),
            pl.BlockSpec((1, 1, RET_WIDTH, HEAD_DIM), lambda b, n: (b, n, 0, 0)),
            pl.BlockSpec((1, SGU_WIDTH), const2),
            pl.BlockSpec((1, SGU_WIDTH), const2),
            pl.BlockSpec((N_SGU_GROUPS, CHUNK, CHUNK), lambda b, n: (0, 0, 0)),
            pl.BlockSpec((N_SGU_GROUPS, CHUNK, HEAD_DIM), lambda b, n: (0, 0, 0)),
            pl.BlockSpec((2 * N_RET_HEADS, HEAD_DIM), const2),
            pl.BlockSpec((N_Q_HEADS, HEAD_DIM), const2),
        ],
        out_specs=pl.BlockSpec((CHUNK, MIX_WIDTH), lambda b, n: (b * n_chunks + n, 0)),
        out_shape=jax.ShapeDtypeStruct((m, MIX_WIDTH), BF16),
        scratch_shapes=[pltpu.VMEM((RET_WIDTH, HEAD_DIM), F32),
                        pltpu.VMEM((N_RET_HEADS * N_RET_TABLES * CHUNK, HEAD_DIM), F32)],
        compiler_params=pltpu.CompilerParams(
            dimension_semantics=("arbitrary", "arbitrary"),
            vmem_limit_bytes=_vmem_limit(est)),
        name="mixers",
    )(proj, proj, proj, cos3, sin3, sb, lng, lnb, ws, bias, raw_rows, sink_rows)


def _rope_tables(seq):
    pos = jnp.arange(seq, dtype=F32)
    inv = ROPE_THETA ** (-jnp.arange(0, HEAD_DIM, 2, dtype=F32) / HEAD_DIM)
    ang = pos[:, None] * inv[None, :]
    ang = jnp.concatenate([ang, ang], axis=-1)
    sgn = jnp.where(jnp.arange(HEAD_DIM) < HEAD_DIM // 2, -1.0, 1.0).astype(F32)
    return jnp.cos(ang), jnp.sin(ang) * sgn[None, :]


def _three_chunk_view(t, n_chunks):
    seq = t.shape[0]
    pad = jnp.zeros((CHUNK, HEAD_DIM), t.dtype)
    tp = jnp.concatenate([pad, t, pad], axis=0)
    parts = [tp[s * CHUNK: s * CHUNK + seq].reshape(n_chunks, CHUNK, HEAD_DIM) for s in range(3)]
    return jnp.concatenate(parts, axis=1).reshape(n_chunks * 3 * CHUNK, HEAD_DIM)


def kernel(x, ln_mix_g, w_in, sgu_ln_g, sgu_ln_b, sgu_w, sgu_b, ret_log_decay, attn_sink,
           w_out, ln_mlp_g, w_up, w_down, final_norm_g):
    batch, seq, d_model = x.shape
    depth = w_in.shape[0]
    assert w_in.shape[2] == IN_WIDTH and w_out.shape[1] == MIX_WIDTH and seq % CHUNK == 0
    n_chunks = seq // CHUNK
    m = batch * seq

    cos, sgn_sin = _rope_tables(seq)
    cos3 = _three_chunk_view(cos, n_chunks)
    sin3 = _three_chunk_view(sgn_sin, n_chunks)

    next_gains = [ln_mix_g[l] for l in range(1, depth)] + [final_norm_g]

    w_in_b = _cast_layer(w_in, 0)
    x2 = x.reshape(m, d_model)
    xg, ss = _prenorm(x2, ln_mix_g[0])
    for l in range(depth):
        raw_rows = jnp.broadcast_to(
            ret_log_decay[l].astype(F32).reshape(2 * N_RET_HEADS, 1), (2 * N_RET_HEADS, HEAD_DIM))
        sink_rows = jnp.broadcast_to(
            attn_sink[l].astype(F32).reshape(N_Q_HEADS, 1), (N_Q_HEADS, HEAD_DIM))
        sgu_bias = jnp.broadcast_to(
            sgu_b[l].astype(F32)[:, :, None], (N_SGU_GROUPS, CHUNK, HEAD_DIM))
        proj, w_out_b = _scaled_matmul(
            xg, ss, w_in_b, relu2=False, out_dtype=F32, side_casts=((w_out, l),))
        sb = _ret_bwd_states(proj, cos, sgn_sin, raw_rows, batch=batch, n_chunks=n_chunks)
        mix = _mixers(proj, sb, cos3, sin3,
                      sgu_ln_g[l].reshape(1, SGU_WIDTH), sgu_ln_b[l].reshape(1, SGU_WIDTH),
                      sgu_w[l].astype(BF16), sgu_bias, raw_rows, sink_rows,
                      batch=batch, n_chunks=n_chunks)
        x2, ss, xg, w_up_b = _matmul_res(mix, w_out_b, x2, ln_mlp_g[l], tm=1024, tn=512,
                                         tk_options=(MIX_WIDTH,), side_casts=((w_up, l),))
        last = l + 1 == depth
        hid, w_down_b, *maybe_w_in = _scaled_matmul(
            xg, ss, w_up_b, relu2=True, out_dtype=BF16,
            side_casts=((w_down, l),) if last else ((w_down, l), (w_in, l + 1)))
        w_in_b = None if last else maybe_w_in[0]
        x2, ss, *maybe_xg = _matmul_res(hid, w_down_b, x2, next_gains[l], tm=1024, tn=1024,
                                       tk_options=(2048, 4096), emit_xg=not last)
        xg = None if last else maybe_xg[0]
    return _final_scale(x2, ss, final_norm_g).reshape(batch, seq, d_model)
```

```python
import functools

import numpy as np
import jax
import jax.numpy as jnp
from jax import lax
from jax.experimental import pallas as pl
from jax.experimental.pallas import tpu as pltpu

HEAD_DIM = 128
CHUNK = 128
WINDOW = 128
ROPE_THETA = 10000.0
NORM_EPS = 1e-5
NEG_INF = -1e30
LOG2_E = float(np.log2(np.e))
N_SGU_GROUPS = 8
N_RET_HEADS = 8
N_Q_HEADS = 16
N_KV_HEADS = 4
Q_PER_KV = N_Q_HEADS // N_KV_HEADS

SGU_WIDTH = N_SGU_GROUPS * HEAD_DIM
RET_WIDTH = N_RET_HEADS * HEAD_DIM
ATT_WIDTH = N_Q_HEADS * HEAD_DIM
KV_WIDTH = N_KV_HEADS * HEAD_DIM
OFF_U = 0
OFF_V = OFF_U + SGU_WIDTH
OFF_RQ = OFF_V + SGU_WIDTH
OFF_RK = OFF_RQ + RET_WIDTH
OFF_RV = OFF_RK + RET_WIDTH
OFF_RG = OFF_RV + RET_WIDTH
OFF_AQ = OFF_RG + RET_WIDTH
OFF_AK = OFF_AQ + ATT_WIDTH
OFF_AV = OFF_AK + KV_WIDTH
IN_WIDTH = OFF_AV + KV_WIDTH
MIX_WIDTH = SGU_WIDTH + RET_WIDTH + ATT_WIDTH

V7X_VMEM_BUDGET = 56 * 1024 * 1024
VMEM_PLAN_MARGIN = 4 * 1024 * 1024
CAST_SLAB_BYTES = 4 * 1024 * 1024
BF16_SUBLANES = 16

F32 = jnp.float32
BF16 = jnp.bfloat16


def _vmem_limit(est_bytes):
    assert est_bytes <= V7X_VMEM_BUDGET, est_bytes
    return min(max(int(est_bytes * 1.25), 16 << 20), V7X_VMEM_BUDGET)


def _dot(a, b):
    return jnp.dot(a, b, preferred_element_type=F32)


def _dot_nt(a, b):
    return lax.dot_general(a, b, (((1,), (1,)), ((), ())), preferred_element_type=F32)


def _dot_tn(a, b):
    return lax.dot_general(a, b, (((0,), (0,)), ((), ())), preferred_element_type=F32)


def _cast_kernel(w_ref, o_ref):
    o_ref[...] = w_ref[...].astype(o_ref.dtype)


def _cast_layer(w, layer):
    _, k, n = w.shape
    tr = 1
    while tr * 2 * n * 4 <= CAST_SLAB_BYTES and k % (tr * 2) == 0:
        tr *= 2
    return pl.pallas_call(
        _cast_kernel,
        grid=(k // tr,),
        in_specs=[pl.BlockSpec((None, tr, n), lambda i: (layer, i, 0))],
        out_specs=pl.BlockSpec((tr, n), lambda i: (i, 0)),
        out_shape=jax.ShapeDtypeStruct((k, n), BF16),
        compiler_params=pltpu.CompilerParams(
            dimension_semantics=("parallel",),
            vmem_limit_bytes=_vmem_limit(2 * tr * n * 6 + tr * n * 4)),
        name="cast_bf16",
    )(w)


def _side_cast_specs(side_casts, steps, step_index):
    in_specs, out_specs, out_shape, vmem = [], [], [], 0
    for w, layer in side_casts:
        _, rows, cols = w.shape
        n_slabs = 1
        while n_slabs * 2 <= steps and rows % (n_slabs * 2) == 0 and (rows // (n_slabs * 2)) % BF16_SUBLANES == 0:
            n_slabs *= 2
        slab = rows // n_slabs

        def slab_index(*ids, _last=n_slabs - 1):
            return jnp.minimum(step_index(*ids), _last)

        in_specs.append(pl.BlockSpec(
            (None, slab, cols), lambda *ids, _l=layer, _s=slab_index: (_l, _s(*ids), 0)))
        out_specs.append(pl.BlockSpec((slab, cols), lambda *ids, _s=slab_index: (_s(*ids), 0)))
        out_shape.append(jax.ShapeDtypeStruct((rows, cols), BF16))
        vmem += 2 * slab * cols * 6 + slab * cols * 4
    return in_specs, out_specs, out_shape, vmem


def _row_rsqrt(ss, width):
    return lax.rsqrt(ss / float(width) + NORM_EPS)


def _prenorm_kernel(x_ref, g_ref, xg_ref, ss_ref):
    x = x_ref[...]
    xg_ref[...] = (x * g_ref[...]).astype(BF16)
    ss_ref[...] = jnp.sum(x * x, axis=-1, keepdims=True)


def _prenorm(x, g, *, tm=512):
    m, k = x.shape
    return pl.pallas_call(
        _prenorm_kernel,
        grid=(m // tm,),
        in_specs=[pl.BlockSpec((tm, k), lambda i: (i, 0)),
                  pl.BlockSpec((1, k), lambda i: (0, 0))],
        out_specs=[pl.BlockSpec((tm, k), lambda i: (i, 0)),
                   pl.BlockSpec((tm, 1), lambda i: (i, 0))],
        out_shape=[jax.ShapeDtypeStruct((m, k), BF16), jax.ShapeDtypeStruct((m, 1), F32)],
        compiler_params=pltpu.CompilerParams(
            dimension_semantics=("parallel",),
            vmem_limit_bytes=_vmem_limit(2 * tm * k * 6 + 2 * tm * k * 4)),
        name="prenorm",
    )(x, g.reshape(1, k))


def _scaled_matmul_kernel(a_ref, ss_ref, w_ref, *rest, relu2, n_side):
    side_in, o_ref, side_out = rest[:n_side], rest[n_side], rest[n_side + 1:]
    acc = _dot(a_ref[...], w_ref[...]) * _row_rsqrt(ss_ref[...], a_ref.shape[1])
    if relu2:
        acc = jnp.square(jnp.maximum(acc, 0.0))
    o_ref[...] = acc.astype(o_ref.dtype)
    for src_ref, dst_ref in zip(side_in, side_out):
        dst_ref[...] = src_ref[...].astype(dst_ref.dtype)


def _scaled_matmul(a, ss, w, *, relu2, out_dtype, side_casts=(), tm=1024, tn=1024):
    m, k = a.shape
    n = w.shape[1]
    assert m % tm == 0 and n % tn == 0
    nj = n // tn
    out_bytes = jnp.dtype(out_dtype).itemsize
    side_in, side_out, side_shape, side_vmem = _side_cast_specs(
        side_casts, (m // tm) * nj, lambda i, j: i * nj + j)
    est = 2 * tm * k * 2 + 2 * k * tn * 2 + 2 * tm * tn * out_bytes + 2 * tm * tn * 4 + side_vmem
    return pl.pallas_call(
        functools.partial(_scaled_matmul_kernel, relu2=relu2, n_side=len(side_casts)),
        grid=(m // tm, nj),
        in_specs=[
            pl.BlockSpec((tm, k), lambda i, j: (i, 0)),
            pl.BlockSpec((tm, 1), lambda i, j: (i, 0)),
            pl.BlockSpec((k, tn), lambda i, j: (0, j)),
            *side_in,
        ],
        out_specs=[pl.BlockSpec((tm, tn), lambda i, j: (i, j)), *side_out],
        out_shape=[jax.ShapeDtypeStruct((m, n), out_dtype), *side_shape],
        compiler_params=pltpu.CompilerParams(
            dimension_semantics=("arbitrary", "arbitrary"),
            vmem_limit_bytes=_vmem_limit(est)),
        name="scaled_matmul_relu2" if relu2 else "scaled_matmul",
    )(a, ss, w, *[src for src, _ in side_casts])


def _matmul_res_kernel(a_ref, w_ref, res_ref, g_ref, *refs, nk, emit_xg, res_by_dma, n_side):
    side_in, (o_ref, ss_ref), rest = refs[:n_side], refs[n_side:n_side + 2], refs[n_side + 2:]
    maybe_xg_ref = rest[0] if emit_xg else None
    side_out = rest[emit_xg:emit_xg + n_side]
    i = pl.program_id(0)
    j = pl.program_id(1)
    kk = pl.program_id(2)
    tm, tn = o_ref.shape

    @pl.when(jnp.logical_and(j == 0, kk == 0))
    def _():
        ss_ref[...] = jnp.zeros_like(ss_ref)

    def emit_norm_inputs(rows, x_new):
        if maybe_xg_ref is not None:
            maybe_xg_ref[rows, :] = (x_new * g_ref[...]).astype(BF16)
        ss_ref[rows, :] += jnp.sum(x_new * x_new, axis=-1, keepdims=True)

    def cast_side_slabs():
        for src_ref, dst_ref in zip(side_in, side_out):
            dst_ref[...] = src_ref[...].astype(dst_ref.dtype)

    if nk == 1:
        x_new = res_ref[...] + _dot(a_ref[...], w_ref[...])
        o_ref[...] = x_new
        emit_norm_inputs(slice(None), x_new)
        cast_side_slabs()
        return

    acc_ref = o_ref if res_by_dma else rest[-1]

    def res_copy():
        rows = pl.ds(pl.multiple_of(i * tm, tm), tm)
        cols = pl.ds(pl.multiple_of(j * tn, tn), tn)
        return pltpu.make_async_copy(res_ref.at[rows, cols], o_ref, rest[-1])

    @pl.when(kk == 0)
    def _():
        if res_by_dma:
            res_copy().start()
            part = _dot(a_ref[...], w_ref[...])
            res_copy().wait()
            acc_ref[...] += part
        else:
            acc_ref[...] = res_ref[...] + _dot(a_ref[...], w_ref[...])
        cast_side_slabs()

    @pl.when(jnp.logical_and(kk > 0, kk < nk - 1))
    def _():
        acc_ref[...] += _dot(a_ref[...], w_ref[...])
        cast_side_slabs()

    @pl.when(kk == nk - 1)
    def _():
        x_new = acc_ref[...] + _dot(a_ref[...], w_ref[...])
        o_ref[...] = x_new
        emit_norm_inputs(slice(None), x_new)
        cast_side_slabs()


def _plan_matmul_res(tm, tn, k, tk_options, emit_xg):
    for tk in sorted(tk_options, reverse=True):
        for res_by_dma in (False, True):
            if res_by_dma and tk == k:
                continue
            est = (2 * tm * tk * 2 + 2 * tk * tn * 2 + 2 * tm * tn * (4 + 2 * emit_xg)
                   + (0 if res_by_dma else 2 * tm * tn * 4 + (tk < k) * tm * tn * 4)
                   + 5 * tm * tn * 2)
            if est + VMEM_PLAN_MARGIN <= V7X_VMEM_BUDGET:
                return tk, res_by_dma, est
    raise ValueError("no residual-matmul tiling fits VMEM")


def _matmul_res(a, w, res, g_next, *, tm, tn, tk_options, emit_xg=True, side_casts=()):
    m, k = a.shape
    n = w.shape[1]
    tk, res_by_dma, est = _plan_matmul_res(tm, tn, k, tk_options, emit_xg)
    assert m % tm == 0 and n % tn == 0 and k % tk == 0
    nk, nj = k // tk, n // tn
    side_in, side_out, side_shape, side_vmem = _side_cast_specs(
        side_casts, (m // tm) * nj * nk, lambda i, j, kk: (i * nj + j) * nk + kk)
    est += side_vmem
    if res_by_dma:
        scratch = [pltpu.SemaphoreType.DMA(())]
    else:
        scratch = [pltpu.VMEM((tm, tn), F32)] if nk > 1 else []
    tile = pl.BlockSpec((tm, tn), lambda i, j, kk: (i, j))
    out_specs = [tile, pl.BlockSpec((tm, 1), lambda i, j, kk: (i, 0))]
    out_shape = [jax.ShapeDtypeStruct((m, n), F32), jax.ShapeDtypeStruct((m, 1), F32)]
    if emit_xg:
        out_specs.append(tile)
        out_shape.append(jax.ShapeDtypeStruct((m, n), BF16))
    return pl.pallas_call(
        functools.partial(_matmul_res_kernel, nk=nk, emit_xg=emit_xg, res_by_dma=res_by_dma,
                          n_side=len(side_casts)),
        grid=(m // tm, nj, nk),
        in_specs=[
            pl.BlockSpec((tm, tk), lambda i, j, kk: (i, kk)),
            pl.BlockSpec((tk, tn), lambda i, j, kk: (kk, j)),
            pl.BlockSpec(memory_space=pl.ANY) if res_by_dma else tile,
            pl.BlockSpec((1, tn), lambda i, j, kk: (0, j)),
            *side_in,
        ],
        out_specs=[*out_specs, *side_out],
        out_shape=[*out_shape, *side_shape],
        scratch_shapes=scratch,
        compiler_params=pltpu.CompilerParams(
            dimension_semantics=("arbitrary", "arbitrary", "arbitrary"),
            vmem_limit_bytes=_vmem_limit(est)),
        name="matmul_res_k%d" % nk,
    )(a, w, res, g_next.reshape(1, n), *[src for src, _ in side_casts])


def _final_scale_kernel(x_ref, ss_ref, g_ref, o_ref):
    o_ref[...] = x_ref[...] * _row_rsqrt(ss_ref[...], x_ref.shape[1]) * g_ref[...]


def _final_scale(x, ss, g, *, tm=512):
    m, k = x.shape
    return pl.pallas_call(
        _final_scale_kernel,
        grid=(m // tm,),
        in_specs=[pl.BlockSpec((tm, k), lambda i: (i, 0)),
                  pl.BlockSpec((tm, 1), lambda i: (i, 0)),
                  pl.BlockSpec((1, k), lambda i: (0, 0))],
        out_specs=pl.BlockSpec((tm, k), lambda i: (i, 0)),
        out_shape=jax.ShapeDtypeStruct((m, k), F32),
        compiler_params=pltpu.CompilerParams(
            dimension_semantics=("parallel",),
            vmem_limit_bytes=_vmem_limit(5 * tm * k * 4)),
        name="final_scale",
    )(x, ss, g.reshape(1, k))


def _rope(t, cos, sgn_sin):
    return t * cos + pltpu.roll(t, HEAD_DIM // 2, 1) * sgn_sin


def _row_index_f32():
    return lax.broadcasted_iota(jnp.int32, (CHUNK, HEAD_DIM), 0).astype(F32)


def _head(ref, off, h):
    return ref[:, off + h * HEAD_DIM: off + (h + 1) * HEAD_DIM]


BWD_CHUNKS_PER_STEP = 8


def _ret_bwd_state_kernel(k_ref, v_ref, cos_ref, sin_ref, raw_ref, sb_ref, st_ref):
    @pl.when(pl.program_id(1) == 0)
    def _():
        st_ref[...] = jnp.zeros_like(st_ref)

    row = _row_index_f32()
    scale = HEAD_DIM ** -0.5
    for c in reversed(range(BWD_CHUNKS_PER_STEP)):
        pos = slice(c * CHUNK, (c + 1) * CHUNK)
        cos = cos_ref[pos, :]
        sin = sin_ref[pos, :]
        for h in range(N_RET_HEADS):
            rows = slice(h * HEAD_DIM, (h + 1) * HEAD_DIM)
            cols = slice(h * HEAD_DIM, (h + 1) * HEAD_DIM)
            st = st_ref[rows, :]
            sb_ref[0, c, rows, :] = st.astype(BF16)
            lgb = -jnp.exp(raw_ref[N_RET_HEADS + h: N_RET_HEADS + h + 1, :])
            kh = _rope(k_ref[pos, cols], cos, sin) * scale
            kw = (kh * jnp.exp(lgb * row)).astype(BF16)
            kv = _dot_tn(kw, v_ref[pos, cols].astype(BF16))
            st_ref[rows, :] = jnp.exp(lgb * float(CHUNK)) * st + kv


def _ret_bwd_states(proj, cos, sgn_sin, raw_rows, *, batch, n_chunks):
    assert OFF_RK % RET_WIDTH == 0 and OFF_RV % RET_WIDTH == 0
    kblk, vblk = OFF_RK // RET_WIDTH, OFF_RV // RET_WIDTH
    per = BWD_CHUNKS_PER_STEP
    assert n_chunks % per == 0
    n_steps = n_chunks // per
    last = n_steps - 1
    return pl.pallas_call(
        _ret_bwd_state_kernel,
        grid=(batch, n_steps),
        in_specs=[
            pl.BlockSpec((per * CHUNK, RET_WIDTH), lambda b, t: (b * n_steps + last - t, kblk)),
            pl.BlockSpec((per * CHUNK, RET_WIDTH), lambda b, t: (b * n_steps + last - t, vblk)),
            pl.BlockSpec((per * CHUNK, HEAD_DIM), lambda b, t: (last - t, 0)),
            pl.BlockSpec((per * CHUNK, HEAD_DIM), lambda b, t: (last - t, 0)),
            pl.BlockSpec((2 * N_RET_HEADS, HEAD_DIM), lambda b, t: (0, 0)),
        ],
        out_specs=pl.BlockSpec((1, per, RET_WIDTH, HEAD_DIM), lambda b, t: (b, last - t, 0, 0)),
        out_shape=jax.ShapeDtypeStruct((batch, n_chunks, RET_WIDTH, HEAD_DIM), BF16),
        scratch_shapes=[pltpu.VMEM((RET_WIDTH, HEAD_DIM), F32)],
        compiler_params=pltpu.CompilerParams(
            dimension_semantics=("parallel", "arbitrary"),
            vmem_limit_bytes=_vmem_limit(4 * per * CHUNK * RET_WIDTH * 4 + (6 << 20))),
        name="ret_bwd_states",
    )(proj, proj, cos, sgn_sin, raw_rows)


def _sgu(p_ref, lng_ref, lnb_ref, ws_ref, bs_ref, o_ref):
    sqrt_half = np.sqrt(0.5).astype(np.float32)

    def gelu(t):
        return 0.5 * t * (1.0 + lax.erf(t * sqrt_half))

    u = gelu(p_ref[:, OFF_U:OFF_U + SGU_WIDTH])
    v = gelu(p_ref[:, OFF_V:OFF_V + SGU_WIDTH])
    mu = jnp.mean(v, axis=-1, keepdims=True)
    var = jnp.mean(jnp.square(v - mu), axis=-1, keepdims=True)
    vn = (v - mu) * lax.rsqrt(var + NORM_EPS) * lng_ref[...] + lnb_ref[...]
    for g in range(N_SGU_GROUPS):
        cols = slice(g * HEAD_DIM, (g + 1) * HEAD_DIM)
        s = _dot(ws_ref[g], vn[:, cols].astype(BF16)) + bs_ref[g]
        o_ref[:, cols] = (u[:, cols] * s).astype(o_ref.dtype)


N_RET_TABLES = 4


def _ret_table(tab_ref, h, t):
    base = (h * N_RET_TABLES + t) * CHUNK
    return tab_ref.at[base:base + CHUNK, :]


def _fill_ret_tables(raw_ref, tab_ref):
    row = _row_index_f32()
    col = lax.broadcasted_iota(jnp.int32, (CHUNK, CHUNK), 1).astype(F32)
    delta = row - col
    for h in range(N_RET_HEADS):
        lgf = -jnp.exp(raw_ref[h:h + 1, :])
        lgb = -jnp.exp(raw_ref[N_RET_HEADS + h: N_RET_HEADS + h + 1, :])
        _ret_table(tab_ref, h, 0)[...] = jnp.where(
            delta >= 0.0,
            jnp.exp(lgf * jnp.maximum(delta, 0.0)),
            jnp.exp(lgb * jnp.maximum(-delta, 0.0)))
        _ret_table(tab_ref, h, 1)[...] = jnp.exp(lgf * (float(CHUNK - 1) - row))
        _ret_table(tab_ref, h, 2)[...] = jnp.exp(lgf * (row + 1.0))
        _ret_table(tab_ref, h, 3)[...] = jnp.exp(lgb * (float(CHUNK) - row))


def _retention(p_ref, sb_ref, raw_ref, cos, sin, o_ref, sf_ref, tab_ref):
    scale = HEAD_DIM ** -0.5
    for h in range(N_RET_HEADS):
        rows = slice(h * HEAD_DIM, (h + 1) * HEAD_DIM)
        lgf = -jnp.exp(raw_ref[h:h + 1, :])
        q = _rope(_head(p_ref, OFF_RQ, h), cos, sin)
        k = _rope(_head(p_ref, OFF_RK, h), cos, sin) * scale
        v = _head(p_ref, OFF_RV, h).astype(BF16)
        scores = _dot_nt(q.astype(BF16), k.astype(BF16)) * _ret_table(tab_ref, h, 0)[...]
        inner = _dot(scores.astype(BF16), v)
        sf_prev = sf_ref[rows, :]
        kv_f = _dot_tn((k * _ret_table(tab_ref, h, 1)[...]).astype(BF16), v)
        cross_f = _dot((q * _ret_table(tab_ref, h, 2)[...]).astype(BF16), sf_prev.astype(BF16))
        sf_ref[rows, :] = jnp.exp(lgf * float(CHUNK)) * sf_prev + kv_f
        cross_b = _dot((q * _ret_table(tab_ref, h, 3)[...]).astype(BF16), sb_ref[0, 0, rows, :])
        rf = (inner + cross_f) + cross_b
        rn = rf * lax.rsqrt(jnp.mean(rf * rf, axis=-1, keepdims=True) + NORM_EPS)
        gate = _head(p_ref, OFF_RG, h)
        o_ref[:, SGU_WIDTH + h * HEAD_DIM: SGU_WIDTH + (h + 1) * HEAD_DIM] = (
            gate * jax.nn.sigmoid(gate) * rn).astype(o_ref.dtype)


def _attention(p_ref, kvp_ref, kvn_ref, cos3_ref, sin3_ref, sink_ref, o_ref, chunk, seq):
    cos3 = cos3_ref[...]
    sin3 = sin3_ref[...]
    q_scale = HEAD_DIM ** -0.5 * LOG2_E
    cos_q = cos3[CHUNK:2 * CHUNK] * q_scale
    sin_q = sin3[CHUNK:2 * CHUNK] * q_scale
    kj = lax.broadcasted_iota(jnp.int32, (3 * CHUNK, CHUNK), 0)
    qi = lax.broadcasted_iota(jnp.int32, (3 * CHUNK, CHUNK), 1)
    rel = kj - CHUNK - qi
    kpos = (chunk - 1) * CHUNK + kj
    valid = (jnp.abs(rel) <= WINDOW) & (kpos >= 0) & (kpos < seq)
    valid = jnp.concatenate([valid] * Q_PER_KV, axis=1)
    out_off = SGU_WIDTH + RET_WIDTH
    for kh in range(N_KV_HEADS):
        kcols = slice(kh * HEAD_DIM, (kh + 1) * HEAD_DIM)
        vcols = slice(KV_WIDTH + kh * HEAD_DIM, KV_WIDTH + (kh + 1) * HEAD_DIM)
        k3 = jnp.concatenate([kvp_ref[:, kcols], _head(p_ref, OFF_AK, kh), kvn_ref[:, kcols]], axis=0)
        v3 = jnp.concatenate([kvp_ref[:, vcols], _head(p_ref, OFF_AV, kh), kvn_ref[:, vcols]], axis=0)
        k3 = _rope(k3, cos3, sin3).astype(BF16)
        qs = jnp.concatenate(
            [_rope(_head(p_ref, OFF_AQ, kh * Q_PER_KV + g), cos_q, sin_q) for g in range(Q_PER_KV)],
            axis=0)
        sink = LOG2_E * jnp.concatenate(
            [sink_ref[kh * Q_PER_KV + g: kh * Q_PER_KV + g + 1, :] for g in range(Q_PER_KV)], axis=1)
        s = _dot_nt(k3, qs.astype(BF16))
        s = jnp.concatenate([
            jnp.where(valid[:CHUNK], s[:CHUNK], NEG_INF),
            s[CHUNK:2 * CHUNK],
            jnp.where(valid[2 * CHUNK:], s[2 * CHUNK:], NEG_INF)], axis=0)
        m = jnp.maximum(jnp.max(s, axis=0, keepdims=True), sink)
        p = jnp.exp2(s - m)
        inv = 1.0 / (jnp.sum(p, axis=0, keepdims=True) + jnp.exp2(sink - m))
        o_t = _dot_tn(v3.astype(BF16), (p * inv).astype(BF16)).astype(o_ref.dtype)
        for g in range(Q_PER_KV):
            hq = kh * Q_PER_KV + g
            o_ref[:, out_off + hq * HEAD_DIM: out_off + (hq + 1) * HEAD_DIM] = (
                o_t[:, g * CHUNK:(g + 1) * CHUNK].T)


def _mixer_kernel(p_ref, kvp_ref, kvn_ref, cos3_ref, sin3_ref, sb_ref, lng_ref, lnb_ref,
                  ws_ref, bs_ref, raw_ref, sink_ref, o_ref, sf_ref, tab_ref, *, seq):
    chunk = pl.program_id(1)

    @pl.when(chunk == 0)
    def _():
        sf_ref[...] = jnp.zeros_like(sf_ref)
        _fill_ret_tables(raw_ref, tab_ref)

    cos = cos3_ref[CHUNK:2 * CHUNK, :]
    sin = sin3_ref[CHUNK:2 * CHUNK, :]
    _attention(p_ref, kvp_ref, kvn_ref, cos3_ref, sin3_ref, sink_ref, o_ref, chunk, seq)
    _retention(p_ref, sb_ref, raw_ref, cos, sin, o_ref, sf_ref, tab_ref)
    _sgu(p_ref, lng_ref, lnb_ref, ws_ref, bs_ref, o_ref)


def _mixers(proj, sb, cos3, sin3, lng, lnb, ws, bias, raw_rows, sink_rows, *, batch, n_chunks):
    m = proj.shape[0]
    kv_blk = OFF_AK // (2 * KV_WIDTH)
    assert kv_blk * 2 * KV_WIDTH == OFF_AK
    last = n_chunks - 1
    const2 = lambda b, n: (0, 0)
    est = (2 * CHUNK * IN_WIDTH * 4 + 4 * CHUNK * 2 * KV_WIDTH * 4 + 2 * CHUNK * MIX_WIDTH * 2
           + (12 << 20))
    return pl.pallas_call(
        functools.partial(_mixer_kernel, seq=n_chunks * CHUNK),
        grid=(batch, n_chunks),
        in_specs=[
            pl.BlockSpec((CHUNK, IN_WIDTH), lambda b, n: (b * n_chunks + n, 0)),
            pl.BlockSpec((CHUNK, 2 * KV_WIDTH),
                         lambda b, n: (b * n_chunks + jnp.maximum(n - 1, 0), kv_blk)),
            pl.BlockSpec((CHUNK, 2 * KV_WIDTH),
                         lambda b, n: (b * n_chunks + jnp.minimum(n + 1, last), kv_blk)),
            pl.BlockSpec((3 * CHUNK, HEAD_DIM), lambda b, n: (n, 0)),
            pl.BlockSpec((3 * CHUNK, HEAD_DIM), lambda b, n: (n, 0)),
            pl.BlockSpec((1, 1, RET_WIDTH, HEAD_DIM), lambda b, n: (b, n, 0, 0)),
            pl.BlockSpec((1, SGU_WIDTH), const2),
            pl.BlockSpec((1, SGU_WIDTH), const2),
            pl.BlockSpec((N_SGU_GROUPS, CHUNK, CHUNK), lambda b, n: (0, 0, 0)),
            pl.BlockSpec((N_SGU_GROUPS, CHUNK, HEAD_DIM), lambda b, n: (0, 0, 0)),
            pl.BlockSpec((2 * N_RET_HEADS, HEAD_DIM), const2),
            pl.BlockSpec((N_Q_HEADS, HEAD_DIM), const2),
        ],
        out_specs=pl.BlockSpec((CHUNK, MIX_WIDTH), lambda b, n: (b * n_chunks + n, 0)),
        out_shape=jax.ShapeDtypeStruct((m, MIX_WIDTH), BF16),
        scratch_shapes=[pltpu.VMEM((RET_WIDTH, HEAD_DIM), F32),
                        pltpu.VMEM((N_RET_HEADS * N_RET_TABLES * CHUNK, HEAD_DIM), F32)],
        compiler_params=pltpu.CompilerParams(
            dimension_semantics=("arbitrary", "arbitrary"),
            vmem_limit_bytes=_vmem_limit(est)),
        name="mixers",
    )(proj, proj, proj, cos3, sin3, sb, lng, lnb, ws, bias, raw_rows, sink_rows)


def _rope_tables(seq):
    pos = jnp.arange(seq, dtype=F32)
    inv = ROPE_THETA ** (-jnp.arange(0, HEAD_DIM, 2, dtype=F32) / HEAD_DIM)
    ang = pos[:, None] * inv[None, :]
    ang = jnp.concatenate([ang, ang], axis=-1)
    sgn = jnp.where(jnp.arange(HEAD_DIM) < HEAD_DIM // 2, -1.0, 1.0).astype(F32)
    return jnp.cos(ang), jnp.sin(ang) * sgn[None, :]


def _three_chunk_view(t, n_chunks):
    seq = t.shape[0]
    pad = jnp.zeros((CHUNK, HEAD_DIM), t.dtype)
    tp = jnp.concatenate([pad, t, pad], axis=0)
    parts = [tp[s * CHUNK: s * CHUNK + seq].reshape(n_chunks, CHUNK, HEAD_DIM) for s in range(3)]
    return jnp.concatenate(parts, axis=1).reshape(n_chunks * 3 * CHUNK, HEAD_DIM)


def kernel(x, ln_mix_g, w_in, sgu_ln_g, sgu_ln_b, sgu_w, sgu_b, ret_log_decay, attn_sink,
           w_out, ln_mlp_g, w_up, w_down, final_norm_g):
    batch, seq, d_model = x.shape
    depth = w_in.shape[0]
    assert w_in.shape[2] == IN_WIDTH and w_out.shape[1] == MIX_WIDTH and seq % CHUNK == 0
    n_chunks = seq // CHUNK
    m = batch * seq

    cos, sgn_sin = _rope_tables(seq)
    cos3 = _three_chunk_view(cos, n_chunks)
    sin3 = _three_chunk_view(sgn_sin, n_chunks)

    next_gains = [ln_mix_g[l] for l in range(1, depth)] + [final_norm_g]

    w_in_b = _cast_layer(w_in, 0)
    x2 = x.reshape(m, d_model)
    xg, ss = _prenorm(x2, ln_mix_g[0])
    for l in range(depth):
        raw_rows = jnp.broadcast_to(
            ret_log_decay[l].astype(F32).reshape(2 * N_RET_HEADS, 1), (2 * N_RET_HEADS, HEAD_DIM))
        sink_rows = jnp.broadcast_to(
            attn_sink[l].astype(F32).reshape(N_Q_HEADS, 1), (N_Q_HEADS, HEAD_DIM))
        sgu_bias = jnp.broadcast_to(
            sgu_b[l].astype(F32)[:, :, None], (N_SGU_GROUPS, CHUNK, HEAD_DIM))
        proj, w_out_b = _scaled_matmul(
            xg, ss, w_in_b, relu2=False, out_dtype=F32, side_casts=((w_out, l),))
        sb = _ret_bwd_states(proj, cos, sgn_sin, raw_rows, batch=batch, n_chunks=n_chunks)
        mix = _mixers(proj, sb, cos3, sin3,
                      sgu_ln_g[l].reshape(1, SGU_WIDTH), sgu_ln_b[l].reshape(1, SGU_WIDTH),
                      sgu_w[l].astype(BF16), sgu_bias, raw_rows, sink_rows,
                      batch=batch, n_chunks=n_chunks)
        x2, ss, xg, *maybe_w_up = _matmul_res(mix, w_out_b, x2, ln_mlp_g[l], tm=1024, tn=512,
                                              tk_options=(MIX_WIDTH,),
                                              side_casts=((w_up, l),) if l == 0 else ())
        w_up_b = maybe_w_up[0] if maybe_w_up else w_up_b
        last = l + 1 == depth
        hid, w_down_b, *maybe_w_in = _scaled_matmul(
            xg, ss, w_up_b, relu2=True, out_dtype=BF16,
            side_casts=((w_down, l),) if last else ((w_down, l), (w_in, l + 1)))
        x2, ss, *for_next_layer = _matmul_res(hid, w_down_b, x2, next_gains[l], tm=1024, tn=1024,
                                             tk_options=(2048, 4096), emit_xg=not last,
                                             side_casts=() if last else ((w_up, l + 1),))
        if not last:
            w_in_b = maybe_w_in[0]
            xg, w_up_b = for_next_layer
    return _final_scale(x2, ss, final_norm_g).reshape(batch, seq, d_model)
```

```python
import functools

import numpy as np
import jax
import jax.numpy as jnp
from jax import lax
from jax.experimental import pallas as pl
from jax.experimental.pallas import tpu as pltpu

HEAD_DIM = 128
CHUNK = 128
WINDOW = 128
ROPE_THETA = 10000.0
NORM_EPS = 1e-5
NEG_INF = -1e30
LOG2_E = float(np.log2(np.e))
N_SGU_GROUPS = 8
N_RET_HEADS = 8
N_Q_HEADS = 16
N_KV_HEADS = 4
Q_PER_KV = N_Q_HEADS // N_KV_HEADS

SGU_WIDTH = N_SGU_GROUPS * HEAD_DIM
RET_WIDTH = N_RET_HEADS * HEAD_DIM
ATT_WIDTH = N_Q_HEADS * HEAD_DIM
KV_WIDTH = N_KV_HEADS * HEAD_DIM
OFF_U = 0
OFF_V = OFF_U + SGU_WIDTH
OFF_RQ = OFF_V + SGU_WIDTH
OFF_RK = OFF_RQ + RET_WIDTH
OFF_RV = OFF_RK + RET_WIDTH
OFF_RG = OFF_RV + RET_WIDTH
OFF_AQ = OFF_RG + RET_WIDTH
OFF_AK = OFF_AQ + ATT_WIDTH
OFF_AV = OFF_AK + KV_WIDTH
IN_WIDTH = OFF_AV + KV_WIDTH
MIX_WIDTH = SGU_WIDTH + RET_WIDTH + ATT_WIDTH

V7X_VMEM_BUDGET = 56 * 1024 * 1024
VMEM_PLAN_MARGIN = 4 * 1024 * 1024
CAST_SLAB_BYTES = 4 * 1024 * 1024
BF16_SUBLANES = 16
W_RING = 3

F32 = jnp.float32
BF16 = jnp.bfloat16


def _vmem_limit(est_bytes):
    assert est_bytes <= V7X_VMEM_BUDGET, est_bytes
    return min(max(int(est_bytes * 1.25), 16 << 20), V7X_VMEM_BUDGET)


def _dot(a, b):
    return jnp.dot(a, b, preferred_element_type=F32)


def _dot_nt(a, b):
    return lax.dot_general(a, b, (((1,), (1,)), ((), ())), preferred_element_type=F32)


def _dot_tn(a, b):
    return lax.dot_general(a, b, (((0,), (0,)), ((), ())), preferred_element_type=F32)


def _cast_kernel(w_ref, o_ref):
    o_ref[...] = w_ref[...].astype(o_ref.dtype)


def _cast_layer(w, layer):
    _, k, n = w.shape
    tr = 1
    while tr * 2 * n * 4 <= CAST_SLAB_BYTES and k % (tr * 2) == 0:
        tr *= 2
    return pl.pallas_call(
        _cast_kernel,
        grid=(k // tr,),
        in_specs=[pl.BlockSpec((None, tr, n), lambda i: (layer, i, 0))],
        out_specs=pl.BlockSpec((tr, n), lambda i: (i, 0)),
        out_shape=jax.ShapeDtypeStruct((k, n), BF16),
        compiler_params=pltpu.CompilerParams(
            dimension_semantics=("parallel",),
            vmem_limit_bytes=_vmem_limit(2 * tr * n * 6 + tr * n * 4)),
        name="cast_bf16",
    )(w)


def _side_cast_specs(side_casts, steps, step_index):
    in_specs, out_specs, out_shape, vmem = [], [], [], 0
    for w, layer in side_casts:
        _, rows, cols = w.shape
        n_slabs = 1
        while n_slabs * 2 <= steps and rows % (n_slabs * 2) == 0 and (rows // (n_slabs * 2)) % BF16_SUBLANES == 0:
            n_slabs *= 2
        slab = rows // n_slabs

        def slab_index(*ids, _last=n_slabs - 1):
            return jnp.minimum(step_index(*ids), _last)

        in_specs.append(pl.BlockSpec(
            (None, slab, cols), lambda *ids, _l=layer, _s=slab_index: (_l, _s(*ids), 0)))
        out_specs.append(pl.BlockSpec((slab, cols), lambda *ids, _s=slab_index: (_s(*ids), 0)))
        out_shape.append(jax.ShapeDtypeStruct((rows, cols), BF16))
        vmem += 2 * slab * cols * 6 + slab * cols * 4
    return in_specs, out_specs, out_shape, vmem


def _row_rsqrt(ss, width):
    return lax.rsqrt(ss / float(width) + NORM_EPS)


def _prenorm_kernel(x_ref, g_ref, xg_ref, ss_ref):
    x = x_ref[...]
    xg_ref[...] = (x * g_ref[...]).astype(BF16)
    ss_ref[...] = jnp.sum(x * x, axis=-1, keepdims=True)


def _prenorm(x, g, *, tm=512):
    m, k = x.shape
    return pl.pallas_call(
        _prenorm_kernel,
        grid=(m // tm,),
        in_specs=[pl.BlockSpec((tm, k), lambda i: (i, 0)),
                  pl.BlockSpec((1, k), lambda i: (0, 0))],
        out_specs=[pl.BlockSpec((tm, k), lambda i: (i, 0)),
                   pl.BlockSpec((tm, 1), lambda i: (i, 0))],
        out_shape=[jax.ShapeDtypeStruct((m, k), BF16), jax.ShapeDtypeStruct((m, 1), F32)],
        compiler_params=pltpu.CompilerParams(
            dimension_semantics=("parallel",),
            vmem_limit_bytes=_vmem_limit(2 * tm * k * 6 + 2 * tm * k * 4)),
        name="prenorm",
    )(x, g.reshape(1, k))


def _scaled_matmul_kernel(a_ref, ss_ref, w_ref, *rest, relu2, n_side):
    side_in, o_ref, side_out = rest[:n_side], rest[n_side], rest[n_side + 1:]
    acc = _dot(a_ref[...], w_ref[...]) * _row_rsqrt(ss_ref[...], a_ref.shape[1])
    if relu2:
        acc = jnp.square(jnp.maximum(acc, 0.0))
    o_ref[...] = acc.astype(o_ref.dtype)
    for src_ref, dst_ref in zip(side_in, side_out):
        dst_ref[...] = src_ref[...].astype(dst_ref.dtype)


def _scaled_matmul(a, ss, w, *, relu2, out_dtype, side_casts=(), tm=1024, tn=1024):
    m, k = a.shape
    n = w.shape[1]
    assert m % tm == 0 and n % tn == 0
    nj = n // tn
    out_bytes = jnp.dtype(out_dtype).itemsize
    side_in, side_out, side_shape, side_vmem = _side_cast_specs(
        side_casts, (m // tm) * nj, lambda i, j: i * nj + j)
    est = 2 * tm * k * 2 + 2 * k * tn * 2 + 2 * tm * tn * out_bytes + 2 * tm * tn * 4 + side_vmem
    return pl.pallas_call(
        functools.partial(_scaled_matmul_kernel, relu2=relu2, n_side=len(side_casts)),
        grid=(m // tm, nj),
        in_specs=[
            pl.BlockSpec((tm, k), lambda i, j: (i, 0)),
            pl.BlockSpec((tm, 1), lambda i, j: (i, 0)),
            pl.BlockSpec((k, tn), lambda i, j: (0, j)),
            *side_in,
        ],
        out_specs=[pl.BlockSpec((tm, tn), lambda i, j: (i, j)), *side_out],
        out_shape=[jax.ShapeDtypeStruct((m, n), out_dtype), *side_shape],
        compiler_params=pltpu.CompilerParams(
            dimension_semantics=("arbitrary", "arbitrary"),
            vmem_limit_bytes=_vmem_limit(est)),
        name="scaled_matmul_relu2" if relu2 else "scaled_matmul",
    )(a, ss, w, *[src for src, _ in side_casts])


def _matmul_res_kernel(a_ref, w_ref, res_ref, g_ref, *refs, nk, emit_xg, res_by_dma, n_side):
    side_in, (o_ref, ss_ref), rest = refs[:n_side], refs[n_side:n_side + 2], refs[n_side + 2:]
    maybe_xg_ref = rest[0] if emit_xg else None
    side_out = rest[emit_xg:emit_xg + n_side]
    i = pl.program_id(0)
    j = pl.program_id(1)
    kk = pl.program_id(2)
    tm, tn = o_ref.shape

    @pl.when(jnp.logical_and(j == 0, kk == 0))
    def _():
        ss_ref[...] = jnp.zeros_like(ss_ref)

    def emit_norm_inputs(rows, x_new):
        if maybe_xg_ref is not None:
            maybe_xg_ref[rows, :] = (x_new * g_ref[...]).astype(BF16)
        ss_ref[rows, :] += jnp.sum(x_new * x_new, axis=-1, keepdims=True)

    def cast_side_slabs():
        for src_ref, dst_ref in zip(side_in, side_out):
            dst_ref[...] = src_ref[...].astype(dst_ref.dtype)

    if nk == 1:
        w_buf, w_sems = rest[-2:]
        nj = pl.num_programs(1)
        step = i * nj + j

        def w_copy(s):
            cols = pl.ds(pl.multiple_of((s % nj) * tn, tn), tn)
            return pltpu.make_async_copy(w_ref.at[:, cols], w_buf.at[s % W_RING],
                                         w_sems.at[s % W_RING])

        @pl.when(step == 0)
        def _():
            for s in range(W_RING - 1):
                w_copy(s).start()

        @pl.when(step + (W_RING - 1) < pl.num_programs(0) * nj)
        def _():
            w_copy(step + (W_RING - 1)).start()

        w_copy(step).wait()
        x_new = res_ref[...] + _dot(a_ref[...], w_buf[step % W_RING])
        o_ref[...] = x_new
        emit_norm_inputs(slice(None), x_new)
        cast_side_slabs()
        return

    acc_ref = o_ref if res_by_dma else rest[-1]

    def res_copy():
        rows = pl.ds(pl.multiple_of(i * tm, tm), tm)
        cols = pl.ds(pl.multiple_of(j * tn, tn), tn)
        return pltpu.make_async_copy(res_ref.at[rows, cols], o_ref, rest[-1])

    @pl.when(kk == 0)
    def _():
        if res_by_dma:
            res_copy().start()
            part = _dot(a_ref[...], w_ref[...])
            res_copy().wait()
            acc_ref[...] += part
        else:
            acc_ref[...] = res_ref[...] + _dot(a_ref[...], w_ref[...])
        cast_side_slabs()

    @pl.when(jnp.logical_and(kk > 0, kk < nk - 1))
    def _():
        acc_ref[...] += _dot(a_ref[...], w_ref[...])
        cast_side_slabs()

    @pl.when(kk == nk - 1)
    def _():
        x_new = acc_ref[...] + _dot(a_ref[...], w_ref[...])
        o_ref[...] = x_new
        emit_norm_inputs(slice(None), x_new)
        cast_side_slabs()


def _plan_matmul_res(tm, tn, k, tk_options, emit_xg):
    for tk in sorted(tk_options, reverse=True):
        for res_by_dma in (False, True):
            if res_by_dma and tk == k:
                continue
            est = (2 * tm * tk * 2 + 2 * tk * tn * 2 + 2 * tm * tn * (4 + 2 * emit_xg)
                   + (0 if res_by_dma else 2 * tm * tn * 4 + (tk < k) * tm * tn * 4)
                   + 5 * tm * tn * 2)
            if est + VMEM_PLAN_MARGIN <= V7X_VMEM_BUDGET:
                return tk, res_by_dma, est
    raise ValueError("no residual-matmul tiling fits VMEM")


def _matmul_res(a, w, res, g_next, *, tm, tn, tk_options, emit_xg=True, side_casts=()):
    m, k = a.shape
    n = w.shape[1]
    tk, res_by_dma, est = _plan_matmul_res(tm, tn, k, tk_options, emit_xg)
    assert m % tm == 0 and n % tn == 0 and k % tk == 0
    nk, nj = k // tk, n // tn
    side_in, side_out, side_shape, side_vmem = _side_cast_specs(
        side_casts, (m // tm) * nj * nk, lambda i, j, kk: (i * nj + j) * nk + kk)
    est += side_vmem
    w_ring = nk == 1
    if res_by_dma:
        scratch = [pltpu.SemaphoreType.DMA(())]
    elif w_ring:
        scratch = [pltpu.VMEM((W_RING, k, tn), BF16), pltpu.SemaphoreType.DMA((W_RING,))]
        est += (W_RING - 2) * k * tn * 2
    else:
        scratch = [pltpu.VMEM((tm, tn), F32)]
    tile = pl.BlockSpec((tm, tn), lambda i, j, kk: (i, j))
    out_specs = [tile, pl.BlockSpec((tm, 1), lambda i, j, kk: (i, 0))]
    out_shape = [jax.ShapeDtypeStruct((m, n), F32), jax.ShapeDtypeStruct((m, 1), F32)]
    if emit_xg:
        out_specs.append(tile)
        out_shape.append(jax.ShapeDtypeStruct((m, n), BF16))
    return pl.pallas_call(
        functools.partial(_matmul_res_kernel, nk=nk, emit_xg=emit_xg, res_by_dma=res_by_dma,
                          n_side=len(side_casts)),
        grid=(m // tm, nj, nk),
        in_specs=[
            pl.BlockSpec((tm, tk), lambda i, j, kk: (i, kk)),
            (pl.BlockSpec(memory_space=pl.ANY) if w_ring
             else pl.BlockSpec((tk, tn), lambda i, j, kk: (kk, j))),
            pl.BlockSpec(memory_space=pl.ANY) if res_by_dma else tile,
            pl.BlockSpec((1, tn), lambda i, j, kk: (0, j)),
            *side_in,
        ],
        out_specs=[*out_specs, *side_out],
        out_shape=[*out_shape, *side_shape],
        scratch_shapes=scratch,
        compiler_params=pltpu.CompilerParams(
            dimension_semantics=("arbitrary", "arbitrary", "arbitrary"),
            vmem_limit_bytes=_vmem_limit(est)),
        name="matmul_res_k%d" % nk,
    )(a, w, res, g_next.reshape(1, n), *[src for src, _ in side_casts])


def _final_scale_kernel(x_ref, ss_ref, g_ref, o_ref):
    o_ref[...] = x_ref[...] * _row_rsqrt(ss_ref[...], x_ref.shape[1]) * g_ref[...]


def _final_scale(x, ss, g, *, tm=512):
    m, k = x.shape
    return pl.pallas_call(
        _final_scale_kernel,
        grid=(m // tm,),
        in_specs=[pl.BlockSpec((tm, k), lambda i: (i, 0)),
                  pl.BlockSpec((tm, 1), lambda i: (i, 0)),
                  pl.BlockSpec((1, k), lambda i: (0, 0))],
        out_specs=pl.BlockSpec((tm, k), lambda i: (i, 0)),
        out_shape=jax.ShapeDtypeStruct((m, k), F32),
        compiler_params=pltpu.CompilerParams(
            dimension_semantics=("parallel",),
            vmem_limit_bytes=_vmem_limit(5 * tm * k * 4)),
        name="final_scale",
    )(x, ss, g.reshape(1, k))


def _rope(t, cos, sgn_sin):
    return t * cos + pltpu.roll(t, HEAD_DIM // 2, 1) * sgn_sin


def _row_index_f32():
    return lax.broadcasted_iota(jnp.int32, (CHUNK, HEAD_DIM), 0).astype(F32)


def _head(ref, off, h):
    return ref[:, off + h * HEAD_DIM: off + (h + 1) * HEAD_DIM]


BWD_CHUNKS_PER_STEP = 8


def _ret_bwd_state_kernel(k_ref, v_ref, cos_ref, sin_ref, raw_ref, sb_ref, st_ref):
    @pl.when(pl.program_id(1) == 0)
    def _():
        st_ref[...] = jnp.zeros_like(st_ref)

    row = _row_index_f32()
    scale = HEAD_DIM ** -0.5
    for c in reversed(range(BWD_CHUNKS_PER_STEP)):
        pos = slice(c * CHUNK, (c + 1) * CHUNK)
        cos = cos_ref[pos, :]
        sin = sin_ref[pos, :]
        for h in range(N_RET_HEADS):
            rows = slice(h * HEAD_DIM, (h + 1) * HEAD_DIM)
            cols = slice(h * HEAD_DIM, (h + 1) * HEAD_DIM)
            st = st_ref[rows, :]
            sb_ref[0, c, rows, :] = st.astype(BF16)
            lgb = -jnp.exp(raw_ref[N_RET_HEADS + h: N_RET_HEADS + h + 1, :])
            kh = _rope(k_ref[pos, cols], cos, sin) * scale
            kw = (kh * jnp.exp(lgb * row)).astype(BF16)
            kv = _dot_tn(kw, v_ref[pos, cols].astype(BF16))
            st_ref[rows, :] = jnp.exp(lgb * float(CHUNK)) * st + kv


def _ret_bwd_states(proj, cos, sgn_sin, raw_rows, *, batch, n_chunks):
    assert OFF_RK % RET_WIDTH == 0 and OFF_RV % RET_WIDTH == 0
    kblk, vblk = OFF_RK // RET_WIDTH, OFF_RV // RET_WIDTH
    per = BWD_CHUNKS_PER_STEP
    assert n_chunks % per == 0
    n_steps = n_chunks // per
    last = n_steps - 1
    return pl.pallas_call(
        _ret_bwd_state_kernel,
        grid=(batch, n_steps),
        in_specs=[
            pl.BlockSpec((per * CHUNK, RET_WIDTH), lambda b, t: (b * n_steps + last - t, kblk)),
            pl.BlockSpec((per * CHUNK, RET_WIDTH), lambda b, t: (b * n_steps + last - t, vblk)),
            pl.BlockSpec((per * CHUNK, HEAD_DIM), lambda b, t: (last - t, 0)),
            pl.BlockSpec((per * CHUNK, HEAD_DIM), lambda b, t: (last - t, 0)),
            pl.BlockSpec((2 * N_RET_HEADS, HEAD_DIM), lambda b, t: (0, 0)),
        ],
        out_specs=pl.BlockSpec((1, per, RET_WIDTH, HEAD_DIM), lambda b, t: (b, last - t, 0, 0)),
        out_shape=jax.ShapeDtypeStruct((batch, n_chunks, RET_WIDTH, HEAD_DIM), BF16),
        scratch_shapes=[pltpu.VMEM((RET_WIDTH, HEAD_DIM), F32)],
        compiler_params=pltpu.CompilerParams(
            dimension_semantics=("parallel", "arbitrary"),
            vmem_limit_bytes=_vmem_limit(4 * per * CHUNK * RET_WIDTH * 4 + (6 << 20))),
        name="ret_bwd_states",
    )(proj, proj, cos, sgn_sin, raw_rows)


def _sgu(p_ref, lng_ref, lnb_ref, ws_ref, bs_ref, o_ref):
    sqrt_half = np.sqrt(0.5).astype(np.float32)

    def gelu(t):
        return 0.5 * t * (1.0 + lax.erf(t * sqrt_half))

    u = gelu(p_ref[:, OFF_U:OFF_U + SGU_WIDTH])
    v = gelu(p_ref[:, OFF_V:OFF_V + SGU_WIDTH])
    mu = jnp.mean(v, axis=-1, keepdims=True)
    var = jnp.mean(jnp.square(v - mu), axis=-1, keepdims=True)
    vn = (v - mu) * lax.rsqrt(var + NORM_EPS) * lng_ref[...] + lnb_ref[...]
    for g in range(N_SGU_GROUPS):
        cols = slice(g * HEAD_DIM, (g + 1) * HEAD_DIM)
        s = _dot(ws_ref[g], vn[:, cols].astype(BF16)) + bs_ref[g]
        o_ref[:, cols] = (u[:, cols] * s).astype(o_ref.dtype)


N_RET_TABLES = 4


def _ret_table(tab_ref, h, t):
    base = (h * N_RET_TABLES + t) * CHUNK
    return tab_ref.at[base:base + CHUNK, :]


def _fill_ret_tables(raw_ref, tab_ref):
    row = _row_index_f32()
    col = lax.broadcasted_iota(jnp.int32, (CHUNK, CHUNK), 1).astype(F32)
    delta = row - col
    for h in range(N_RET_HEADS):
        lgf = -jnp.exp(raw_ref[h:h + 1, :])
        lgb = -jnp.exp(raw_ref[N_RET_HEADS + h: N_RET_HEADS + h + 1, :])
        _ret_table(tab_ref, h, 0)[...] = jnp.where(
            delta >= 0.0,
            jnp.exp(lgf * jnp.maximum(delta, 0.0)),
            jnp.exp(lgb * jnp.maximum(-delta, 0.0)))
        _ret_table(tab_ref, h, 1)[...] = jnp.exp(lgf * (float(CHUNK - 1) - row))
        _ret_table(tab_ref, h, 2)[...] = jnp.exp(lgf * (row + 1.0))
        _ret_table(tab_ref, h, 3)[...] = jnp.exp(lgb * (float(CHUNK) - row))


def _retention(p_ref, sb_ref, raw_ref, cos, sin, o_ref, sf_ref, tab_ref):
    scale = HEAD_DIM ** -0.5
    for h in range(N_RET_HEADS):
        rows = slice(h * HEAD_DIM, (h + 1) * HEAD_DIM)
        lgf = -jnp.exp(raw_ref[h:h + 1, :])
        q = _rope(_head(p_ref, OFF_RQ, h), cos, sin)
        k = _rope(_head(p_ref, OFF_RK, h), cos, sin) * scale
        v = _head(p_ref, OFF_RV, h).astype(BF16)
        scores = _dot_nt(q.astype(BF16), k.astype(BF16)) * _ret_table(tab_ref, h, 0)[...]
        inner = _dot(scores.astype(BF16), v)
        sf_prev = sf_ref[rows, :]
        kv_f = _dot_tn((k * _ret_table(tab_ref, h, 1)[...]).astype(BF16), v)
        cross_f = _dot((q * _ret_table(tab_ref, h, 2)[...]).astype(BF16), sf_prev.astype(BF16))
        sf_ref[rows, :] = jnp.exp(lgf * float(CHUNK)) * sf_prev + kv_f
        cross_b = _dot((q * _ret_table(tab_ref, h, 3)[...]).astype(BF16), sb_ref[0, 0, rows, :])
        rf = (inner + cross_f) + cross_b
        rn = rf * lax.rsqrt(jnp.mean(rf * rf, axis=-1, keepdims=True) + NORM_EPS)
        gate = _head(p_ref, OFF_RG, h)
        o_ref[:, SGU_WIDTH + h * HEAD_DIM: SGU_WIDTH + (h + 1) * HEAD_DIM] = (
            gate * jax.nn.sigmoid(gate) * rn).astype(o_ref.dtype)


def _attention(p_ref, kvp_ref, kvn_ref, cos3_ref, sin3_ref, sink_ref, o_ref, chunk, seq):
    cos3 = cos3_ref[...]
    sin3 = sin3_ref[...]
    q_scale = HEAD_DIM ** -0.5 * LOG2_E
    cos_q = cos3[CHUNK:2 * CHUNK] * q_scale
    sin_q = sin3[CHUNK:2 * CHUNK] * q_scale
    kj = lax.broadcasted_iota(jnp.int32, (3 * CHUNK, CHUNK), 0)
    qi = lax.broadcasted_iota(jnp.int32, (3 * CHUNK, CHUNK), 1)
    rel = kj - CHUNK - qi
    kpos = (chunk - 1) * CHUNK + kj
    valid = (jnp.abs(rel) <= WINDOW) & (kpos >= 0) & (kpos < seq)
    valid = jnp.concatenate([valid] * Q_PER_KV, axis=1)
    out_off = SGU_WIDTH + RET_WIDTH
    for kh in range(N_KV_HEADS):
        kcols = slice(kh * HEAD_DIM, (kh + 1) * HEAD_DIM)
        vcols = slice(KV_WIDTH + kh * HEAD_DIM, KV_WIDTH + (kh + 1) * HEAD_DIM)
        k3 = jnp.concatenate([kvp_ref[:, kcols], _head(p_ref, OFF_AK, kh), kvn_ref[:, kcols]], axis=0)
        v3 = jnp.concatenate([kvp_ref[:, vcols], _head(p_ref, OFF_AV, kh), kvn_ref[:, vcols]], axis=0)
        k3 = _rope(k3, cos3, sin3).astype(BF16)
        qs = jnp.concatenate(
            [_rope(_head(p_ref, OFF_AQ, kh * Q_PER_KV + g), cos_q, sin_q) for g in range(Q_PER_KV)],
            axis=0)
        sink = LOG2_E * jnp.concatenate(
            [sink_ref[kh * Q_PER_KV + g: kh * Q_PER_KV + g + 1, :] for g in range(Q_PER_KV)], axis=1)
        s = _dot_nt(k3, qs.astype(BF16))
        s = jnp.concatenate([
            jnp.where(valid[:CHUNK], s[:CHUNK], NEG_INF),
            s[CHUNK:2 * CHUNK],
            jnp.where(valid[2 * CHUNK:], s[2 * CHUNK:], NEG_INF)], axis=0)
        m = jnp.maximum(jnp.max(s, axis=0, keepdims=True), sink)
        p = jnp.exp2(s - m)
        inv = 1.0 / (jnp.sum(p, axis=0, keepdims=True) + jnp.exp2(sink - m))
        o_t = _dot_tn(v3.astype(BF16), (p * inv).astype(BF16)).astype(o_ref.dtype)
        for g in range(Q_PER_KV):
            hq = kh * Q_PER_KV + g
            o_ref[:, out_off + hq * HEAD_DIM: out_off + (hq + 1) * HEAD_DIM] = (
                o_t[:, g * CHUNK:(g + 1) * CHUNK].T)


def _mixer_kernel(p_ref, kvp_ref, kvn_ref, cos3_ref, sin3_ref, sb_ref, lng_ref, lnb_ref,
                  ws_ref, bs_ref, raw_ref, sink_ref, o_ref, sf_ref, tab_ref, *, seq):
    chunk = pl.program_id(1)

    @pl.when(chunk == 0)
    def _():
        sf_ref[...] = jnp.zeros_like(sf_ref)
        _fill_ret_tables(raw_ref, tab_ref)

    cos = cos3_ref[CHUNK:2 * CHUNK, :]
    sin = sin3_ref[CHUNK:2 * CHUNK, :]
    _attention(p_ref, kvp_ref, kvn_ref, cos3_ref, sin3_ref, sink_ref, o_ref, chunk, seq)
    _retention(p_ref, sb_ref, raw_ref, cos, sin, o_ref, sf_ref, tab_ref)
    _sgu(p_ref, lng_ref, lnb_ref, ws_ref, bs_ref, o_ref)


def _mixers(proj, sb, cos3, sin3, lng, lnb, ws, bias, raw_rows, sink_rows, *, batch, n_chunks):
    m = proj.shape[0]
    kv_blk = OFF_AK // (2 * KV_WIDTH)
    assert kv_blk * 2 * KV_WIDTH == OFF_AK
    last = n_chunks - 1
    const2 = lambda b, n: (0, 0)
    est = (2 * CHUNK * IN_WIDTH * 4 + 4 * CHUNK * 2 * KV_WIDTH * 4 + 2 * CHUNK * MIX_WIDTH * 2
           + (12 << 20))
    return pl.pallas_call(
        functools.partial(_mixer_kernel, seq=n_chunks * CHUNK),
        grid=(batch, n_chunks),
        in_specs=[
            pl.BlockSpec((CHUNK, IN_WIDTH), lambda b, n: (b * n_chunks + n, 0)),
            pl.BlockSpec((CHUNK, 2 * KV_WIDTH),
                         lambda b, n: (b * n_chunks + jnp.maximum(n - 1, 0), kv_blk)),
            pl.BlockSpec((CHUNK, 2 * KV_WIDTH),
                         lambda b, n: (b * n_chunks + jnp.minimum(n + 1, last), kv_blk)),
            pl.BlockSpec((3 * CHUNK, HEAD_DIM), lambda b, n: (n, 0)),
            pl.BlockSpec((3 * CHUNK, HEAD_DIM), lambda b, n: (n, 0)),
            pl.BlockSpec((1, 1, RET_WIDTH, HEAD_DIM), lambda b, n: (b, n, 0, 0)),
            pl.BlockSpec((1, SGU_WIDTH), const2),
            pl.BlockSpec((1, SGU_WIDTH), const2),
            pl.BlockSpec((N_SGU_GROUPS, CHUNK, CHUNK), lambda b, n: (0, 0, 0)),
            pl.BlockSpec((N_SGU_GROUPS, CHUNK, HEAD_DIM), lambda b, n: (0, 0, 0)),
            pl.BlockSpec((2 * N_RET_HEADS, HEAD_DIM), const2),
            pl.BlockSpec((N_Q_HEADS, HEAD_DIM), const2),
        ],
        out_specs=pl.BlockSpec((CHUNK, MIX_WIDTH), lambda b, n: (b * n_chunks + n, 0)),
        out_shape=jax.ShapeDtypeStruct((m, MIX_WIDTH), BF16),
        scratch_shapes=[pltpu.VMEM((RET_WIDTH, HEAD_DIM), F32),
                        pltpu.VMEM((N_RET_HEADS * N_RET_TABLES * CHUNK, HEAD_DIM), F32)],
        compiler_params=pltpu.CompilerParams(
            dimension_semantics=("arbitrary", "arbitrary"),
            vmem_limit_bytes=_vmem_limit(est)),
        name="mixers",
    )(proj, proj, proj, cos3, sin3, sb, lng, lnb, ws, bias, raw_rows, sink_rows)


def _rope_tables(seq):
    pos = jnp.arange(seq, dtype=F32)
    inv = ROPE_THETA ** (-jnp.arange(0, HEAD_DIM, 2, dtype=F32) / HEAD_DIM)
    ang = pos[:, None] * inv[None, :]
    ang = jnp.concatenate([ang, ang], axis=-1)
    sgn = jnp.where(jnp.arange(HEAD_DIM) < HEAD_DIM // 2, -1.0, 1.0).astype(F32)
    return jnp.cos(ang), jnp.sin(ang) * sgn[None, :]


def _three_chunk_view(t, n_chunks):
    seq = t.shape[0]
    pad = jnp.zeros((CHUNK, HEAD_DIM), t.dtype)
    tp = jnp.concatenate([pad, t, pad], axis=0)
    parts = [tp[s * CHUNK: s * CHUNK + seq].reshape(n_chunks, CHUNK, HEAD_DIM) for s in range(3)]
    return jnp.concatenate(parts, axis=1).reshape(n_chunks * 3 * CHUNK, HEAD_DIM)


def kernel(x, ln_mix_g, w_in, sgu_ln_g, sgu_ln_b, sgu_w, sgu_b, ret_log_decay, attn_sink,
           w_out, ln_mlp_g, w_up, w_down, final_norm_g):
    batch, seq, d_model = x.shape
    depth = w_in.shape[0]
    assert w_in.shape[2] == IN_WIDTH and w_out.shape[1] == MIX_WIDTH and seq % CHUNK == 0
    n_chunks = seq // CHUNK
    m = batch * seq

    cos, sgn_sin = _rope_tables(seq)
    cos3 = _three_chunk_view(cos, n_chunks)
    sin3 = _three_chunk_view(sgn_sin, n_chunks)

    next_gains = [ln_mix_g[l] for l in range(1, depth)] + [final_norm_g]

    w_in_b = _cast_layer(w_in, 0)
    x2 = x.reshape(m, d_model)
    xg, ss = _prenorm(x2, ln_mix_g[0])
    for l in range(depth):
        raw_rows = jnp.broadcast_to(
            ret_log_decay[l].astype(F32).reshape(2 * N_RET_HEADS, 1), (2 * N_RET_HEADS, HEAD_DIM))
        sink_rows = jnp.broadcast_to(
            attn_sink[l].astype(F32).reshape(N_Q_HEADS, 1), (N_Q_HEADS, HEAD_DIM))
        sgu_bias = jnp.broadcast_to(
            sgu_b[l].astype(F32)[:, :, None], (N_SGU_GROUPS, CHUNK, HEAD_DIM))
        proj, w_out_b = _scaled_matmul(
            xg, ss, w_in_b, relu2=False, out_dtype=F32, side_casts=((w_out, l),))
        sb = _ret_bwd_states(proj, cos, sgn_sin, raw_rows, batch=batch, n_chunks=n_chunks)
        mix = _mixers(proj, sb, cos3, sin3,
                      sgu_ln_g[l].reshape(1, SGU_WIDTH), sgu_ln_b[l].reshape(1, SGU_WIDTH),
                      sgu_w[l].astype(BF16), sgu_bias, raw_rows, sink_rows,
                      batch=batch, n_chunks=n_chunks)
        x2, ss, xg, *maybe_w_up = _matmul_res(mix, w_out_b, x2, ln_mlp_g[l], tm=1024, tn=512,
                                              tk_options=(MIX_WIDTH,),
                                              side_casts=((w_up, l),) if l == 0 else ())
        w_up_b = maybe_w_up[0] if maybe_w_up else w_up_b
        last = l + 1 == depth
        hid, w_down_b, *maybe_w_in = _scaled_matmul(
            xg, ss, w_up_b, relu2=True, out_dtype=BF16,
            side_casts=((w_down, l),) if last else ((w_down, l), (w_in, l + 1)))
        x2, ss, *for_next_layer = _matmul_res(hid, w_down_b, x2, next_gains[l], tm=1024, tn=1024,
                                             tk_options=(2048, 4096), emit_xg=not last,
                                             side_casts=() if last else ((w_up, l + 1),))
        if not last:
            w_in_b = maybe_w_in[0]
            xg, w_up_b = for_next_layer
    return _final_scale(x2, ss, final_norm_g).reshape(batch, seq, d_model)
```

```python
import functools

import numpy as np
import jax
import jax.numpy as jnp
from jax import lax
from jax.experimental import pallas as pl
from jax.experimental.pallas import tpu as pltpu

HEAD_DIM = 128
CHUNK = 128
WINDOW = 128
ROPE_THETA = 10000.0
NORM_EPS = 1e-5
NEG_INF = -1e30
LOG2_E = float(np.log2(np.e))
N_SGU_GROUPS = 8
N_RET_HEADS = 8
N_Q_HEADS = 16
N_KV_HEADS = 4
Q_PER_KV = N_Q_HEADS // N_KV_HEADS

SGU_WIDTH = N_SGU_GROUPS * HEAD_DIM
RET_WIDTH = N_RET_HEADS * HEAD_DIM
ATT_WIDTH = N_Q_HEADS * HEAD_DIM
KV_WIDTH = N_KV_HEADS * HEAD_DIM
OFF_U = 0
OFF_V = OFF_U + SGU_WIDTH
OFF_RQ = OFF_V + SGU_WIDTH
OFF_RK = OFF_RQ + RET_WIDTH
OFF_RV = OFF_RK + RET_WIDTH
OFF_RG = OFF_RV + RET_WIDTH
OFF_AQ = OFF_RG + RET_WIDTH
OFF_AK = OFF_AQ + ATT_WIDTH
OFF_AV = OFF_AK + KV_WIDTH
IN_WIDTH = OFF_AV + KV_WIDTH
MIX_WIDTH = SGU_WIDTH + RET_WIDTH + ATT_WIDTH

V7X_VMEM_BUDGET = 56 * 1024 * 1024
VMEM_PLAN_MARGIN = 4 * 1024 * 1024
CAST_SLAB_BYTES = 4 * 1024 * 1024
BF16_SUBLANES = 16
W_RING = 3

F32 = jnp.float32
BF16 = jnp.bfloat16


def _vmem_limit(est_bytes):
    assert est_bytes <= V7X_VMEM_BUDGET, est_bytes
    return min(max(int(est_bytes * 1.25), 16 << 20), V7X_VMEM_BUDGET)


def _dot(a, b):
    return jnp.dot(a, b, preferred_element_type=F32)


def _dot_nt(a, b):
    return lax.dot_general(a, b, (((1,), (1,)), ((), ())), preferred_element_type=F32)


def _dot_tn(a, b):
    return lax.dot_general(a, b, (((0,), (0,)), ((), ())), preferred_element_type=F32)


def _cast_kernel(w_ref, o_ref):
    o_ref[...] = w_ref[...].astype(o_ref.dtype)


def _cast_layer(w, layer):
    _, k, n = w.shape
    tr = 1
    while tr * 2 * n * 4 <= CAST_SLAB_BYTES and k % (tr * 2) == 0:
        tr *= 2
    return pl.pallas_call(
        _cast_kernel,
        grid=(k // tr,),
        in_specs=[pl.BlockSpec((None, tr, n), lambda i: (layer, i, 0))],
        out_specs=pl.BlockSpec((tr, n), lambda i: (i, 0)),
        out_shape=jax.ShapeDtypeStruct((k, n), BF16),
        compiler_params=pltpu.CompilerParams(
            dimension_semantics=("parallel",),
            vmem_limit_bytes=_vmem_limit(2 * tr * n * 6 + tr * n * 4)),
        name="cast_bf16",
    )(w)


def _side_cast_specs(side_casts, steps, step_index):
    in_specs, out_specs, out_shape, vmem = [], [], [], 0
    for w, layer in side_casts:
        _, rows, cols = w.shape
        n_slabs = 1
        while n_slabs * 2 <= steps and rows % (n_slabs * 2) == 0 and (rows // (n_slabs * 2)) % BF16_SUBLANES == 0:
            n_slabs *= 2
        slab = rows // n_slabs

        def slab_index(*ids, _last=n_slabs - 1):
            return jnp.minimum(step_index(*ids), _last)

        in_specs.append(pl.BlockSpec(
            (None, slab, cols), lambda *ids, _l=layer, _s=slab_index: (_l, _s(*ids), 0)))
        out_specs.append(pl.BlockSpec((slab, cols), lambda *ids, _s=slab_index: (_s(*ids), 0)))
        out_shape.append(jax.ShapeDtypeStruct((rows, cols), BF16))
        vmem += 2 * slab * cols * 6 + slab * cols * 4
    return in_specs, out_specs, out_shape, vmem


def _row_rsqrt(ss, width):
    return lax.rsqrt(ss / float(width) + NORM_EPS)


def _prenorm_kernel(x_ref, g_ref, xg_ref, ss_ref):
    x = x_ref[...]
    xg_ref[...] = (x * g_ref[...]).astype(BF16)
    ss_ref[...] = jnp.sum(x * x, axis=-1, keepdims=True)


def _prenorm(x, g, *, tm=512):
    m, k = x.shape
    return pl.pallas_call(
        _prenorm_kernel,
        grid=(m // tm,),
        in_specs=[pl.BlockSpec((tm, k), lambda i: (i, 0)),
                  pl.BlockSpec((1, k), lambda i: (0, 0))],
        out_specs=[pl.BlockSpec((tm, k), lambda i: (i, 0)),
                   pl.BlockSpec((tm, 1), lambda i: (i, 0))],
        out_shape=[jax.ShapeDtypeStruct((m, k), BF16), jax.ShapeDtypeStruct((m, 1), F32)],
        compiler_params=pltpu.CompilerParams(
            dimension_semantics=("parallel",),
            vmem_limit_bytes=_vmem_limit(2 * tm * k * 6 + 2 * tm * k * 4)),
        name="prenorm",
    )(x, g.reshape(1, k))


def _scaled_matmul_kernel(a_ref, ss_ref, w_ref, *rest, relu2, n_side):
    side_in, o_ref, side_out = rest[:n_side], rest[n_side], rest[n_side + 1:]
    acc = _dot(a_ref[...], w_ref[...]) * _row_rsqrt(ss_ref[...], a_ref.shape[1])
    if relu2:
        acc = jnp.square(jnp.maximum(acc, 0.0))
    o_ref[...] = acc.astype(o_ref.dtype)
    for src_ref, dst_ref in zip(side_in, side_out):
        dst_ref[...] = src_ref[...].astype(dst_ref.dtype)


def _scaled_matmul(a, ss, w, *, relu2, out_dtype, side_casts=(), tm=1024, tn=1024):
    m, k = a.shape
    n = w.shape[1]
    assert m % tm == 0 and n % tn == 0
    nj = n // tn
    out_bytes = jnp.dtype(out_dtype).itemsize
    side_in, side_out, side_shape, side_vmem = _side_cast_specs(
        side_casts, (m // tm) * nj, lambda i, j: i * nj + j)
    est = 2 * tm * k * 2 + 2 * k * tn * 2 + 2 * tm * tn * out_bytes + 2 * tm * tn * 4 + side_vmem
    return pl.pallas_call(
        functools.partial(_scaled_matmul_kernel, relu2=relu2, n_side=len(side_casts)),
        grid=(m // tm, nj),
        in_specs=[
            pl.BlockSpec((tm, k), lambda i, j: (i, 0)),
            pl.BlockSpec((tm, 1), lambda i, j: (i, 0)),
            pl.BlockSpec((k, tn), lambda i, j: (0, j)),
            *side_in,
        ],
        out_specs=[pl.BlockSpec((tm, tn), lambda i, j: (i, j)), *side_out],
        out_shape=[jax.ShapeDtypeStruct((m, n), out_dtype), *side_shape],
        compiler_params=pltpu.CompilerParams(
            dimension_semantics=("arbitrary", "arbitrary"),
            vmem_limit_bytes=_vmem_limit(est)),
        name="scaled_matmul_relu2" if relu2 else "scaled_matmul",
    )(a, ss, w, *[src for src, _ in side_casts])


def _matmul_res_kernel(a_ref, w_ref, res_ref, g_ref, *refs, nk, emit_xg, res_by_dma, n_side):
    side_in, (o_ref, ss_ref), rest = refs[:n_side], refs[n_side:n_side + 2], refs[n_side + 2:]
    maybe_xg_ref = rest[0] if emit_xg else None
    side_out = rest[emit_xg:emit_xg + n_side]
    i = pl.program_id(0)
    j = pl.program_id(1)
    kk = pl.program_id(2)
    tm, tn = o_ref.shape

    @pl.when(jnp.logical_and(j == 0, kk == 0))
    def _():
        ss_ref[...] = jnp.zeros_like(ss_ref)

    def emit_norm_inputs(rows, x_new):
        if maybe_xg_ref is not None:
            maybe_xg_ref[rows, :] = (x_new * g_ref[...]).astype(BF16)
        ss_ref[rows, :] += jnp.sum(x_new * x_new, axis=-1, keepdims=True)

    def cast_side_slabs():
        for src_ref, dst_ref in zip(side_in, side_out):
            dst_ref[...] = src_ref[...].astype(dst_ref.dtype)

    if res_by_dma:
        weight_tile = lambda: w_ref[...]
    else:
        w_buf, w_sems = rest[-2:]
        tk = w_buf.shape[1]
        nj = pl.num_programs(1)
        step = (i * nj + j) * nk + kk

        def w_copy(s):
            rows = pl.ds(pl.multiple_of((s % nk) * tk, tk), tk)
            cols = pl.ds(pl.multiple_of(((s // nk) % nj) * tn, tn), tn)
            return pltpu.make_async_copy(w_ref.at[rows, cols], w_buf.at[s % W_RING],
                                         w_sems.at[s % W_RING])

        @pl.when(step == 0)
        def _():
            for s in range(W_RING - 1):
                w_copy(s).start()

        @pl.when(step + (W_RING - 1) < pl.num_programs(0) * nj * nk)
        def _():
            w_copy(step + (W_RING - 1)).start()

        w_copy(step).wait()
        weight_tile = lambda: w_buf[step % W_RING]

    if nk == 1:
        x_new = res_ref[...] + _dot(a_ref[...], weight_tile())
        o_ref[...] = x_new
        emit_norm_inputs(slice(None), x_new)
        cast_side_slabs()
        return

    acc_ref = o_ref if res_by_dma else rest[-3]

    def res_copy():
        rows = pl.ds(pl.multiple_of(i * tm, tm), tm)
        cols = pl.ds(pl.multiple_of(j * tn, tn), tn)
        return pltpu.make_async_copy(res_ref.at[rows, cols], o_ref, rest[-1])

    @pl.when(kk == 0)
    def _():
        if res_by_dma:
            res_copy().start()
            part = _dot(a_ref[...], weight_tile())
            res_copy().wait()
            acc_ref[...] += part
        else:
            acc_ref[...] = res_ref[...] + _dot(a_ref[...], weight_tile())
        cast_side_slabs()

    @pl.when(jnp.logical_and(kk > 0, kk < nk - 1))
    def _():
        acc_ref[...] += _dot(a_ref[...], weight_tile())
        cast_side_slabs()

    @pl.when(kk == nk - 1)
    def _():
        x_new = acc_ref[...] + _dot(a_ref[...], weight_tile())
        o_ref[...] = x_new
        emit_norm_inputs(slice(None), x_new)
        cast_side_slabs()


def _plan_matmul_res(tm, tn, k, tk_options, emit_xg):
    for tk in sorted(tk_options, reverse=True):
        for res_by_dma in (False, True):
            if res_by_dma and tk == k:
                continue
            est = (2 * tm * tk * 2 + 2 * tk * tn * 2 + 2 * tm * tn * (4 + 2 * emit_xg)
                   + (0 if res_by_dma else 2 * tm * tn * 4 + (tk < k) * tm * tn * 4)
                   + 5 * tm * tn * 2)
            if est + VMEM_PLAN_MARGIN <= V7X_VMEM_BUDGET:
                return tk, res_by_dma, est
    raise ValueError("no residual-matmul tiling fits VMEM")


def _matmul_res(a, w, res, g_next, *, tm, tn, tk_options, emit_xg=True, side_casts=()):
    m, k = a.shape
    n = w.shape[1]
    tk, res_by_dma, est = _plan_matmul_res(tm, tn, k, tk_options, emit_xg)
    assert m % tm == 0 and n % tn == 0 and k % tk == 0
    nk, nj = k // tk, n // tn
    side_in, side_out, side_shape, side_vmem = _side_cast_specs(
        side_casts, (m // tm) * nj * nk, lambda i, j, kk: (i * nj + j) * nk + kk)
    est += side_vmem
    w_ring = not res_by_dma
    if res_by_dma:
        scratch = [pltpu.SemaphoreType.DMA(())]
    else:
        scratch = ([pltpu.VMEM((tm, tn), F32)] if nk > 1 else []) + [
            pltpu.VMEM((W_RING, tk, tn), BF16), pltpu.SemaphoreType.DMA((W_RING,))]
        est += (W_RING - 2) * tk * tn * 2
    tile = pl.BlockSpec((tm, tn), lambda i, j, kk: (i, j))
    out_specs = [tile, pl.BlockSpec((tm, 1), lambda i, j, kk: (i, 0))]
    out_shape = [jax.ShapeDtypeStruct((m, n), F32), jax.ShapeDtypeStruct((m, 1), F32)]
    if emit_xg:
        out_specs.append(tile)
        out_shape.append(jax.ShapeDtypeStruct((m, n), BF16))
    return pl.pallas_call(
        functools.partial(_matmul_res_kernel, nk=nk, emit_xg=emit_xg, res_by_dma=res_by_dma,
                          n_side=len(side_casts)),
        grid=(m // tm, nj, nk),
        in_specs=[
            pl.BlockSpec((tm, tk), lambda i, j, kk: (i, kk)),
            (pl.BlockSpec(memory_space=pl.ANY) if w_ring
             else pl.BlockSpec((tk, tn), lambda i, j, kk: (kk, j))),
            pl.BlockSpec(memory_space=pl.ANY) if res_by_dma else tile,
            pl.BlockSpec((1, tn), lambda i, j, kk: (0, j)),
            *side_in,
        ],
        out_specs=[*out_specs, *side_out],
        out_shape=[*out_shape, *side_shape],
        scratch_shapes=scratch,
        compiler_params=pltpu.CompilerParams(
            dimension_semantics=("arbitrary", "arbitrary", "arbitrary"),
            vmem_limit_bytes=_vmem_limit(est)),
        name="matmul_res_k%d" % nk,
    )(a, w, res, g_next.reshape(1, n), *[src for src, _ in side_casts])


def _final_scale_kernel(x_ref, ss_ref, g_ref, o_ref):
    o_ref[...] = x_ref[...] * _row_rsqrt(ss_ref[...], x_ref.shape[1]) * g_ref[...]


def _final_scale(x, ss, g, *, tm=512):
    m, k = x.shape
    return pl.pallas_call(
        _final_scale_kernel,
        grid=(m // tm,),
        in_specs=[pl.BlockSpec((tm, k), lambda i: (i, 0)),
                  pl.BlockSpec((tm, 1), lambda i: (i, 0)),
                  pl.BlockSpec((1, k), lambda i: (0, 0))],
        out_specs=pl.BlockSpec((tm, k), lambda i: (i, 0)),
        out_shape=jax.ShapeDtypeStruct((m, k), F32),
        compiler_params=pltpu.CompilerParams(
            dimension_semantics=("parallel",),
            vmem_limit_bytes=_vmem_limit(5 * tm * k * 4)),
        name="final_scale",
    )(x, ss, g.reshape(1, k))


def _rope(t, cos, sgn_sin):
    return t * cos + pltpu.roll(t, HEAD_DIM // 2, 1) * sgn_sin


def _row_index_f32():
    return lax.broadcasted_iota(jnp.int32, (CHUNK, HEAD_DIM), 0).astype(F32)


def _head(ref, off, h):
    return ref[:, off + h * HEAD_DIM: off + (h + 1) * HEAD_DIM]


BWD_CHUNKS_PER_STEP = 8


def _ret_bwd_state_kernel(k_ref, v_ref, cos_ref, sin_ref, raw_ref, sb_ref, st_ref):
    @pl.when(pl.program_id(1) == 0)
    def _():
        st_ref[...] = jnp.zeros_like(st_ref)

    row = _row_index_f32()
    scale = HEAD_DIM ** -0.5
    for c in reversed(range(BWD_CHUNKS_PER_STEP)):
        pos = slice(c * CHUNK, (c + 1) * CHUNK)
        cos = cos_ref[pos, :]
        sin = sin_ref[pos, :]
        for h in range(N_RET_HEADS):
            rows = slice(h * HEAD_DIM, (h + 1) * HEAD_DIM)
            cols = slice(h * HEAD_DIM, (h + 1) * HEAD_DIM)
            st = st_ref[rows, :]
            sb_ref[0, c, rows, :] = st.astype(BF16)
            lgb = -jnp.exp(raw_ref[N_RET_HEADS + h: N_RET_HEADS + h + 1, :])
            kh = _rope(k_ref[pos, cols], cos, sin) * scale
            kw = (kh * jnp.exp(lgb * row)).astype(BF16)
            kv = _dot_tn(kw, v_ref[pos, cols].astype(BF16))
            st_ref[rows, :] = jnp.exp(lgb * float(CHUNK)) * st + kv


def _ret_bwd_states(proj, cos, sgn_sin, raw_rows, *, batch, n_chunks):
    assert OFF_RK % RET_WIDTH == 0 and OFF_RV % RET_WIDTH == 0
    kblk, vblk = OFF_RK // RET_WIDTH, OFF_RV // RET_WIDTH
    per = BWD_CHUNKS_PER_STEP
    assert n_chunks % per == 0
    n_steps = n_chunks // per
    last = n_steps - 1
    return pl.pallas_call(
        _ret_bwd_state_kernel,
        grid=(batch, n_steps),
        in_specs=[
            pl.BlockSpec((per * CHUNK, RET_WIDTH), lambda b, t: (b * n_steps + last - t, kblk)),
            pl.BlockSpec((per * CHUNK, RET_WIDTH), lambda b, t: (b * n_steps + last - t, vblk)),
            pl.BlockSpec((per * CHUNK, HEAD_DIM), lambda b, t: (last - t, 0)),
            pl.BlockSpec((per * CHUNK, HEAD_DIM), lambda b, t: (last - t, 0)),
            pl.BlockSpec((2 * N_RET_HEADS, HEAD_DIM), lambda b, t: (0, 0)),
        ],
        out_specs=pl.BlockSpec((1, per, RET_WIDTH, HEAD_DIM), lambda b, t: (b, last - t, 0, 0)),
        out_shape=jax.ShapeDtypeStruct((batch, n_chunks, RET_WIDTH, HEAD_DIM), BF16),
        scratch_shapes=[pltpu.VMEM((RET_WIDTH, HEAD_DIM), F32)],
        compiler_params=pltpu.CompilerParams(
            dimension_semantics=("parallel", "arbitrary"),
            vmem_limit_bytes=_vmem_limit(4 * per * CHUNK * RET_WIDTH * 4 + (6 << 20))),
        name="ret_bwd_states",
    )(proj, proj, cos, sgn_sin, raw_rows)


def _sgu(p_ref, lng_ref, lnb_ref, ws_ref, bs_ref, o_ref):
    sqrt_half = np.sqrt(0.5).astype(np.float32)

    def gelu(t):
        return 0.5 * t * (1.0 + lax.erf(t * sqrt_half))

    u = gelu(p_ref[:, OFF_U:OFF_U + SGU_WIDTH])
    v = gelu(p_ref[:, OFF_V:OFF_V + SGU_WIDTH])
    mu = jnp.mean(v, axis=-1, keepdims=True)
    var = jnp.mean(jnp.square(v - mu), axis=-1, keepdims=True)
    vn = (v - mu) * lax.rsqrt(var + NORM_EPS) * lng_ref[...] + lnb_ref[...]
    for g in range(N_SGU_GROUPS):
        cols = slice(g * HEAD_DIM, (g + 1) * HEAD_DIM)
        s = _dot(ws_ref[g], vn[:, cols].astype(BF16)) + bs_ref[g]
        o_ref[:, cols] = (u[:, cols] * s).astype(o_ref.dtype)


N_RET_TABLES = 4


def _ret_table(tab_ref, h, t):
    base = (h * N_RET_TABLES + t) * CHUNK
    return tab_ref.at[base:base + CHUNK, :]


def _fill_ret_tables(raw_ref, tab_ref):
    row = _row_index_f32()
    col = lax.broadcasted_iota(jnp.int32, (CHUNK, CHUNK), 1).astype(F32)
    delta = row - col
    for h in range(N_RET_HEADS):
        lgf = -jnp.exp(raw_ref[h:h + 1, :])
        lgb = -jnp.exp(raw_ref[N_RET_HEADS + h: N_RET_HEADS + h + 1, :])
        _ret_table(tab_ref, h, 0)[...] = jnp.where(
            delta >= 0.0,
            jnp.exp(lgf * jnp.maximum(delta, 0.0)),
            jnp.exp(lgb * jnp.maximum(-delta, 0.0)))
        _ret_table(tab_ref, h, 1)[...] = jnp.exp(lgf * (float(CHUNK - 1) - row))
        _ret_table(tab_ref, h, 2)[...] = jnp.exp(lgf * (row + 1.0))
        _ret_table(tab_ref, h, 3)[...] = jnp.exp(lgb * (float(CHUNK) - row))


def _retention(p_ref, sb_ref, raw_ref, cos, sin, o_ref, sf_ref, tab_ref):
    scale = HEAD_DIM ** -0.5
    for h in range(N_RET_HEADS):
        rows = slice(h * HEAD_DIM, (h + 1) * HEAD_DIM)
        lgf = -jnp.exp(raw_ref[h:h + 1, :])
        q = _rope(_head(p_ref, OFF_RQ, h), cos, sin)
        k = _rope(_head(p_ref, OFF_RK, h), cos, sin) * scale
        v = _head(p_ref, OFF_RV, h).astype(BF16)
        scores = _dot_nt(q.astype(BF16), k.astype(BF16)) * _ret_table(tab_ref, h, 0)[...]
        inner = _dot(scores.astype(BF16), v)
        sf_prev = sf_ref[rows, :]
        kv_f = _dot_tn((k * _ret_table(tab_ref, h, 1)[...]).astype(BF16), v)
        cross_f = _dot((q * _ret_table(tab_ref, h, 2)[...]).astype(BF16), sf_prev.astype(BF16))
        sf_ref[rows, :] = jnp.exp(lgf * float(CHUNK)) * sf_prev + kv_f
        cross_b = _dot((q * _ret_table(tab_ref, h, 3)[...]).astype(BF16), sb_ref[0, 0, rows, :])
        rf = (inner + cross_f) + cross_b
        rn = rf * lax.rsqrt(jnp.mean(rf * rf, axis=-1, keepdims=True) + NORM_EPS)
        gate = _head(p_ref, OFF_RG, h)
        o_ref[:, SGU_WIDTH + h * HEAD_DIM: SGU_WIDTH + (h + 1) * HEAD_DIM] = (
            gate * jax.nn.sigmoid(gate) * rn).astype(o_ref.dtype)


def _attention(p_ref, kvp_ref, kvn_ref, cos3_ref, sin3_ref, sink_ref, o_ref, chunk, seq):
    cos3 = cos3_ref[...]
    sin3 = sin3_ref[...]
    q_scale = HEAD_DIM ** -0.5 * LOG2_E
    cos_q = cos3[CHUNK:2 * CHUNK] * q_scale
    sin_q = sin3[CHUNK:2 * CHUNK] * q_scale
    kj = lax.broadcasted_iota(jnp.int32, (3 * CHUNK, CHUNK), 0)
    qi = lax.broadcasted_iota(jnp.int32, (3 * CHUNK, CHUNK), 1)
    rel = kj - CHUNK - qi
    kpos = (chunk - 1) * CHUNK + kj
    valid = (jnp.abs(rel) <= WINDOW) & (kpos >= 0) & (kpos < seq)
    valid = jnp.concatenate([valid] * Q_PER_KV, axis=1)
    out_off = SGU_WIDTH + RET_WIDTH
    for kh in range(N_KV_HEADS):
        kcols = slice(kh * HEAD_DIM, (kh + 1) * HEAD_DIM)
        vcols = slice(KV_WIDTH + kh * HEAD_DIM, KV_WIDTH + (kh + 1) * HEAD_DIM)
        k3 = jnp.concatenate([kvp_ref[:, kcols], _head(p_ref, OFF_AK, kh), kvn_ref[:, kcols]], axis=0)
        v3 = jnp.concatenate([kvp_ref[:, vcols], _head(p_ref, OFF_AV, kh), kvn_ref[:, vcols]], axis=0)
        k3 = _rope(k3, cos3, sin3).astype(BF16)
        qs = jnp.concatenate(
            [_rope(_head(p_ref, OFF_AQ, kh * Q_PER_KV + g), cos_q, sin_q) for g in range(Q_PER_KV)],
            axis=0)
        sink = LOG2_E * jnp.concatenate(
            [sink_ref[kh * Q_PER_KV + g: kh * Q_PER_KV + g + 1, :] for g in range(Q_PER_KV)], axis=1)
        s = _dot_nt(k3, qs.astype(BF16))
        s = jnp.concatenate([
            jnp.where(valid[:CHUNK], s[:CHUNK], NEG_INF),
            s[CHUNK:2 * CHUNK],
            jnp.where(valid[2 * CHUNK:], s[2 * CHUNK:], NEG_INF)], axis=0)
        m = jnp.maximum(jnp.max(s, axis=0, keepdims=True), sink)
        p = jnp.exp2(s - m)
        inv = 1.0 / (jnp.sum(p, axis=0, keepdims=True) + jnp.exp2(sink - m))
        o_t = _dot_tn(v3.astype(BF16), (p * inv).astype(BF16)).astype(o_ref.dtype)
        for g in range(Q_PER_KV):
            hq = kh * Q_PER_KV + g
            o_ref[:, out_off + hq * HEAD_DIM: out_off + (hq + 1) * HEAD_DIM] = (
                o_t[:, g * CHUNK:(g + 1) * CHUNK].T)


def _mixer_kernel(p_ref, kvp_ref, kvn_ref, cos3_ref, sin3_ref, sb_ref, lng_ref, lnb_ref,
                  ws_ref, bs_ref, raw_ref, sink_ref, o_ref, sf_ref, tab_ref, *, seq):
    chunk = pl.program_id(1)

    @pl.when(chunk == 0)
    def _():
        sf_ref[...] = jnp.zeros_like(sf_ref)
        _fill_ret_tables(raw_ref, tab_ref)

    cos = cos3_ref[CHUNK:2 * CHUNK, :]
    sin = sin3_ref[CHUNK:2 * CHUNK, :]
    _attention(p_ref, kvp_ref, kvn_ref, cos3_ref, sin3_ref, sink_ref, o_ref, chunk, seq)
    _retention(p_ref, sb_ref, raw_ref, cos, sin, o_ref, sf_ref, tab_ref)
    _sgu(p_ref, lng_ref, lnb_ref, ws_ref, bs_ref, o_ref)


def _mixers(proj, sb, cos3, sin3, lng, lnb, ws, bias, raw_rows, sink_rows, *, batch, n_chunks):
    m = proj.shape[0]
    kv_blk = OFF_AK // (2 * KV_WIDTH)
    assert kv_blk * 2 * KV_WIDTH == OFF_AK
    last = n_chunks - 1
    const2 = lambda b, n: (0, 0)
    est = (2 * CHUNK * IN_WIDTH * 4 + 4 * CHUNK * 2 * KV_WIDTH * 4 + 2 * CHUNK * MIX_WIDTH * 2
           + (12 << 20))
    return pl.pallas_call(
        functools.partial(_mixer_kernel, seq=n_chunks * CHUNK),
        grid=(batch, n_chunks),
        in_specs=[
            pl.BlockSpec((CHUNK, IN_WIDTH), lambda b, n: (b * n_chunks + n, 0)),
            pl.BlockSpec((CHUNK, 2 * KV_WIDTH),
                         lambda b, n: (b * n_chunks + jnp.maximum(n - 1, 0), kv_blk)),
            pl.BlockSpec((CHUNK, 2 * KV_WIDTH),
                         lambda b, n: (b * n_chunks + jnp.minimum(n + 1, last), kv_blk)),
            pl.BlockSpec((3 * CHUNK, HEAD_DIM), lambda b, n: (n, 0)),
            pl.BlockSpec((3 * CHUNK, HEAD_DIM), lambda b, n: (n, 0)),
            pl.BlockSpec((1, 1, RET_WIDTH, HEAD_DIM), lambda b, n: (b, n, 0, 0)),
            pl.BlockSpec((1, SGU_WIDTH), const2),
            pl.BlockSpec((1, SGU_WIDTH), const2),
            pl.BlockSpec((N_SGU_GROUPS, CHUNK, CHUNK), lambda b, n: (0, 0, 0)),
            pl.BlockSpec((N_SGU_GROUPS, CHUNK, HEAD_DIM), lambda b, n: (0, 0, 0)),
            pl.BlockSpec((2 * N_RET_HEADS, HEAD_DIM), const2),
            pl.BlockSpec((N_Q_HEADS, HEAD_DIM), const2),
        ],
        out_specs=pl.BlockSpec((CHUNK, MIX_WIDTH), lambda b, n: (b * n_chunks + n, 0)),
        out_shape=jax.ShapeDtypeStruct((m, MIX_WIDTH), BF16),
        scratch_shapes=[pltpu.VMEM((RET_WIDTH, HEAD_DIM), F32),
                        pltpu.VMEM((N_RET_HEADS * N_RET_TABLES * CHUNK, HEAD_DIM), F32)],
        compiler_params=pltpu.CompilerParams(
            dimension_semantics=("arbitrary", "arbitrary"),
            vmem_limit_bytes=_vmem_limit(est)),
        name="mixers",
    )(proj, proj, proj, cos3, sin3, sb, lng, lnb, ws, bias, raw_rows, sink_rows)


def _rope_tables(seq):
    pos = jnp.arange(seq, dtype=F32)
    inv = ROPE_THETA ** (-jnp.arange(0, HEAD_DIM, 2, dtype=F32) / HEAD_DIM)
    ang = pos[:, None] * inv[None, :]
    ang = jnp.concatenate([ang, ang], axis=-1)
    sgn = jnp.where(jnp.arange(HEAD_DIM) < HEAD_DIM // 2, -1.0, 1.0).astype(F32)
    return jnp.cos(ang), jnp.sin(ang) * sgn[None, :]


def _three_chunk_view(t, n_chunks):
    seq = t.shape[0]
    pad = jnp.zeros((CHUNK, HEAD_DIM), t.dtype)
    tp = jnp.concatenate([pad, t, pad], axis=0)
    parts = [tp[s * CHUNK: s * CHUNK + seq].reshape(n_chunks, CHUNK, HEAD_DIM) for s in range(3)]
    return jnp.concatenate(parts, axis=1).reshape(n_chunks * 3 * CHUNK, HEAD_DIM)


def kernel(x, ln_mix_g, w_in, sgu_ln_g, sgu_ln_b, sgu_w, sgu_b, ret_log_decay, attn_sink,
           w_out, ln_mlp_g, w_up, w_down, final_norm_g):
    batch, seq, d_model = x.shape
    depth = w_in.shape[0]
    assert w_in.shape[2] == IN_WIDTH and w_out.shape[1] == MIX_WIDTH and seq % CHUNK == 0
    n_chunks = seq // CHUNK
    m = batch * seq

    cos, sgn_sin = _rope_tables(seq)
    cos3 = _three_chunk_view(cos, n_chunks)
    sin3 = _three_chunk_view(sgn_sin, n_chunks)

    next_gains = [ln_mix_g[l] for l in range(1, depth)] + [final_norm_g]

    w_in_b = _cast_layer(w_in, 0)
    x2 = x.reshape(m, d_model)
    xg, ss = _prenorm(x2, ln_mix_g[0])
    for l in range(depth):
        raw_rows = jnp.broadcast_to(
            ret_log_decay[l].astype(F32).reshape(2 * N_RET_HEADS, 1), (2 * N_RET_HEADS, HEAD_DIM))
        sink_rows = jnp.broadcast_to(
            attn_sink[l].astype(F32).reshape(N_Q_HEADS, 1), (N_Q_HEADS, HEAD_DIM))
        sgu_bias = jnp.broadcast_to(
            sgu_b[l].astype(F32)[:, :, None], (N_SGU_GROUPS, CHUNK, HEAD_DIM))
        proj, w_out_b = _scaled_matmul(
            xg, ss, w_in_b, relu2=False, out_dtype=F32, side_casts=((w_out, l),))
        sb = _ret_bwd_states(proj, cos, sgn_sin, raw_rows, batch=batch, n_chunks=n_chunks)
        mix = _mixers(proj, sb, cos3, sin3,
                      sgu_ln_g[l].reshape(1, SGU_WIDTH), sgu_ln_b[l].reshape(1, SGU_WIDTH),
                      sgu_w[l].astype(BF16), sgu_bias, raw_rows, sink_rows,
                      batch=batch, n_chunks=n_chunks)
        x2, ss, xg, w_up_b = _matmul_res(mix, w_out_b, x2, ln_mlp_g[l], tm=1024, tn=512,
                                         tk_options=(MIX_WIDTH,), side_casts=((w_up, l),))
        last = l + 1 == depth
        hid, w_down_b, *maybe_w_in = _scaled_matmul(
            xg, ss, w_up_b, relu2=True, out_dtype=BF16,
            side_casts=((w_down, l),) if last else ((w_down, l), (w_in, l + 1)))
        x2, ss, *maybe_xg = _matmul_res(hid, w_down_b, x2, next_gains[l], tm=1024, tn=1024,
                                       tk_options=(2048, 4096), emit_xg=not last)
        if not last:
            w_in_b, xg = maybe_w_in[0], maybe_xg[0]
    return _final_scale(x2, ss, final_norm_g).reshape(batch, seq, d_model)
```
